```python
import math
import jax, jax.numpy as jnp
from jax import lax
import numpy as np

D_MODEL = 1024
BATCH = 4
SEQ = 8192
DEPTH = 1

PLE_DIM = 256
RET_HEADS = 8
RET_QK = D_MODEL // RET_HEADS
RET_V = 2 * RET_QK
RET_QKW = RET_HEADS * RET_QK
RET_VW = RET_HEADS * RET_V
RET_CHUNK = 128
XPOS_BASE = 10000.0
DIFF_HEADS = 8
DIFF_HD = D_MODEL // DIFF_HEADS // 2
DIFF_QW = 2 * DIFF_HEADS * DIFF_HD
DIFF_VW = DIFF_HEADS * 2 * DIFF_HD
Q_BLOCK = 128
REL_BUCKETS = 32
REL_MAX_DIST = 128
N_BRANCH = 2
SPLITS = (RET_QKW, RET_QKW, RET_VW, RET_VW, DIFF_QW, DIFF_QW, DIFF_VW, N_BRANCH * D_MODEL)
IN_COLS = sum(SPLITS)
N_EXPERTS = 256
TOP_K = 8
N_GROUPS = 8
TOPK_GROUPS = 4
D_EXPERT = 256
D_SHARED = 256
ROUTED_SCALE = 2.5
EXPERT_BLOCK = 128
DN_ALPHA = (2.0 * DEPTH) ** 0.25
DN_BETA = (8.0 * DEPTH) ** -0.25
EPS = 1e-5

kernel_name = "hybrid_retention_diffattn_moe_block"


def layer_norm(x, g, b):
    xf = x.astype(jnp.float32)
    mu = jnp.mean(xf, -1, keepdims=True)
    var = jnp.mean(jnp.square(xf - mu), -1, keepdims=True)
    return ((xf - mu) * lax.rsqrt(var + EPS) * g + b).astype(x.dtype)


def swiglu(h, wg, wu, wd):
    return (jax.nn.silu(h @ wg) * (h @ wu)) @ wd


def xpos_rotate(t, pos):
    half = t.shape[-1] // 2
    theta = 1.0 / (XPOS_BASE ** jnp.linspace(0.0, 1.0, half, dtype=jnp.float32))
    ang = pos[:, None] * theta[None, :]
    cos = jnp.cos(ang)[None, :, None, :]
    sin = jnp.sin(ang)[None, :, None, :]
    t1, t2 = t[..., :half], t[..., half:]
    return jnp.concatenate([t1 * cos - t2 * sin, t2 * cos + t1 * sin], axis=-1)


def retention_chunkwise(q, k, v):
    b, s, h, dk = q.shape
    dv = v.shape[-1]
    c = RET_CHUNK
    n = s // c
    log_gamma = jnp.log(1.0 - 2.0 ** (-5.0 - jnp.arange(h, dtype=jnp.float32)))
    idx = jnp.arange(c, dtype=jnp.float32)
    rel = idx[:, None] - idx[None, :]
    inner = jnp.where(rel >= 0, jnp.exp(jnp.maximum(rel, 0.0)[None] * log_gamma[:, None, None]), 0.0)
    q_dec = jnp.exp((idx + 1.0)[None, :] * log_gamma[:, None])
    k_dec = jnp.exp((c - 1.0 - idx)[None, :] * log_gamma[:, None])
    c_dec = jnp.exp(c * log_gamma)
    k = k * dk ** -0.5

    def to_chunks(t):
        return t.reshape(b, n, c, h, t.shape[-1]).transpose(1, 0, 3, 2, 4)

    def step(state, inp):
        qc, kc, vc = inp
        scores = jnp.einsum('bhid,bhjd->bhij', qc, kc) * inner
        o = (jnp.einsum('bhij,bhjv->bhiv', scores, vc)
             + jnp.einsum('bhid,bhdv->bhiv', qc * q_dec[..., None], state))
        state = c_dec[:, None, None] * state + jnp.einsum('bhjd,bhjv->bhdv', kc * k_dec[..., None], vc)
        return state, o

    state0 = jnp.zeros((b, h, dk, dv), jnp.float32)
    _, o = lax.scan(step, state0, (to_chunks(q), to_chunks(k), to_chunks(v)))
    return o.transpose(1, 0, 3, 2, 4).reshape(b, s, h, dv)


def t5_bucket(n):
    max_exact = REL_BUCKETS // 2
    nf = jnp.maximum(n, 1).astype(jnp.float32)
    large = max_exact + (jnp.log(nf / max_exact) / math.log(REL_MAX_DIST / max_exact)
                         * (REL_BUCKETS - max_exact)).astype(jnp.int32)
    large = jnp.minimum(large, REL_BUCKETS - 1)
    return jnp.where(n < max_exact, n, large)


def diff_attention(q, k, v, rel_bias, lam):
    b, s, h, _, d = q.shape
    scale = d ** -0.5
    outs = []
    for i in range(s // Q_BLOCK):
        q0 = i * Q_BLOCK
        kl = q0 + Q_BLOCK
        qb, kb, vb = q[:, q0:kl], k[:, :kl], v[:, :kl]
        dist = jnp.arange(q0, kl)[:, None] - jnp.arange(kl)[None, :]
        causal = dist >= 0
        bias = jnp.take(rel_bias, t5_bucket(jnp.maximum(dist, 0)), axis=0).astype(jnp.float32)
        bias = jnp.transpose(bias, (2, 0, 1))[None, :, None]
        logits = jnp.einsum('bqhmd,bkhmd->bhmqk', qb, kb).astype(jnp.float32) * scale + bias
        logits = jnp.where(causal, logits, -jnp.inf)
        a = jax.nn.softmax(logits, axis=-1)
        a = a[:, :, 0] - lam * a[:, :, 1]
        outs.append(jnp.einsum('bhqk,bkhv->bqhv', a.astype(vb.dtype), vb))
    return jnp.concatenate(outs, axis=1)


def routed_experts(hf, expert_idx, w, wg, wu, wd):
    n, d = hf.shape
    a = n * TOP_K
    g = EXPERT_BLOCK
    e_flat = expert_idx.reshape(a).astype(jnp.int32)
    order = jnp.argsort(e_flat)
    e_sorted = e_flat[order]
    tok_sorted = (order // TOP_K).astype(jnp.int32)
    w_sorted = w.reshape(a)[order]
    counts = jnp.bincount(e_flat, length=N_EXPERTS)
    padded = (counts + g - 1) // g * g
    start = jnp.cumsum(counts) - counts
    pend = jnp.cumsum(padded)
    pstart = pend - padded
    dest = pstart[e_sorted] + jnp.arange(a, dtype=jnp.int32) - start[e_sorted]
    n_rows = -(-(a + N_EXPERTS * (g - 1)) // g) * g
    n_blocks = n_rows // g
    row_tok = jnp.full((n_rows,), n, jnp.int32).at[dest].set(tok_sorted)
    row_w = jnp.zeros((n_rows,), jnp.float32).at[dest].set(w_sorted)
    block_e = jnp.minimum(jnp.searchsorted(pend, jnp.arange(n_blocks, dtype=jnp.int32) * g, side='right'),
                          N_EXPERTS - 1)
    h_pad = jnp.concatenate([hf, jnp.zeros((1, d), hf.dtype)], axis=0)

    def expert_block(args):
        rows, e = args
        return swiglu(h_pad[rows], wg[e], wu[e], wd[e])

    y = lax.map(expert_block, (row_tok.reshape(n_blocks, g), block_e))
    y = y.reshape(n_rows, d) * row_w[:, None].astype(y.dtype)
    return jnp.zeros((n + 1, d), y.dtype).at[row_tok].add(y)[:n]


def moe_ffn(h, router_w, router_bias, wg, wu, wd, sg, su, sd):
    b, s, d = h.shape
    n = b * s
    hf = h.reshape(n, d)
    scores = jax.nn.sigmoid((hf @ router_w).astype(jnp.float32))
    choice = scores + router_bias.astype(jnp.float32)
    grouped = choice.reshape(n, N_GROUPS, N_EXPERTS // N_GROUPS)
    group_score = jnp.sum(lax.top_k(grouped, 2)[0], axis=-1)
    _, top_groups = lax.top_k(group_score, TOPK_GROUPS)
    group_ok = jnp.sum(jax.nn.one_hot(top_groups, N_GROUPS, dtype=jnp.float32), axis=1) > 0
    masked = jnp.where(group_ok[:, :, None], grouped, -jnp.inf).reshape(n, N_EXPERTS)
    _, expert_idx = lax.top_k(masked, TOP_K)
    w = jnp.take_along_axis(scores, expert_idx, axis=-1)
    w = w / jnp.sum(w, axis=-1, keepdims=True) * ROUTED_SCALE
    routed = routed_experts(hf, expert_idx, w, wg, wu, wd)
    return (routed + swiglu(hf, sg, su, sd)).reshape(b, s, d)


def setup_inputs(seed: int = 0) -> dict:
    key = jax.random.key(seed)
    ks = jax.random.split(key, 32)

    def nrm(k, shape, scale):
        return jax.random.normal(k, shape, jnp.float32) * scale

    seg_scale = [1.0, 1.0, DN_BETA, 1.0, 1.0, 1.0, DN_BETA, 1.0]
    col_scale = jnp.asarray(np.concatenate([np.full((w_,), s_, np.float32)
                                            for w_, s_ in zip(SPLITS, seg_scale)]))
    return {
        "x": nrm(ks[0], (BATCH, SEQ, D_MODEL), 1.0),
        "p": nrm(ks[1], (DEPTH, BATCH, SEQ, PLE_DIM), 1.0),
        "ln_in_g": 1.0 + nrm(ks[2], (D_MODEL,), 0.02),
        "ln_in_b": nrm(ks[3], (D_MODEL,), 0.02),
        "rel_bias": nrm(ks[4], (REL_BUCKETS, DIFF_HEADS), 0.2),
        "w_in": nrm(ks[5], (DEPTH, D_MODEL, IN_COLS), D_MODEL ** -0.5) * col_scale,
        "ret_gn_g": 1.0 + nrm(ks[6], (DEPTH, RET_VW), 0.02),
        "lambda_q1": nrm(ks[7], (DEPTH, DIFF_HD), 0.1),
        "lambda_k1": nrm(ks[8], (DEPTH, DIFF_HD), 0.1),
        "lambda_q2": nrm(ks[9], (DEPTH, DIFF_HD), 0.1),
        "lambda_k2": nrm(ks[10], (DEPTH, DIFF_HD), 0.1),
        "diff_subln_g": 1.0 + nrm(ks[11], (DEPTH, 2 * DIFF_HD), 0.02),
        "w_ret_o": nrm(ks[12], (DEPTH, RET_VW, D_MODEL), RET_VW ** -0.5 * DN_BETA),
        "w_diff_o": nrm(ks[13], (DEPTH, DIFF_VW, D_MODEL), DIFF_VW ** -0.5 * DN_BETA),
        "w_o": nrm(ks[14], (DEPTH, D_MODEL, D_MODEL), D_MODEL ** -0.5 * DN_BETA),
        "ln1_g": 1.0 + nrm(ks[15], (DEPTH, D_MODEL), 0.02),
        "ln1_b": nrm(ks[16], (DEPTH, D_MODEL), 0.02),
        "router_w": nrm(ks[17], (DEPTH, D_MODEL, N_EXPERTS), D_MODEL ** -0.5),
        "router_bias": nrm(ks[18], (DEPTH, N_EXPERTS), 0.01),
        "exp_w_gate": nrm(ks[19], (DEPTH, N_EXPERTS, D_MODEL, D_EXPERT), D_MODEL ** -0.5),
        "exp_w_up": nrm(ks[20], (DEPTH, N_EXPERTS, D_MODEL, D_EXPERT), D_MODEL ** -0.5),
        "exp_w_down": nrm(ks[21], (DEPTH, N_EXPERTS, D_EXPERT, D_MODEL), D_EXPERT ** -0.5 * DN_BETA),
        "sh_w_gate": nrm(ks[22], (DEPTH, D_MODEL, D_SHARED), D_MODEL ** -0.5),
        "sh_w_up": nrm(ks[23], (DEPTH, D_MODEL, D_SHARED), D_MODEL ** -0.5),
        "sh_w_down": nrm(ks[24], (DEPTH, D_SHARED, D_MODEL), D_SHARED ** -0.5 * DN_BETA),
        "ple_w": nrm(ks[25], (DEPTH, PLE_DIM, D_MODEL), PLE_DIM ** -0.5 * DN_BETA),
        "ple_gate_w": nrm(ks[26], (DEPTH, D_MODEL, D_MODEL), D_MODEL ** -0.5),
        "ln2_g": 1.0 + nrm(ks[27], (DEPTH, D_MODEL), 0.02),
        "ln2_b": nrm(ks[28], (DEPTH, D_MODEL), 0.02),
    }


def reference(x, p, ln_in_g, ln_in_b, rel_bias, w_in, ret_gn_g, lambda_q1, lambda_k1, lambda_q2,
              lambda_k2, diff_subln_g, w_ret_o, w_diff_o, w_o, ln1_g, ln1_b, router_w, router_bias,
              exp_w_gate, exp_w_up, exp_w_down, sh_w_gate, sh_w_up, sh_w_down, ple_w, ple_gate_w,
              ln2_g, ln2_b):
    b, s, _ = x.shape
    pos = jnp.arange(s, dtype=jnp.float32)
    offsets = np.cumsum(SPLITS)[:-1].tolist()
    h = layer_norm(x, ln_in_g, ln_in_b)
    for i in range(DEPTH):
        proj = h @ w_in[i]
        q_r, k_r, v_r, g_r, q_d, k_d, v_d, gate_logits = jnp.split(proj, offsets, axis=-1)

        q_r = xpos_rotate(q_r.reshape(b, s, RET_HEADS, RET_QK).astype(jnp.float32), pos)
        k_r = xpos_rotate(k_r.reshape(b, s, RET_HEADS, RET_QK).astype(jnp.float32), pos)
        v_r = v_r.reshape(b, s, RET_HEADS, RET_V).astype(jnp.float32)
        o_r = retention_chunkwise(q_r, k_r, v_r)
        mu = jnp.mean(o_r, -1, keepdims=True)
        var = jnp.mean(jnp.square(o_r - mu), -1, keepdims=True)
        o_r = ((o_r - mu) * lax.rsqrt(var + EPS)).reshape(b, s, RET_VW) * ret_gn_g[i]
        y_r = (jax.nn.silu(g_r) * o_r.astype(x.dtype)) @ w_ret_o[i]

        lam_init = 0.8 - 0.6 * math.exp(-0.3 * i)
        lam = (jnp.exp(jnp.sum(lambda_q1[i].astype(jnp.float32) * lambda_k1[i].astype(jnp.float32)))
               - jnp.exp(jnp.sum(lambda_q2[i].astype(jnp.float32) * lambda_k2[i].astype(jnp.float32)))
               + lam_init)
        o_d = diff_attention(q_d.reshape(b, s, DIFF_HEADS, 2, DIFF_HD),
                             k_d.reshape(b, s, DIFF_HEADS, 2, DIFF_HD),
                             v_d.reshape(b, s, DIFF_HEADS, 2 * DIFF_HD), rel_bias, lam)
        o_d = o_d.astype(jnp.float32)
        o_d = o_d * lax.rsqrt(jnp.mean(jnp.square(o_d), -1, keepdims=True) + EPS) * diff_subln_g[i] * (1.0 - lam_init)
        y_d = o_d.reshape(b, s, DIFF_VW).astype(x.dtype) @ w_diff_o[i]

        gates = jax.nn.sigmoid(gate_logits.reshape(b, s, N_BRANCH, D_MODEL))
        mix = (gates[:, :, 0] * y_r + gates[:, :, 1] * y_d) @ w_o[i]
        h = layer_norm(DN_ALPHA * h + mix, ln1_g[i], ln1_b[i])

        ffn = moe_ffn(h, router_w[i], router_bias[i], exp_w_gate[i], exp_w_up[i], exp_w_down[i],
                      sh_w_gate[i], sh_w_up[i], sh_w_down[i])
        ple = (p[i] @ ple_w[i]) * jax.nn.sigmoid(h @ ple_gate_w[i])
        h = layer_norm(DN_ALPHA * h + ffn + ple, ln2_g[i], ln2_b[i])
    return h
```

```python
import functools
import math

import numpy as np
import jax
import jax.numpy as jnp
from jax import lax
from jax.experimental import pallas as pl
from jax.experimental.pallas import tpu as pltpu

F32 = jnp.float32
BF16 = jnp.bfloat16
I32 = jnp.int32

D_MODEL = 1024
PLE_DIM = 256
RET_HEADS = 8
RET_QK = 128
RET_V = 256
RET_CHUNK = 128
XPOS_BASE = 10000.0
DIFF_HEADS = 8
DIFF_HD = 64
REL_BUCKETS = 32
REL_MAX_DIST = 128
N_EXPERTS = 256
TOP_K = 8
N_GROUPS = 8
GROUP_SIZE = N_EXPERTS // N_GROUPS
TOPK_GROUPS = 4
D_EXPERT = 256
ROUTED_SCALE = 2.5
EXPERT_BLOCK = 128
DEPTH = 1
DN_ALPHA = (2.0 * DEPTH) ** 0.25
EPS = 1e-5
LAM_INIT = 0.8 - 0.6 * math.exp(-0.3 * 0)

SEG_QR, SEG_KR, SEG_VR, SEG_GR, SEG_QD, SEG_KD, SEG_VD, SEG_GATE = (
    0, 1024, 2048, 4096, 6144, 7168, 8192, 9216)
IN_COLS = 11264

MASK_VALUE = -1e30
VMEM_LIMIT = 56 * 1024 * 1024

PROJ_TM = 512
PROJ_TN = 1024
ATTN_T = 512
POST_TM = 256
DEST_T = 2048
COMB_T = 64


def _layer_norm(x, g, b):
    mu = jnp.mean(x, -1, keepdims=True)
    d = x - mu
    var = jnp.mean(d * d, -1, keepdims=True)
    return d * lax.rsqrt(var + EPS) * g + b


def _silu(x):
    return x * jax.nn.sigmoid(x)


def _dot(a, b):
    return jnp.dot(a, b, preferred_element_type=F32)


def _dot_nt(a, b):
    return lax.dot_general(a, b, (((1,), (1,)), ((), ())), preferred_element_type=F32)


def _proj_kernel(x_ref, g_ref, b_ref, cos_ref, sin_ref, w_ref, h_ref, o_ref, hb_ref):
    j = pl.program_id(1)

    @pl.when(j == 0)
    def _():
        h = _layer_norm(x_ref[...], g_ref[...], b_ref[...])
        h_ref[...] = h
        hb_ref[...] = h.astype(BF16)

    acc = _dot(hb_ref[...], w_ref[...])

    def rotate(scale):
        c = cos_ref[...]
        s = sin_ref[...]
        for hh in range(PROJ_TN // RET_QK):
            t = acc[:, hh * RET_QK:(hh + 1) * RET_QK]
            v = t * c + pltpu.roll(t, RET_QK // 2, 1) * s
            if scale != 1.0:
                v = v * scale
            o_ref[:, hh * RET_QK:(hh + 1) * RET_QK] = v.astype(BF16)

    @pl.when(j == SEG_QR // PROJ_TN)
    def _():
        rotate(1.0)

    @pl.when(j == SEG_KR // PROJ_TN)
    def _():
        rotate(RET_QK ** -0.5)

    is_gr = (j >= SEG_GR // PROJ_TN) & (j < SEG_QD // PROJ_TN)
    is_qd = j == SEG_QD // PROJ_TN
    is_gate = j >= SEG_GATE // PROJ_TN
    is_plain = ((j >= SEG_VR // PROJ_TN) & (j < SEG_GR // PROJ_TN)) | (
        (j >= SEG_KD // PROJ_TN) & (j < SEG_GATE // PROJ_TN))

    @pl.when(is_plain)
    def _():
        o_ref[...] = acc.astype(BF16)

    @pl.when(is_gr)
    def _():
        o_ref[...] = _silu(acc).astype(BF16)

    @pl.when(is_qd)
    def _():
        o_ref[...] = (acc * (DIFF_HD ** -0.5)).astype(BF16)

    @pl.when(is_gate)
    def _():
        o_ref[...] = jax.nn.sigmoid(acc).astype(BF16)


def _proj_call(x2, ln_g, ln_b, cosf, sinf, w_bf, seq):
    n = x2.shape[0]
    tm, tn = PROJ_TM, PROJ_TN
    s_tiles = seq // tm
    return pl.pallas_call(
        _proj_kernel,
        name="proj",
        grid=(n // tm, IN_COLS // tn),
        in_specs=[
            pl.BlockSpec((tm, D_MODEL), lambda i, j: (i, 0)),
            pl.BlockSpec((1, D_MODEL), lambda i, j: (0, 0)),
            pl.BlockSpec((1, D_MODEL), lambda i, j: (0, 0)),
            pl.BlockSpec((tm, RET_QK), lambda i, j: (i % s_tiles, 0)),
            pl.BlockSpec((tm, RET_QK), lambda i, j: (i % s_tiles, 0)),
            pl.BlockSpec((D_MODEL, tn), lambda i, j: (0, j)),
        ],
        out_specs=[
            pl.BlockSpec((tm, D_MODEL), lambda i, j: (i, 0)),
            pl.BlockSpec((tm, tn), lambda i, j: (i, j)),
        ],
        out_shape=[
            jax.ShapeDtypeStruct((n, D_MODEL), F32),
            jax.ShapeDtypeStruct((n, IN_COLS), BF16),
        ],
        scratch_shapes=[pltpu.VMEM((tm, D_MODEL), BF16)],
        compiler_params=pltpu.CompilerParams(
            dimension_semantics=("arbitrary", "arbitrary"), vmem_limit_bytes=VMEM_LIMIT),
    )(x2, ln_g, ln_b, cosf, sinf, w_bf)


def _ret_kernel(q_ref, k_ref, v_ref, g_ref, inner_ref, qdec_ref, kdec_ref, cdec_ref, gn_ref,
                o_ref, state_ref):
    c = pl.program_id(2)

    @pl.when(c == 0)
    def _():
        state_ref[...] = jnp.zeros_like(state_ref)

    q = q_ref[...]
    k = k_ref[...]
    v = v_ref[...]
    st = state_ref[...]
    scores = _dot_nt(q, k) * inner_ref[0]
    o = _dot(scores.astype(BF16), v) + qdec_ref[0] * _dot(q, st.astype(BF16))
    kd = (k.astype(F32) * kdec_ref[0]).astype(BF16)
    kv = lax.dot_general(kd, v, (((0,), (0,)), ((), ())), preferred_element_type=F32)
    state_ref[...] = cdec_ref[0] * st + kv

    mu = jnp.mean(o, -1, keepdims=True)
    d = o - mu
    var = jnp.mean(d * d, -1, keepdims=True)
    on = d * lax.rsqrt(var + EPS) * gn_ref[0]
    o_ref[...] = (g_ref[...].astype(F32) * on).astype(BF16)


def _ret_call(proj, inner, qdec, kdec, cdec, gn, batch, seq):
    n = proj.shape[0]
    c = RET_CHUNK
    nc = seq // c
    qb = SEG_QR // RET_QK
    kb = SEG_KR // RET_QK
    vb = SEG_VR // RET_V
    gb = SEG_GR // RET_V
    return pl.pallas_call(
        _ret_kernel,
        name="retention",
        grid=(batch, RET_HEADS, nc),
        in_specs=[
            pl.BlockSpec((c, RET_QK), lambda b, h, i: (b * nc + i, qb + h)),
            pl.BlockSpec((c, RET_QK), lambda b, h, i: (b * nc + i, kb + h)),
            pl.BlockSpec((c, RET_V), lambda b, h, i: (b * nc + i, vb + h)),
            pl.BlockSpec((c, RET_V), lambda b, h, i: (b * nc + i, gb + h)),
            pl.BlockSpec((1, c, c), lambda b, h, i: (h, 0, 0)),
            pl.BlockSpec((1, c, 1), lambda b, h, i: (h, 0, 0)),
            pl.BlockSpec((1, c, 1), lambda b, h, i: (h, 0, 0)),
            pl.BlockSpec((1, 1, RET_V), lambda b, h, i: (h, 0, 0)),
            pl.BlockSpec((1, 1, RET_V), lambda b, h, i: (h, 0, 0)),
        ],
        out_specs=pl.BlockSpec((c, RET_V), lambda b, h, i: (b * nc + i, h)),
        out_shape=jax.ShapeDtypeStruct((n, RET_HEADS * RET_V), BF16),
        scratch_shapes=[pltpu.VMEM((RET_QK, RET_V), F32)],
        compiler_params=pltpu.CompilerParams(
            dimension_semantics=("arbitrary", "arbitrary", "arbitrary"),
            vmem_limit_bytes=VMEM_LIMIT),
    )(proj, proj, proj, proj, inner, qdec, kdec, cdec, gn)


def _attn_kernel(qi_ref, ki_ref, q_ref, k_ref, v_ref, tab_ref, lq1_ref, lk1_ref, lq2_ref,
                 lk2_ref, sg_ref, o_ref, qs_ref, m_ref, l_ref, acc_ref):
    t = ATTN_T
    p = pl.program_id(2)
    qi = qi_ref[p]
    ki = ki_ref[p]

    @pl.when(ki == 0)
    def _():
        q = q_ref[...]
        lane = lax.broadcasted_iota(I32, q.shape, 1)
        zero = jnp.zeros_like(q)
        qs_ref[0:t, :] = jnp.where(lane < DIFF_HD, q, zero)
        qs_ref[t:2 * t, :] = jnp.where(lane >= DIFF_HD, q, zero)
        m_ref[...] = jnp.full_like(m_ref, MASK_VALUE)
        l_ref[...] = jnp.zeros_like(l_ref)
        acc_ref[...] = jnp.zeros_like(acc_ref)

    def step(bias):
        kk = k_ref[...]
        vv = v_ref[...]
        for half in range(2):
            rows = slice(half * t, (half + 1) * t)
            s = _dot_nt(qs_ref[rows, :], kk)
            if bias is not None:
                s = s + bias()
            m_prev = m_ref[rows, :]
            m_new = jnp.maximum(m_prev, jnp.max(s, -1, keepdims=True))
            alpha = jnp.exp(m_prev - m_new)
            pr = jnp.exp(s - m_new)
            l_ref[rows, :] = alpha * l_ref[rows, :] + jnp.sum(pr, -1, keepdims=True)
            acc_ref[rows, :] = alpha * acc_ref[rows, :] + _dot(pr.astype(BF16), vv)
            m_ref[rows, :] = m_new

    @pl.when(ki < qi - 1)
    def _():
        step(None)

    @pl.when(ki == qi - 1)
    def _():
        step(lambda: tab_ref[0, 1])

    @pl.when(ki == qi)
    def _():
        step(lambda: tab_ref[0, 0])
        lam = (jnp.exp(jnp.sum(lq1_ref[...] * lk1_ref[...], -1, keepdims=True))
               - jnp.exp(jnp.sum(lq2_ref[...] * lk2_ref[...], -1, keepdims=True)) + LAM_INIT)
        o1 = acc_ref[0:t, :] / l_ref[0:t, :]
        o2 = acc_ref[t:2 * t, :] / l_ref[t:2 * t, :]
        o = o1 - lam * o2
        o = o * lax.rsqrt(jnp.mean(o * o, -1, keepdims=True) + EPS) * sg_ref[...] * (1.0 - LAM_INIT)
        o_ref[...] = o.astype(BF16)


def _attn_call(proj, tab, lq1, lk1, lq2, lk2, sg, batch, seq):
    n = proj.shape[0]
    t = ATTN_T
    nq = seq // t
    pairs = [(a, b) for a in range(nq) for b in range(a + 1)]
    qi_tab = jnp.asarray(np.array([a for a, _ in pairs], np.int32))
    ki_tab = jnp.asarray(np.array([b for _, b in pairs], np.int32))
    hd2 = 2 * DIFF_HD
    qb, kb, vb = SEG_QD // hd2, SEG_KD // hd2, SEG_VD // hd2
    vec = pl.BlockSpec((1, DIFF_HD), lambda b, h, p, qi, ki: (0, 0))
    grid_spec = pltpu.PrefetchScalarGridSpec(
        num_scalar_prefetch=2,
        grid=(batch, DIFF_HEADS, len(pairs)),
        in_specs=[
            pl.BlockSpec((t, hd2), lambda b, h, p, qi, ki: (b * nq + qi[p], qb + h)),
            pl.BlockSpec((t, hd2), lambda b, h, p, qi, ki: (b * nq + ki[p], kb + h)),
            pl.BlockSpec((t, hd2), lambda b, h, p, qi, ki: (b * nq + ki[p], vb + h)),
            pl.BlockSpec((1, 2, t, t), lambda b, h, p, qi, ki: (h, 0, 0, 0)),
            vec, vec, vec, vec,
            pl.BlockSpec((1, hd2), lambda b, h, p, qi, ki: (0, 0)),
        ],
        out_specs=pl.BlockSpec((t, hd2), lambda b, h, p, qi, ki: (b * nq + qi[p], h)),
        scratch_shapes=[
            pltpu.VMEM((2 * t, hd2), BF16),
            pltpu.VMEM((2 * t, 1), F32),
            pltpu.VMEM((2 * t, 1), F32),
            pltpu.VMEM((2 * t, hd2), F32),
        ],
    )
    return pl.pallas_call(
        _attn_kernel,
        name="diffattn",
        grid_spec=grid_spec,
        out_shape=jax.ShapeDtypeStruct((n, DIFF_HEADS * hd2), BF16),
        compiler_params=pltpu.CompilerParams(
            dimension_semantics=("arbitrary", "arbitrary", "arbitrary"),
            vmem_limit_bytes=VMEM_LIMIT),
    )(qi_tab, ki_tab, proj, proj, proj, tab, lq1, lk1, lq2, lk2, sg)


def _route(scores, bias, carry):
    e, tm = scores.shape
    neg = -jnp.inf
    choice = scores + bias
    row_g = lax.broadcasted_iota(I32, (GROUP_SIZE, tm), 0)
    group_scores = []
    for g in range(N_GROUPS):
        blk = choice[g * GROUP_SIZE:(g + 1) * GROUP_SIZE]
        m1 = jnp.max(blk, 0, keepdims=True)
        i1 = jnp.min(jnp.where(blk == m1, row_g, GROUP_SIZE), 0, keepdims=True)
        m2 = jnp.max(jnp.where(row_g == i1, neg, blk), 0, keepdims=True)
        group_scores.append(m1 + m2)
    gsc = jnp.concatenate(group_scores, 0)
    row8 = lax.broadcasted_iota(I32, (N_GROUPS, tm), 0)
    okf = jnp.zeros((N_GROUPS, tm), F32)
    for _ in range(TOPK_GROUPS):
        m = jnp.max(gsc, 0, keepdims=True)
        i = jnp.min(jnp.where(gsc == m, row8, N_GROUPS), 0, keepdims=True)
        hit = row8 == i
        okf = jnp.where(hit, 1.0, okf)
        gsc = jnp.where(hit, neg, gsc)
    masked = jnp.concatenate(
        [jnp.where(jnp.broadcast_to(okf[g:g + 1], (GROUP_SIZE, tm)) > 0.0,
                   choice[g * GROUP_SIZE:(g + 1) * GROUP_SIZE], neg)
         for g in range(N_GROUPS)], 0)
    row = lax.broadcasted_iota(I32, (e, tm), 0)
    idxs, ws = [], []
    multi = jnp.zeros((e, tm), F32)
    for _ in range(TOP_K):
        m = jnp.max(masked, 0, keepdims=True)
        i = jnp.min(jnp.where(masked == m, row, e), 0, keepdims=True)
        hit = row == i
        ws.append(jnp.sum(jnp.where(hit, scores, 0.0), 0, keepdims=True))
        idxs.append(i)
        multi = jnp.where(hit, 1.0, multi)
        masked = jnp.where(hit, neg, masked)
    a = lax.broadcasted_iota(I32, (tm, tm), 0)
    b = lax.broadcasted_iota(I32, (tm, tm), 1)
    upper = jnp.where(a < b, 1.0, 0.0).astype(BF16)
    before = _dot(multi.astype(BF16), upper) + carry
    ranks = [jnp.sum(jnp.where(row == i, before, 0.0), 0, keepdims=True).astype(I32) for i in idxs]
    new_carry = carry + jnp.sum(multi, 1, keepdims=True)
    wsum = ws[0]
    for w in ws[1:]:
        wsum = wsum + w
    ws = [w / wsum * ROUTED_SCALE for w in ws]
    return idxs, ws, ranks, new_carry


def _post_kernel(yr_ref, od_ref, g0_ref, g1_ref, h_ref, p_ref, wr_ref, wd_ref, wo_ref, l1g_ref,
                 l1b_ref, sg_ref, su_ref, sd_ref, pw_ref, pg_ref, rwt_ref, rb_ref,
                 h1_ref, base_ref, idx_ref, w_ref, rank_ref, cnt_ref, carry_ref):
    i = pl.program_id(0)

    @pl.when(i == 0)
    def _():
        carry_ref[...] = jnp.zeros_like(carry_ref)

    y_r = _dot(yr_ref[...], wr_ref[...])
    y_d = _dot(od_ref[...], wd_ref[...])
    merged = g0_ref[...].astype(F32) * y_r + g1_ref[...].astype(F32) * y_d
    mix = _dot(merged.astype(BF16), wo_ref[...])
    h1 = _layer_norm(DN_ALPHA * h_ref[...] + mix, l1g_ref[...], l1b_ref[...])
    h1_ref[...] = h1
    hb = h1.astype(BF16)

    shared = _dot((_silu(_dot(hb, sg_ref[...])) * _dot(hb, su_ref[...])).astype(BF16), sd_ref[...])
    ple = _dot(p_ref[...].astype(BF16), pw_ref[...]) * jax.nn.sigmoid(_dot(hb, pg_ref[...]))
    base_ref[...] = DN_ALPHA * h1 + shared + ple

    h_lo = (h1 - hb.astype(F32)).astype(BF16)
    rw = rwt_ref[...]
    rw_hi = rw.astype(BF16)
    rw_lo = (rw - rw_hi.astype(F32)).astype(BF16)
    logits = _dot_nt(rw_hi, hb) + (_dot_nt(rw_hi, h_lo) + _dot_nt(rw_lo, hb))
    scores = jax.nn.sigmoid(logits)
    idxs, ws, ranks, new_carry = _route(scores, rb_ref[...], carry_ref[...])
    idx_ref[...] = jnp.concatenate(idxs, 0)
    w_ref[...] = jnp.concatenate(ws, 0)
    rank_ref[...] = jnp.concatenate(ranks, 0)
    carry_ref[...] = new_carry
    cnt_ref[...] = new_carry.astype(I32)


def _post_call(yr, od, proj, h, p2, wr, wd, wo, l1g, l1b, sg, su, sd, pw, pg, rwt, rb):
    n = h.shape[0]
    tm = POST_TM
    g0b = SEG_GATE // D_MODEL

    def full(a):
        return pl.BlockSpec(a.shape, lambda i: (0,) * a.ndim)

    row = lambda w: pl.BlockSpec((tm, w), lambda i: (i, 0))
    tok = lambda: pl.BlockSpec((TOP_K, tm), lambda i: (0, i))
    return pl.pallas_call(
        _post_kernel,
        name="post",
        grid=(n // tm,),
        in_specs=[
            row(RET_HEADS * RET_V), row(D_MODEL),
            pl.BlockSpec((tm, D_MODEL), lambda i: (i, g0b)),
            pl.BlockSpec((tm, D_MODEL), lambda i: (i, g0b + 1)),
            row(D_MODEL), row(PLE_DIM),
            full(wr), full(wd), full(wo), full(l1g), full(l1b), full(sg), full(su), full(sd),
            full(pw), full(pg), full(rwt), full(rb),
        ],
        out_specs=[row(D_MODEL), row(D_MODEL), tok(), tok(), tok(),
                   pl.BlockSpec((N_EXPERTS, 1), lambda i: (0, 0))],
        out_shape=[
            jax.ShapeDtypeStruct((n, D_MODEL), F32),
            jax.ShapeDtypeStruct((n, D_MODEL), F32),
            jax.ShapeDtypeStruct((TOP_K, n), I32),
            jax.ShapeDtypeStruct((TOP_K, n), F32),
            jax.ShapeDtypeStruct((TOP_K, n), I32),
            jax.ShapeDtypeStruct((N_EXPERTS, 1), I32),
        ],
        scratch_shapes=[pltpu.VMEM((N_EXPERTS, 1), F32)],
        compiler_params=pltpu.CompilerParams(
            dimension_semantics=("arbitrary",), vmem_limit_bytes=VMEM_LIMIT),
    )(yr, od, proj, proj, h, p2, wr, wd, wo, l1g, l1b, sg, su, sd, pw, pg, rwt, rb)


def _dest_kernel(idx_ref, rank_ref, ps_ref, o_ref):
    idx = idx_ref[...]
    ps = ps_ref[...]
    td = idx.shape[1]
    row = lax.broadcasted_iota(I32, (N_EXPERTS, td), 0)
    outs = []
    for k in range(TOP_K):
        sel = jnp.sum(jnp.where(row == idx[k:k + 1], ps, 0.0), 0, keepdims=True)
        outs.append(sel.astype(I32))
    o_ref[...] = jnp.concatenate(outs, 0) + rank_ref[...]


def _dest_call(idx, rank, pstart_f):
    n = idx.shape[1]
    td = min(DEST_T, n)
    spec = pl.BlockSpec((TOP_K, td), lambda i: (0, i))
    return pl.pallas_call(
        _dest_kernel,
        name="dest",
        grid=(n // td,),
        in_specs=[spec, spec, pl.BlockSpec((N_EXPERTS, 1), lambda i: (0, 0))],
        out_specs=spec,
        out_shape=jax.ShapeDtypeStruct((TOP_K, n), I32),
        compiler_params=pltpu.CompilerParams(dimension_semantics=("arbitrary",)),
    )(idx, rank, pstart_f)


def _expert_kernel(be_ref, rt_ref, rtn_ref, h_hbm, wg_ref, wu_ref, wd_ref, y_ref,
                   buf_ref, wgb_ref, wub_ref, wdb_ref, sem):
    i = pl.program_id(0)
    nb = pl.num_programs(0)
    g = EXPERT_BLOCK
    slot = i % 2

    def issue(tok_ref, s):
        def body(r, c):
            pltpu.make_async_copy(h_hbm.at[pl.ds(tok_ref[0, 0, r], 1)],
                                  buf_ref.at[s, pl.ds(r, 1)], sem.at[s]).start()
            return c
        lax.fori_loop(0, g, body, 0)

    @pl.when(i == 0)
    def _():
        issue(rt_ref, 0)

    @pl.when(i + 1 < nb)
    def _():
        issue(rtn_ref, 1 - slot)

    changed = jnp.logical_or(i == 0, be_ref[i] != be_ref[jnp.maximum(i - 1, 0)])

    @pl.when(changed)
    def _():
        wgb_ref[...] = wg_ref[0].astype(BF16)
        wub_ref[...] = wu_ref[0].astype(BF16)
        wdb_ref[...] = wd_ref[0].astype(BF16)

    pltpu.make_async_copy(h_hbm.at[pl.ds(0, g)], buf_ref.at[slot], sem.at[slot]).wait()
    x = buf_ref[slot].astype(BF16)
    act = _silu(_dot(x, wgb_ref[...])) * _dot(x, wub_ref[...])
    y_ref[...] = _dot(act.astype(BF16), wdb_ref[...])


def _expert_call(block_e, row_tok3, h1, wg, wu, wd):
    nb = block_e.shape[0]
    g = EXPERT_BLOCK
    grid_spec = pltpu.PrefetchScalarGridSpec(
        num_scalar_prefetch=1,
        grid=(nb,),
        in_specs=[
            pl.BlockSpec((1, 1, g), lambda i, be: (i, 0, 0), memory_space=pltpu.SMEM),
            pl.BlockSpec((1, 1, g), lambda i, be: (jnp.minimum(i + 1, nb - 1), 0, 0),
                         memory_space=pltpu.SMEM),
            pl.BlockSpec(memory_space=pl.ANY),
            pl.BlockSpec((1, D_MODEL, D_EXPERT), lambda i, be: (be[i], 0, 0)),
            pl.BlockSpec((1, D_MODEL, D_EXPERT), lambda i, be: (be[i], 0, 0)),
            pl.BlockSpec((1, D_EXPERT, D_MODEL), lambda i, be: (be[i], 0, 0)),
        ],
        out_specs=pl.BlockSpec((g, D_MODEL), lambda i, be: (i, 0)),
        scratch_shapes=[
            pltpu.VMEM((2, g, D_MODEL), F32),
            pltpu.VMEM((D_MODEL, D_EXPERT), BF16),
            pltpu.VMEM((D_MODEL, D_EXPERT), BF16),
            pltpu.VMEM((D_EXPERT, D_MODEL), BF16),
            pltpu.SemaphoreType.DMA((2,)),
        ],
    )
    return pl.pallas_call(
        _expert_kernel,
        name="experts",
        grid_spec=grid_spec,
        out_shape=jax.ShapeDtypeStruct((nb * g, D_MODEL), F32),
        compiler_params=pltpu.CompilerParams(
            dimension_semantics=("arbitrary",), vmem_limit_bytes=VMEM_LIMIT),
    )(block_e, row_tok3, row_tok3, h1, wg, wu, wd)


def _combine_kernel(d_ref, dn_ref, y_hbm, w_ref, base_ref, g_ref, b_ref, o_ref, buf_ref, sem):
    i = pl.program_id(0)
    nt = pl.num_programs(0)
    tt = COMB_T
    slot = i % 2

    def issue(dest_ref, s):
        def body(t, c):
            for k in range(TOP_K):
                pltpu.make_async_copy(y_hbm.at[pl.ds(dest_ref[0, k, t], 1)],
                                      buf_ref.at[s, k, pl.ds(t, 1)], sem.at[s]).start()
            return c
        lax.fori_loop(0, tt, body, 0)

    @pl.when(i == 0)
    def _():
        issue(d_ref, 0)

    @pl.when(i + 1 < nt)
    def _():
        issue(dn_ref, 1 - slot)

    for k in range(TOP_K):
        pltpu.make_async_copy(y_hbm.at[pl.ds(0, tt)], buf_ref.at[slot, k], sem.at[slot]).wait()
    w = w_ref[...]
    routed = w[:, 0:1] * buf_ref[slot, 0]
    for k in range(1, TOP_K):
        routed = routed + w[:, k:k + 1] * buf_ref[slot, k]
    o_ref[...] = _layer_norm(base_ref[...] + routed, g_ref[...], b_ref[...])


def _combine_call(dest3, y, w_t, base, ln_g, ln_b):
    n = base.shape[0]
    tt = COMB_T
    nt = n // tt
    return pl.pallas_call(
        _combine_kernel,
        name="combine",
        grid=(nt,),
        in_specs=[
            pl.BlockSpec((1, TOP_K, tt), lambda i: (i, 0, 0), memory_space=pltpu.SMEM),
            pl.BlockSpec((1, TOP_K, tt), lambda i: (jnp.minimum(i + 1, nt - 1), 0, 0),
                         memory_space=pltpu.SMEM),
            pl.BlockSpec(memory_space=pl.ANY),
            pl.BlockSpec((tt, TOP_K), lambda i: (i, 0)),
            pl.BlockSpec((tt, D_MODEL), lambda i: (i, 0)),
            pl.BlockSpec((1, D_MODEL), lambda i: (0, 0)),
            pl.BlockSpec((1, D_MODEL), lambda i: (0, 0)),
        ],
        out_specs=pl.BlockSpec((tt, D_MODEL), lambda i: (i, 0)),
        out_shape=jax.ShapeDtypeStruct((n, D_MODEL), F32),
        scratch_shapes=[
            pltpu.VMEM((2, TOP_K, tt, D_MODEL), F32),
            pltpu.SemaphoreType.DMA((2,)),
        ],
        compiler_params=pltpu.CompilerParams(
            dimension_semantics=("arbitrary",), vmem_limit_bytes=VMEM_LIMIT),
    )(dest3, dest3, y, w_t, base, ln_g, ln_b)


def _xpos_tables(seq):
    half = RET_QK // 2
    pos = jnp.arange(seq, dtype=F32)
    theta = 1.0 / (XPOS_BASE ** jnp.linspace(0.0, 1.0, half, dtype=F32))
    ang = pos[:, None] * theta[None, :]
    cos = jnp.cos(ang)
    sin = jnp.sin(ang)
    return jnp.concatenate([cos, cos], -1), jnp.concatenate([-sin, sin], -1)


def _decay_tables():
    c = RET_CHUNK
    log_gamma = jnp.log(1.0 - 2.0 ** (-5.0 - jnp.arange(RET_HEADS, dtype=F32)))
    idx = jnp.arange(c, dtype=F32)
    rel = idx[:, None] - idx[None, :]
    inner = jnp.where(rel >= 0, jnp.exp(jnp.maximum(rel, 0.0)[None] * log_gamma[:, None, None]), 0.0)
    q_dec = jnp.exp((idx + 1.0)[None, :] * log_gamma[:, None])
    k_dec = jnp.exp((c - 1.0 - idx)[None, :] * log_gamma[:, None])
    c_dec = jnp.exp(c * log_gamma)
    cdec = jnp.broadcast_to(c_dec[:, None, None], (RET_HEADS, 1, RET_V))
    return inner, q_dec[:, :, None], k_dec[:, :, None], cdec


def _t5_bucket(n):
    max_exact = REL_BUCKETS // 2
    nf = jnp.maximum(n, 1).astype(F32)
    large = max_exact + (jnp.log(nf / max_exact) / math.log(REL_MAX_DIST / max_exact)
                         * (REL_BUCKETS - max_exact)).astype(I32)
    large = jnp.minimum(large, REL_BUCKETS - 1)
    return jnp.where(n < max_exact, n, large)


def _bias_tables(rel_bias, t):
    i = jnp.arange(t)[:, None]
    j = jnp.arange(t)[None, :]
    far = rel_bias[REL_BUCKETS - 1].astype(F32)
    d0 = i - j
    b0 = jnp.take(rel_bias, _t5_bucket(jnp.maximum(d0, 0)), axis=0).astype(F32) - far
    b0 = jnp.where((d0 >= 0)[:, :, None], b0, MASK_VALUE)
    b1 = jnp.take(rel_bias, _t5_bucket(d0 + t), axis=0).astype(F32) - far
    return jnp.transpose(jnp.stack([b0, b1], 0), (3, 0, 1, 2))


def kernel(x, p, ln_in_g, ln_in_b, rel_bias, w_in, ret_gn_g, lambda_q1, lambda_k1, lambda_q2,
           lambda_k2, diff_subln_g, w_ret_o, w_diff_o, w_o, ln1_g, ln1_b, router_w, router_bias,
           exp_w_gate, exp_w_up, exp_w_down, sh_w_gate, sh_w_up, sh_w_down, ple_w, ple_gate_w,
           ln2_g, ln2_b):
    batch, seq, d = x.shape
    n = batch * seq
    assert d == D_MODEL and seq % ATTN_T == 0 and seq % PROJ_TM == 0
    assert n % min(DEST_T, n) == 0 and n % COMB_T == 0 and n % POST_TM == 0
    x2 = x.reshape(n, d)
    row = lambda a: a.reshape(1, -1).astype(F32)

    cosf, sinf = _xpos_tables(seq)
    inner, qdec, kdec, cdec = _decay_tables()
    h, proj = _proj_call(x2, row(ln_in_g), row(ln_in_b), cosf, sinf, w_in[0].astype(BF16), seq)

    yr = _ret_call(proj, inner, qdec, kdec, cdec,
                   ret_gn_g[0].reshape(RET_HEADS, 1, RET_V).astype(F32), batch, seq)

    od = _attn_call(proj, _bias_tables(rel_bias, ATTN_T), row(lambda_q1[0]), row(lambda_k1[0]),
                    row(lambda_q2[0]), row(lambda_k2[0]), row(diff_subln_g[0]), batch, seq)

    h1, base, idx, w, rank, counts = _post_call(
        yr, od, proj, h, p[0].reshape(n, PLE_DIM),
        w_ret_o[0].astype(BF16), w_diff_o[0].astype(BF16), w_o[0].astype(BF16),
        row(ln1_g[0]), row(ln1_b[0]),
        sh_w_gate[0].astype(BF16), sh_w_up[0].astype(BF16), sh_w_down[0].astype(BF16),
        ple_w[0].astype(BF16), ple_gate_w[0].astype(BF16),
        router_w[0].T.astype(F32), router_bias[0].reshape(N_EXPERTS, 1).astype(F32))

    g = EXPERT_BLOCK
    counts = counts.reshape(N_EXPERTS)
    padded = (counts + g - 1) // g * g
    pend = jnp.cumsum(padded)
    pstart = pend - padded
    n_rows = -(-(n * TOP_K + N_EXPERTS * (g - 1)) // g) * g
    n_blocks = n_rows // g
    block_e = jnp.minimum(
        jnp.searchsorted(pend, jnp.arange(n_blocks, dtype=I32) * g, side='right'),
        N_EXPERTS - 1).astype(I32)

    dest = _dest_call(idx, rank, pstart.astype(F32).reshape(N_EXPERTS, 1))
    tok = jnp.broadcast_to(jnp.arange(n, dtype=I32)[None, :], (TOP_K, n))
    row_tok = jnp.zeros((n_rows,), I32).at[dest.reshape(-1)].set(
        tok.reshape(-1), unique_indices=True, mode='drop')

    y = _expert_call(block_e, row_tok.reshape(n_blocks, 1, g), h1,
                     exp_w_gate[0], exp_w_up[0], exp_w_down[0])

    dest3 = dest.reshape(TOP_K, n // COMB_T, COMB_T).transpose(1, 0, 2)
    out = _combine_call(dest3, y, w.T, base, row(ln2_g[0]), row(ln2_b[0]))
    return out.reshape(batch, seq, d)
```

```python
import functools
import math

import numpy as np
import jax
import jax.numpy as jnp
from jax import lax
from jax.experimental import pallas as pl
from jax.experimental.pallas import tpu as pltpu

F32 = jnp.float32
BF16 = jnp.bfloat16
I32 = jnp.int32

D_MODEL = 1024
PLE_DIM = 256
RET_HEADS = 8
RET_QK = 128
RET_V = 256
RET_CHUNK = 128
XPOS_BASE = 10000.0
DIFF_HEADS = 8
DIFF_HD = 64
REL_BUCKETS = 32
REL_MAX_DIST = 128
N_EXPERTS = 256
TOP_K = 8
N_GROUPS = 8
GROUP_SIZE = N_EXPERTS // N_GROUPS
TOPK_GROUPS = 4
D_EXPERT = 256
ROUTED_SCALE = 2.5
EXPERT_BLOCK = 128
DEPTH = 1
DN_ALPHA = (2.0 * DEPTH) ** 0.25
EPS = 1e-5
LAM_INIT = 0.8 - 0.6 * math.exp(-0.3 * 0)

SEG_QR, SEG_KR, SEG_VR, SEG_GR, SEG_QD, SEG_KD, SEG_VD, SEG_GATE = (
    0, 1024, 2048, 4096, 6144, 7168, 8192, 9216)
IN_COLS = 11264

MASK_VALUE = -1e30
LOG2E = math.log2(math.e)
VMEM_LIMIT = 56 * 1024 * 1024

PROJ_TM = 512
PROJ_TN = 1024
ATTN_T = 512
POST_TM = 256
DEST_T = 2048
COMB_T = 64


def _layer_norm(x, g, b):
    mu = jnp.mean(x, -1, keepdims=True)
    d = x - mu
    var = jnp.mean(d * d, -1, keepdims=True)
    return d * lax.rsqrt(var + EPS) * g + b


def _silu(x):
    return x * jax.nn.sigmoid(x)


def _dot(a, b):
    return jnp.dot(a, b, preferred_element_type=F32)


def _tree(op, items):
    items = list(items)
    while len(items) > 1:
        items = [op(items[i], items[i + 1]) if i + 1 < len(items) else items[i]
                 for i in range(0, len(items), 2)]
    return items[0]


def _dot_nt(a, b):
    return lax.dot_general(a, b, (((1,), (1,)), ((), ())), preferred_element_type=F32)


def _proj_kernel(x_ref, g_ref, b_ref, cos_ref, sin_ref, w_ref, h_ref, o_ref, hb_ref):
    j = pl.program_id(1)

    @pl.when(j == 0)
    def _():
        h = _layer_norm(x_ref[...], g_ref[...], b_ref[...])
        h_ref[...] = h
        hb_ref[...] = h.astype(BF16)

    acc = _dot(hb_ref[...], w_ref[...])

    def rotate(scale):
        c = cos_ref[...]
        s = sin_ref[...]
        for hh in range(PROJ_TN // RET_QK):
            t = acc[:, hh * RET_QK:(hh + 1) * RET_QK]
            v = t * c + pltpu.roll(t, RET_QK // 2, 1) * s
            if scale != 1.0:
                v = v * scale
            o_ref[:, hh * RET_QK:(hh + 1) * RET_QK] = v.astype(BF16)

    @pl.when(j == SEG_QR // PROJ_TN)
    def _():
        rotate(1.0)

    @pl.when(j == SEG_KR // PROJ_TN)
    def _():
        rotate(RET_QK ** -0.5)

    is_gr = (j >= SEG_GR // PROJ_TN) & (j < SEG_QD // PROJ_TN)
    is_qd = j == SEG_QD // PROJ_TN
    is_gate = j >= SEG_GATE // PROJ_TN
    is_plain = ((j >= SEG_VR // PROJ_TN) & (j < SEG_GR // PROJ_TN)) | (
        (j >= SEG_KD // PROJ_TN) & (j < SEG_GATE // PROJ_TN))

    @pl.when(is_plain)
    def _():
        o_ref[...] = acc.astype(BF16)

    @pl.when(is_gr)
    def _():
        o_ref[...] = _silu(acc).astype(BF16)

    @pl.when(is_qd)
    def _():
        o_ref[...] = (acc * (DIFF_HD ** -0.5 * LOG2E)).astype(BF16)

    @pl.when(is_gate)
    def _():
        o_ref[...] = jax.nn.sigmoid(acc).astype(BF16)


def _proj_call(x2, ln_g, ln_b, cosf, sinf, w_bf, seq):
    n = x2.shape[0]
    tm, tn = PROJ_TM, PROJ_TN
    s_tiles = seq // tm
    return pl.pallas_call(
        _proj_kernel,
        name="proj",
        grid=(n // tm, IN_COLS // tn),
        in_specs=[
            pl.BlockSpec((tm, D_MODEL), lambda i, j: (i, 0)),
            pl.BlockSpec((1, D_MODEL), lambda i, j: (0, 0)),
            pl.BlockSpec((1, D_MODEL), lambda i, j: (0, 0)),
            pl.BlockSpec((tm, RET_QK), lambda i, j: (i % s_tiles, 0)),
            pl.BlockSpec((tm, RET_QK), lambda i, j: (i % s_tiles, 0)),
            pl.BlockSpec((D_MODEL, tn), lambda i, j: (0, j)),
        ],
        out_specs=[
            pl.BlockSpec((tm, D_MODEL), lambda i, j: (i, 0)),
            pl.BlockSpec((tm, tn), lambda i, j: (i, j)),
        ],
        out_shape=[
            jax.ShapeDtypeStruct((n, D_MODEL), F32),
            jax.ShapeDtypeStruct((n, IN_COLS), BF16),
        ],
        scratch_shapes=[pltpu.VMEM((tm, D_MODEL), BF16)],
        compiler_params=pltpu.CompilerParams(
            dimension_semantics=("arbitrary", "arbitrary"), vmem_limit_bytes=VMEM_LIMIT),
    )(x2, ln_g, ln_b, cosf, sinf, w_bf)


def _ret_kernel(q_ref, k_ref, v_ref, g_ref, inner_ref, qdec_ref, kdec_ref, cdec_ref, gn_ref,
                o_ref, state_ref):
    c = pl.program_id(2)

    @pl.when(c == 0)
    def _():
        state_ref[...] = jnp.zeros_like(state_ref)

    q = q_ref[...]
    k = k_ref[...]
    v = v_ref[...]
    st = state_ref[...]
    scores = _dot_nt(q, k) * inner_ref[0]
    o = _dot(scores.astype(BF16), v) + qdec_ref[0] * _dot(q, st.astype(BF16))
    kd = (k.astype(F32) * kdec_ref[0]).astype(BF16)
    kv = lax.dot_general(kd, v, (((0,), (0,)), ((), ())), preferred_element_type=F32)
    state_ref[...] = cdec_ref[0] * st + kv

    mu = jnp.mean(o, -1, keepdims=True)
    d = o - mu
    var = jnp.mean(d * d, -1, keepdims=True)
    on = d * lax.rsqrt(var + EPS) * gn_ref[0]
    o_ref[...] = (g_ref[...].astype(F32) * on).astype(BF16)


def _ret_call(proj, inner, qdec, kdec, cdec, gn, batch, seq):
    n = proj.shape[0]
    c = RET_CHUNK
    nc = seq // c
    qb = SEG_QR // RET_QK
    kb = SEG_KR // RET_QK
    vb = SEG_VR // RET_V
    gb = SEG_GR // RET_V
    return pl.pallas_call(
        _ret_kernel,
        name="retention",
        grid=(batch, RET_HEADS, nc),
        in_specs=[
            pl.BlockSpec((c, RET_QK), lambda b, h, i: (b * nc + i, qb + h)),
            pl.BlockSpec((c, RET_QK), lambda b, h, i: (b * nc + i, kb + h)),
            pl.BlockSpec((c, RET_V), lambda b, h, i: (b * nc + i, vb + h)),
            pl.BlockSpec((c, RET_V), lambda b, h, i: (b * nc + i, gb + h)),
            pl.BlockSpec((1, c, c), lambda b, h, i: (h, 0, 0)),
            pl.BlockSpec((1, c, 1), lambda b, h, i: (h, 0, 0)),
            pl.BlockSpec((1, c, 1), lambda b, h, i: (h, 0, 0)),
            pl.BlockSpec((1, 1, RET_V), lambda b, h, i: (h, 0, 0)),
            pl.BlockSpec((1, 1, RET_V), lambda b, h, i: (h, 0, 0)),
        ],
        out_specs=pl.BlockSpec((c, RET_V), lambda b, h, i: (b * nc + i, h)),
        out_shape=jax.ShapeDtypeStruct((n, RET_HEADS * RET_V), BF16),
        scratch_shapes=[pltpu.VMEM((RET_QK, RET_V), F32)],
        compiler_params=pltpu.CompilerParams(
            dimension_semantics=("arbitrary", "arbitrary", "arbitrary"),
            vmem_limit_bytes=VMEM_LIMIT),
    )(proj, proj, proj, proj, inner, qdec, kdec, cdec, gn)


def _attn_kernel(qi_ref, ki_ref, q_ref, k_ref, v_ref, tab_ref, lq1_ref, lk1_ref, lq2_ref,
                 lk2_ref, sg_ref, o_ref, qst_ref, m_ref, l_ref, acc_ref, s_ref, p_ref):
    t = ATTN_T
    p = pl.program_id(2)
    qi = qi_ref[p]
    ki = ki_ref[p]

    @pl.when(ki == 0)
    def _():
        qt = q_ref[...].astype(F32).T
        sub = lax.broadcasted_iota(I32, qt.shape, 0)
        qst_ref[:, 0:t] = jnp.where(sub < DIFF_HD, qt, 0.0).astype(BF16)
        qst_ref[:, t:2 * t] = jnp.where(sub >= DIFF_HD, qt, 0.0).astype(BF16)
        m_ref[...] = jnp.full_like(m_ref, MASK_VALUE)
        l_ref[...] = jnp.zeros_like(l_ref)
        acc_ref[...] = jnp.zeros_like(acc_ref)

    def step(bias):
        kk = k_ref[...]
        vv = v_ref[...]
        slabs = range(0, t, 8)
        for half in range(2):
            cols = slice(half * t, (half + 1) * t)
            s = _dot(kk, qst_ref[:, cols])
            if bias is not None:
                s = s + bias()
            s_ref[half] = s
        for half in range(2):
            cols = slice(half * t, (half + 1) * t)
            mx = _tree(jnp.maximum, [s_ref[half, r:r + 8, :] for r in slabs])
            m_prev = m_ref[:, cols]
            m_new = jnp.maximum(m_prev, jnp.max(mx, 0, keepdims=True))
            alpha = jnp.exp2(m_prev - m_new)
            acc_l = None
            for r in range(0, t, 16):
                pr = jnp.exp2(s_ref[half, r:r + 16, :] - m_new)
                p_ref[half, r:r + 16, :] = pr.astype(BF16)
                part = pr[0:8] + pr[8:16]
                acc_l = part if acc_l is None else acc_l + part
            l_ref[:, cols] = alpha * l_ref[:, cols] + jnp.sum(acc_l, 0, keepdims=True)
            pv = lax.dot_general(vv, p_ref[half], (((0,), (0,)), ((), ())),
                                 preferred_element_type=F32)
            acc_ref[:, cols] = alpha * acc_ref[:, cols] + pv
            m_ref[:, cols] = m_new

    @pl.when(ki < qi - 1)
    def _():
        step(None)

    @pl.when(ki == qi - 1)
    def _():
        step(lambda: tab_ref[0, 1])

    @pl.when(ki == qi)
    def _():
        step(lambda: tab_ref[0, 0])
        lam = (jnp.exp(jnp.sum(lq1_ref[...] * lk1_ref[...], -1, keepdims=True))
               - jnp.exp(jnp.sum(lq2_ref[...] * lk2_ref[...], -1, keepdims=True)) + LAM_INIT)
        o1 = acc_ref[:, 0:t] / l_ref[:, 0:t]
        o2 = acc_ref[:, t:2 * t] / l_ref[:, t:2 * t]
        o = o1 - lam * o2
        o = o * lax.rsqrt(jnp.mean(o * o, 0, keepdims=True) + EPS) * (sg_ref[...] * (1.0 - LAM_INIT))
        o_ref[...] = o.T.astype(BF16)


def _attn_call(proj, tab, lq1, lk1, lq2, lk2, sg, batch, seq):
    n = proj.shape[0]
    t = ATTN_T
    nq = seq // t
    pairs = [(a, b) for a in range(nq) for b in range(a + 1)]
    qi_tab = jnp.asarray(np.array([a for a, _ in pairs], np.int32))
    ki_tab = jnp.asarray(np.array([b for _, b in pairs], np.int32))
    hd2 = 2 * DIFF_HD
    qb, kb, vb = SEG_QD // hd2, SEG_KD // hd2, SEG_VD // hd2
    vec = pl.BlockSpec((1, DIFF_HD), lambda b, h, p, qi, ki: (0, 0))
    grid_spec = pltpu.PrefetchScalarGridSpec(
        num_scalar_prefetch=2,
        grid=(batch, DIFF_HEADS, len(pairs)),
        in_specs=[
            pl.BlockSpec((t, hd2), lambda b, h, p, qi, ki: (b * nq + qi[p], qb + h)),
            pl.BlockSpec((t, hd2), lambda b, h, p, qi, ki: (b * nq + ki[p], kb + h)),
            pl.BlockSpec((t, hd2), lambda b, h, p, qi, ki: (b * nq + ki[p], vb + h)),
            pl.BlockSpec((1, 2, t, t), lambda b, h, p, qi, ki: (h, 0, 0, 0)),
            vec, vec, vec, vec,
            pl.BlockSpec((hd2, 1), lambda b, h, p, qi, ki: (0, 0)),
        ],
        out_specs=pl.BlockSpec((t, hd2), lambda b, h, p, qi, ki: (b * nq + qi[p], h)),
        scratch_shapes=[
            pltpu.VMEM((hd2, 2 * t), BF16),
            pltpu.VMEM((1, 2 * t), F32),
            pltpu.VMEM((1, 2 * t), F32),
            pltpu.VMEM((hd2, 2 * t), F32),
            pltpu.VMEM((2, t, t), F32),
            pltpu.VMEM((2, t, t), BF16),
        ],
    )
    return pl.pallas_call(
        _attn_kernel,
        name="diffattn",
        grid_spec=grid_spec,
        out_shape=jax.ShapeDtypeStruct((n, DIFF_HEADS * hd2), BF16),
        compiler_params=pltpu.CompilerParams(
            dimension_semantics=("arbitrary", "arbitrary", "arbitrary"),
            vmem_limit_bytes=VMEM_LIMIT),
    )(qi_tab, ki_tab, proj, proj, proj, tab, lq1, lk1, lq2, lk2, sg)


def _route(scores, bias, carry):
    e, tm = scores.shape
    neg = -jnp.inf
    choice = scores + bias
    row_g = lax.broadcasted_iota(I32, (GROUP_SIZE, tm), 0)
    group_scores = []
    for g in range(N_GROUPS):
        blk = choice[g * GROUP_SIZE:(g + 1) * GROUP_SIZE]
        m1 = jnp.max(blk, 0, keepdims=True)
        i1 = jnp.min(jnp.where(blk == m1, row_g, GROUP_SIZE), 0, keepdims=True)
        m2 = jnp.max(jnp.where(row_g == i1, neg, blk), 0, keepdims=True)
        group_scores.append(m1 + m2)
    gsc = jnp.concatenate(group_scores, 0)
    row8 = lax.broadcasted_iota(I32, (N_GROUPS, tm), 0)
    okf = jnp.zeros((N_GROUPS, tm), F32)
    for _ in range(TOPK_GROUPS):
        m = jnp.max(gsc, 0, keepdims=True)
        i = jnp.min(jnp.where(gsc == m, row8, N_GROUPS), 0, keepdims=True)
        hit = row8 == i
        okf = jnp.where(hit, 1.0, okf)
        gsc = jnp.where(hit, neg, gsc)
    masked = jnp.concatenate(
        [jnp.where(jnp.broadcast_to(okf[g:g + 1], (GROUP_SIZE, tm)) > 0.0,
                   choice[g * GROUP_SIZE:(g + 1) * GROUP_SIZE], neg)
         for g in range(N_GROUPS)], 0)
    row = lax.broadcasted_iota(I32, (e, tm), 0)
    idxs, ws = [], []
    multi = jnp.zeros((e, tm), F32)
    for _ in range(TOP_K):
        m = jnp.max(masked, 0, keepdims=True)
        i = jnp.min(jnp.where(masked == m, row, e), 0, keepdims=True)
        hit = row == i
        ws.append(jnp.sum(jnp.where(hit, scores, 0.0), 0, keepdims=True))
        idxs.append(i)
        multi = jnp.where(hit, 1.0, multi)
        masked = jnp.where(hit, neg, masked)
    a = lax.broadcasted_iota(I32, (tm, tm), 0)
    b = lax.broadcasted_iota(I32, (tm, tm), 1)
    upper = jnp.where(a < b, 1.0, 0.0).astype(BF16)
    before = _dot(multi.astype(BF16), upper) + carry
    ranks = [jnp.sum(jnp.where(row == i, before, 0.0), 0, keepdims=True).astype(I32) for i in idxs]
    new_carry = carry + jnp.sum(multi, 1, keepdims=True)
    wsum = ws[0]
    for w in ws[1:]:
        wsum = wsum + w
    ws = [w / wsum * ROUTED_SCALE for w in ws]
    return idxs, ws, ranks, new_carry


def _post_kernel(yr_ref, od_ref, g0_ref, g1_ref, h_ref, p_ref, wr_ref, wd_ref, wo_ref, l1g_ref,
                 l1b_ref, sg_ref, su_ref, sd_ref, pw_ref, pg_ref, rwt_ref, rb_ref,
                 h1_ref, base_ref, idx_ref, w_ref, rank_ref, cnt_ref, carry_ref):
    i = pl.program_id(0)

    @pl.when(i == 0)
    def _():
        carry_ref[...] = jnp.zeros_like(carry_ref)

    y_r = _dot(yr_ref[...], wr_ref[...])
    y_d = _dot(od_ref[...], wd_ref[...])
    merged = g0_ref[...].astype(F32) * y_r + g1_ref[...].astype(F32) * y_d
    mix = _dot(merged.astype(BF16), wo_ref[...])
    h1 = _layer_norm(DN_ALPHA * h_ref[...] + mix, l1g_ref[...], l1b_ref[...])
    h1_ref[...] = h1
    hb = h1.astype(BF16)

    shared = _dot((_silu(_dot(hb, sg_ref[...])) * _dot(hb, su_ref[...])).astype(BF16), sd_ref[...])
    ple = _dot(p_ref[...].astype(BF16), pw_ref[...]) * jax.nn.sigmoid(_dot(hb, pg_ref[...]))
    base_ref[...] = DN_ALPHA * h1 + shared + ple

    h_lo = (h1 - hb.astype(F32)).astype(BF16)
    rw = rwt_ref[...]
    rw_hi = rw.astype(BF16)
    rw_lo = (rw - rw_hi.astype(F32)).astype(BF16)
    logits = _dot_nt(rw_hi, hb) + (_dot_nt(rw_hi, h_lo) + _dot_nt(rw_lo, hb))
    scores = jax.nn.sigmoid(logits)
    idxs, ws, ranks, new_carry = _route(scores, rb_ref[...], carry_ref[...])
    idx_ref[...] = jnp.concatenate(idxs, 0)
    w_ref[...] = jnp.concatenate(ws, 0)
    rank_ref[...] = jnp.concatenate(ranks, 0)
    carry_ref[...] = new_carry
    cnt_ref[...] = new_carry.astype(I32)


def _post_call(yr, od, proj, h, p2, wr, wd, wo, l1g, l1b, sg, su, sd, pw, pg, rwt, rb):
    n = h.shape[0]
    tm = POST_TM
    g0b = SEG_GATE // D_MODEL

    def full(a):
        return pl.BlockSpec(a.shape, lambda i: (0,) * a.ndim)

    row = lambda w: pl.BlockSpec((tm, w), lambda i: (i, 0))
    tok = lambda: pl.BlockSpec((TOP_K, tm), lambda i: (0, i))
    return pl.pallas_call(
        _post_kernel,
        name="post",
        grid=(n // tm,),
        in_specs=[
            row(RET_HEADS * RET_V), row(D_MODEL),
            pl.BlockSpec((tm, D_MODEL), lambda i: (i, g0b)),
            pl.BlockSpec((tm, D_MODEL), lambda i: (i, g0b + 1)),
            row(D_MODEL), row(PLE_DIM),
            full(wr), full(wd), full(wo), full(l1g), full(l1b), full(sg), full(su), full(sd),
            full(pw), full(pg), full(rwt), full(rb),
        ],
        out_specs=[row(D_MODEL), row(D_MODEL), tok(), tok(), tok(),
                   pl.BlockSpec((N_EXPERTS, 1), lambda i: (0, 0))],
        out_shape=[
            jax.ShapeDtypeStruct((n, D_MODEL), F32),
            jax.ShapeDtypeStruct((n, D_MODEL), F32),
            jax.ShapeDtypeStruct((TOP_K, n), I32),
            jax.ShapeDtypeStruct((TOP_K, n), F32),
            jax.ShapeDtypeStruct((TOP_K, n), I32),
            jax.ShapeDtypeStruct((N_EXPERTS, 1), I32),
        ],
        scratch_shapes=[pltpu.VMEM((N_EXPERTS, 1), F32)],
        compiler_params=pltpu.CompilerParams(
            dimension_semantics=("arbitrary",), vmem_limit_bytes=VMEM_LIMIT),
    )(yr, od, proj, proj, h, p2, wr, wd, wo, l1g, l1b, sg, su, sd, pw, pg, rwt, rb)


def _dest_kernel(idx_ref, rank_ref, ps_ref, o_ref):
    idx = idx_ref[...]
    ps = ps_ref[...]
    td = idx.shape[1]
    row = lax.broadcasted_iota(I32, (N_EXPERTS, td), 0)
    outs = []
    for k in range(TOP_K):
        sel = jnp.sum(jnp.where(row == idx[k:k + 1], ps, 0.0), 0, keepdims=True)
        outs.append(sel.astype(I32))
    o_ref[...] = jnp.concatenate(outs, 0) + rank_ref[...]


def _dest_call(idx, rank, pstart_f):
    n = idx.shape[1]
    td = min(DEST_T, n)
    spec = pl.BlockSpec((TOP_K, td), lambda i: (0, i))
    return pl.pallas_call(
        _dest_kernel,
        name="dest",
        grid=(n // td,),
        in_specs=[spec, spec, pl.BlockSpec((N_EXPERTS, 1), lambda i: (0, 0))],
        out_specs=spec,
        out_shape=jax.ShapeDtypeStruct((TOP_K, n), I32),
        compiler_params=pltpu.CompilerParams(dimension_semantics=("arbitrary",)),
    )(idx, rank, pstart_f)


def _expert_kernel(be_ref, rt_ref, rtn_ref, h_hbm, wg_ref, wu_ref, wd_ref, y_ref,
                   buf_ref, wgb_ref, wub_ref, wdb_ref, sem):
    i = pl.program_id(0)
    nb = pl.num_programs(0)
    g = EXPERT_BLOCK
    slot = i % 2

    def issue(tok_ref, s):
        def body(r, c):
            pltpu.make_async_copy(h_hbm.at[pl.ds(tok_ref[0, 0, r], 1)],
                                  buf_ref.at[s, pl.ds(r, 1)], sem.at[s]).start()
            return c
        lax.fori_loop(0, g, body, 0, unroll=8)

    @pl.when(i == 0)
    def _():
        issue(rt_ref, 0)

    @pl.when(i + 1 < nb)
    def _():
        issue(rtn_ref, 1 - slot)

    changed = jnp.logical_or(i == 0, be_ref[i] != be_ref[jnp.maximum(i - 1, 0)])

    @pl.when(changed)
    def _():
        wgb_ref[...] = wg_ref[0].astype(BF16)
        wub_ref[...] = wu_ref[0].astype(BF16)
        wdb_ref[...] = wd_ref[0].astype(BF16)

    pltpu.make_async_copy(h_hbm.at[pl.ds(0, g)], buf_ref.at[slot], sem.at[slot]).wait()
    x = buf_ref[slot].astype(BF16)
    act = _silu(_dot(x, wgb_ref[...])) * _dot(x, wub_ref[...])
    y_ref[...] = _dot(act.astype(BF16), wdb_ref[...])


def _expert_call(block_e, row_tok3, h1, wg, wu, wd):
    nb = block_e.shape[0]
    g = EXPERT_BLOCK
    grid_spec = pltpu.PrefetchScalarGridSpec(
        num_scalar_prefetch=1,
        grid=(nb,),
        in_specs=[
            pl.BlockSpec((1, 1, g), lambda i, be: (i, 0, 0), memory_space=pltpu.SMEM),
            pl.BlockSpec((1, 1, g), lambda i, be: (jnp.minimum(i + 1, nb - 1), 0, 0),
                         memory_space=pltpu.SMEM),
            pl.BlockSpec(memory_space=pl.ANY),
            pl.BlockSpec((1, D_MODEL, D_EXPERT), lambda i, be: (be[i], 0, 0)),
            pl.BlockSpec((1, D_MODEL, D_EXPERT), lambda i, be: (be[i], 0, 0)),
            pl.BlockSpec((1, D_EXPERT, D_MODEL), lambda i, be: (be[i], 0, 0)),
        ],
        out_specs=pl.BlockSpec((g, D_MODEL), lambda i, be: (i, 0)),
        scratch_shapes=[
            pltpu.VMEM((2, g, D_MODEL), F32),
            pltpu.VMEM((D_MODEL, D_EXPERT), BF16),
            pltpu.VMEM((D_MODEL, D_EXPERT), BF16),
            pltpu.VMEM((D_EXPERT, D_MODEL), BF16),
            pltpu.SemaphoreType.DMA((2,)),
        ],
    )
    return pl.pallas_call(
        _expert_kernel,
        name="experts",
        grid_spec=grid_spec,
        out_shape=jax.ShapeDtypeStruct((nb * g, D_MODEL), F32),
        compiler_params=pltpu.CompilerParams(
            dimension_semantics=("arbitrary",), vmem_limit_bytes=VMEM_LIMIT),
    )(block_e, row_tok3, row_tok3, h1, wg, wu, wd)


def _combine_kernel(d_ref, dn_ref, y_hbm, w_ref, base_ref, g_ref, b_ref, o_ref, buf_ref, sem):
    i = pl.program_id(0)
    nt = pl.num_programs(0)
    tt = COMB_T
    slot = i % 2

    def issue(dest_ref, s):
        def body(t, c):
            for k in range(TOP_K):
                pltpu.make_async_copy(y_hbm.at[pl.ds(dest_ref[0, k, t], 1)],
                                      buf_ref.at[s, k, pl.ds(t, 1)], sem.at[s]).start()
            return c
        lax.fori_loop(0, tt, body, 0)

    @pl.when(i == 0)
    def _():
        issue(d_ref, 0)

    @pl.when(i + 1 < nt)
    def _():
        issue(dn_ref, 1 - slot)

    for k in range(TOP_K):
        pltpu.make_async_copy(y_hbm.at[pl.ds(0, tt)], buf_ref.at[slot, k], sem.at[slot]).wait()
    w = w_ref[...]
    routed = w[:, 0:1] * buf_ref[slot, 0]
    for k in range(1, TOP_K):
        routed = routed + w[:, k:k + 1] * buf_ref[slot, k]
    o_ref[...] = _layer_norm(base_ref[...] + routed, g_ref[...], b_ref[...])


def _combine_call(dest3, y, w_t, base, ln_g, ln_b):
    n = base.shape[0]
    tt = COMB_T
    nt = n // tt
    return pl.pallas_call(
        _combine_kernel,
        name="combine",
        grid=(nt,),
        in_specs=[
            pl.BlockSpec((1, TOP_K, tt), lambda i: (i, 0, 0), memory_space=pltpu.SMEM),
            pl.BlockSpec((1, TOP_K, tt), lambda i: (jnp.minimum(i + 1, nt - 1), 0, 0),
                         memory_space=pltpu.SMEM),
            pl.BlockSpec(memory_space=pl.ANY),
            pl.BlockSpec((tt, TOP_K), lambda i: (i, 0)),
            pl.BlockSpec((tt, D_MODEL), lambda i: (i, 0)),
            pl.BlockSpec((1, D_MODEL), lambda i: (0, 0)),
            pl.BlockSpec((1, D_MODEL), lambda i: (0, 0)),
        ],
        out_specs=pl.BlockSpec((tt, D_MODEL), lambda i: (i, 0)),
        out_shape=jax.ShapeDtypeStruct((n, D_MODEL), F32),
        scratch_shapes=[
            pltpu.VMEM((2, TOP_K, tt, D_MODEL), F32),
            pltpu.SemaphoreType.DMA((2,)),
        ],
        compiler_params=pltpu.CompilerParams(
            dimension_semantics=("arbitrary",), vmem_limit_bytes=VMEM_LIMIT),
    )(dest3, dest3, y, w_t, base, ln_g, ln_b)


def _xpos_tables(seq):
    half = RET_QK // 2
    pos = jnp.arange(seq, dtype=F32)
    theta = 1.0 / (XPOS_BASE ** jnp.linspace(0.0, 1.0, half, dtype=F32))
    ang = pos[:, None] * theta[None, :]
    cos = jnp.cos(ang)
    sin = jnp.sin(ang)
    return jnp.concatenate([cos, cos], -1), jnp.concatenate([-sin, sin], -1)


def _decay_tables():
    c = RET_CHUNK
    log_gamma = jnp.log(1.0 - 2.0 ** (-5.0 - jnp.arange(RET_HEADS, dtype=F32)))
    idx = jnp.arange(c, dtype=F32)
    rel = idx[:, None] - idx[None, :]
    inner = jnp.where(rel >= 0, jnp.exp(jnp.maximum(rel, 0.0)[None] * log_gamma[:, None, None]), 0.0)
    q_dec = jnp.exp((idx + 1.0)[None, :] * log_gamma[:, None])
    k_dec = jnp.exp((c - 1.0 - idx)[None, :] * log_gamma[:, None])
    c_dec = jnp.exp(c * log_gamma)
    cdec = jnp.broadcast_to(c_dec[:, None, None], (RET_HEADS, 1, RET_V))
    return inner, q_dec[:, :, None], k_dec[:, :, None], cdec


def _t5_bucket(n):
    max_exact = REL_BUCKETS // 2
    nf = jnp.maximum(n, 1).astype(F32)
    large = max_exact + (jnp.log(nf / max_exact) / math.log(REL_MAX_DIST / max_exact)
                         * (REL_BUCKETS - max_exact)).astype(I32)
    large = jnp.minimum(large, REL_BUCKETS - 1)
    return jnp.where(n < max_exact, n, large)


def _bias_tables(rel_bias, t):
    c = REL_MAX_DIST
    nb = t // c
    far = rel_bias[REL_BUCKETS - 1].astype(F32)
    d = jnp.arange(c)[None, :] - jnp.arange(c)[:, None]
    near0 = (jnp.take(rel_bias, _t5_bucket(jnp.maximum(d, 0)), axis=0).astype(F32) - far) * LOG2E
    near0 = jnp.where((d >= 0)[:, :, None], near0, MASK_VALUE)
    near1 = (jnp.take(rel_bias, _t5_bucket(d + c), axis=0).astype(F32) - far) * LOG2E
    tile0 = jnp.tile(near0, (nb, nb, 1))
    tile1 = jnp.tile(near1, (nb, nb, 1))
    kb = (jnp.arange(t) // c)[:, None, None]
    qb = (jnp.arange(t) // c)[None, :, None]
    diag = jnp.where(qb == kb, tile0,
                     jnp.where(qb == kb + 1, tile1, jnp.where(qb > kb, 0.0, MASK_VALUE)))
    left = jnp.where((qb == 0) & (kb == nb - 1), tile1, 0.0)
    return jnp.transpose(jnp.stack([diag, left], 0), (3, 0, 1, 2))


def kernel(x, p, ln_in_g, ln_in_b, rel_bias, w_in, ret_gn_g, lambda_q1, lambda_k1, lambda_q2,
           lambda_k2, diff_subln_g, w_ret_o, w_diff_o, w_o, ln1_g, ln1_b, router_w, router_bias,
           exp_w_gate, exp_w_up, exp_w_down, sh_w_gate, sh_w_up, sh_w_down, ple_w, ple_gate_w,
           ln2_g, ln2_b):
    batch, seq, d = x.shape
    n = batch * seq
    assert d == D_MODEL and seq % ATTN_T == 0 and seq % PROJ_TM == 0
    assert n % min(DEST_T, n) == 0 and n % COMB_T == 0 and n % POST_TM == 0
    x2 = x.reshape(n, d)
    row = lambda a: a.reshape(1, -1).astype(F32)

    cosf, sinf = _xpos_tables(seq)
    inner, qdec, kdec, cdec = _decay_tables()
    h, proj = _proj_call(x2, row(ln_in_g), row(ln_in_b), cosf, sinf, w_in[0].astype(BF16), seq)

    yr = _ret_call(proj, inner, qdec, kdec, cdec,
                   ret_gn_g[0].reshape(RET_HEADS, 1, RET_V).astype(F32), batch, seq)

    od = _attn_call(proj, _bias_tables(rel_bias, ATTN_T), row(lambda_q1[0]), row(lambda_k1[0]),
                    row(lambda_q2[0]), row(lambda_k2[0]),
                    diff_subln_g[0].reshape(-1, 1).astype(F32), batch, seq)

    h1, base, idx, w, rank, counts = _post_call(
        yr, od, proj, h, p[0].reshape(n, PLE_DIM),
        w_ret_o[0].astype(BF16), w_diff_o[0].astype(BF16), w_o[0].astype(BF16),
        row(ln1_g[0]), row(ln1_b[0]),
        sh_w_gate[0].astype(BF16), sh_w_up[0].astype(BF16), sh_w_down[0].astype(BF16),
        ple_w[0].astype(BF16), ple_gate_w[0].astype(BF16),
        router_w[0].T.astype(F32), router_bias[0].reshape(N_EXPERTS, 1).astype(F32))

    g = EXPERT_BLOCK
    counts = counts.reshape(N_EXPERTS)
    padded = (counts + g - 1) // g * g
    pend = jnp.cumsum(padded)
    pstart = pend - padded
    n_rows = -(-(n * TOP_K + N_EXPERTS * (g - 1)) // g) * g
    n_blocks = n_rows // g
    block_start = jnp.arange(n_blocks, dtype=I32) * g
    block_e = jnp.minimum(
        jnp.sum((pend[None, :] <= block_start[:, None]).astype(I32), axis=1),
        N_EXPERTS - 1).astype(I32)

    dest = _dest_call(idx, rank, pstart.astype(F32).reshape(N_EXPERTS, 1))
    tok = jnp.broadcast_to(jnp.arange(n, dtype=I32)[None, :], (TOP_K, n))
    row_tok = jnp.zeros((n_rows,), I32).at[dest.reshape(-1)].set(
        tok.reshape(-1), unique_indices=True, mode='drop')

    y = _expert_call(block_e, row_tok.reshape(n_blocks, 1, g), h1,
                     exp_w_gate[0], exp_w_up[0], exp_w_down[0])

    dest3 = dest.reshape(TOP_K, n // COMB_T, COMB_T).transpose(1, 0, 2)
    out = _combine_call(dest3, y, w.T, base, row(ln2_g[0]), row(ln2_b[0]))
    return out.reshape(batch, seq, d)
```

```python
import functools
import math

import numpy as np
import jax
import jax.numpy as jnp
from jax import lax
from jax.experimental import pallas as pl
from jax.experimental.pallas import tpu as pltpu

F32 = jnp.float32
BF16 = jnp.bfloat16
I32 = jnp.int32

D_MODEL = 1024
PLE_DIM = 256
RET_HEADS = 8
RET_QK = 128
RET_V = 256
RET_CHUNK = 128
XPOS_BASE = 10000.0
DIFF_HEADS = 8
DIFF_HD = 64
REL_BUCKETS = 32
REL_MAX_DIST = 128
N_EXPERTS = 256
TOP_K = 8
N_GROUPS = 8
GROUP_SIZE = N_EXPERTS // N_GROUPS
TOPK_GROUPS = 4
D_EXPERT = 256
ROUTED_SCALE = 2.5
EXPERT_BLOCK = 128
DEPTH = 1
DN_ALPHA = (2.0 * DEPTH) ** 0.25
EPS = 1e-5
LAM_INIT = 0.8 - 0.6 * math.exp(-0.3 * 0)

SEG_QR, SEG_KR, SEG_VR, SEG_GR, SEG_QD, SEG_KD, SEG_VD, SEG_GATE = (
    0, 1024, 2048, 4096, 6144, 7168, 8192, 9216)
IN_COLS = 11264

LANES = 128
ROW_SUB = D_MODEL // LANES
MASK_VALUE = -1e30
LOG2E = math.log2(math.e)
VMEM_LIMIT = 56 * 1024 * 1024

RET_SUB = 4
PROJ_TM = 512
PROJ_TN = 1024
ATTN_T = 512
ATTN_VT_ROWS = 2 * DIFF_HD + 16
POST_TM = 256
DEST_T = 2048
COMB_T = 64


def _layer_norm(x, g, b):
    mu = jnp.mean(x, -1, keepdims=True)
    d = x - mu
    var = jnp.mean(d * d, -1, keepdims=True)
    return d * lax.rsqrt(var + EPS) * g + b


def _silu(x):
    return x * jax.nn.sigmoid(x)


def _dot(a, b):
    return jnp.dot(a, b, preferred_element_type=F32)


def _tree(op, items):
    items = list(items)
    while len(items) > 1:
        items = [op(items[i], items[i + 1]) if i + 1 < len(items) else items[i]
                 for i in range(0, len(items), 2)]
    return items[0]


def _col_block(first_row, rows, s):
    return (pl.ds(first_row * ROW_SUB + s, rows, stride=ROW_SUB), slice(None))


def _one_row(ref, row):
    if isinstance(row, int):
        return ref.at[pl.ds(row * ROW_SUB, ROW_SUB), :]
    return ref.at[pl.ds(pl.multiple_of(row * ROW_SUB, ROW_SUB), ROW_SUB), :]


def _row_span(ref, first_row, rows):
    return ref.at[pl.ds(first_row * ROW_SUB, rows * ROW_SUB), :]


def _dot_nt(a, b):
    return lax.dot_general(a, b, (((1,), (1,)), ((), ())), preferred_element_type=F32)


def _proj_kernel(x_ref, g_ref, b_ref, cos_ref, sin_ref, w_ref, h_ref, o_ref, hb_ref):
    j = pl.program_id(1)

    @pl.when(j == 0)
    def _():
        h = _layer_norm(x_ref[...], g_ref[...], b_ref[...])
        h_ref[...] = h
        hb_ref[...] = h.astype(BF16)

    acc = _dot(hb_ref[...], w_ref[...])

    def rotate(scale):
        c = cos_ref[...]
        s = sin_ref[...]
        for hh in range(PROJ_TN // RET_QK):
            t = acc[:, hh * RET_QK:(hh + 1) * RET_QK]
            v = t * c + pltpu.roll(t, RET_QK // 2, 1) * s
            if scale != 1.0:
                v = v * scale
            o_ref[:, hh * RET_QK:(hh + 1) * RET_QK] = v.astype(BF16)

    @pl.when(j == SEG_QR // PROJ_TN)
    def _():
        rotate(1.0)

    @pl.when(j == SEG_KR // PROJ_TN)
    def _():
        rotate(RET_QK ** -0.5)

    is_gr = (j >= SEG_GR // PROJ_TN) & (j < SEG_QD // PROJ_TN)
    is_qd = j == SEG_QD // PROJ_TN
    is_gate = j >= SEG_GATE // PROJ_TN
    is_plain = ((j >= SEG_VR // PROJ_TN) & (j < SEG_GR // PROJ_TN)) | (
        (j >= SEG_KD // PROJ_TN) & (j < SEG_GATE // PROJ_TN))

    @pl.when(is_plain)
    def _():
        o_ref[...] = acc.astype(BF16)

    @pl.when(is_gr)
    def _():
        o_ref[...] = _silu(acc).astype(BF16)

    @pl.when(is_qd)
    def _():
        o_ref[...] = (acc * (DIFF_HD ** -0.5 * LOG2E)).astype(BF16)

    @pl.when(is_gate)
    def _():
        o_ref[...] = jax.nn.sigmoid(acc).astype(BF16)


def _proj_call(x2, ln_g, ln_b, cosf, sinf, w_bf, seq):
    n = x2.shape[0]
    tm, tn = PROJ_TM, PROJ_TN
    s_tiles = seq // tm
    return pl.pallas_call(
        _proj_kernel,
        name="proj",
        grid=(n // tm, IN_COLS // tn),
        in_specs=[
            pl.BlockSpec((tm, D_MODEL), lambda i, j: (i, 0)),
            pl.BlockSpec((1, D_MODEL), lambda i, j: (0, 0)),
            pl.BlockSpec((1, D_MODEL), lambda i, j: (0, 0)),
            pl.BlockSpec((tm, RET_QK), lambda i, j: (i % s_tiles, 0)),
            pl.BlockSpec((tm, RET_QK), lambda i, j: (i % s_tiles, 0)),
            pl.BlockSpec((D_MODEL, tn), lambda i, j: (0, j)),
        ],
        out_specs=[
            pl.BlockSpec((tm, D_MODEL), lambda i, j: (i, 0)),
            pl.BlockSpec((tm, tn), lambda i, j: (i, j)),
        ],
        out_shape=[
            jax.ShapeDtypeStruct((n, D_MODEL), F32),
            jax.ShapeDtypeStruct((n, IN_COLS), BF16),
        ],
        scratch_shapes=[pltpu.VMEM((tm, D_MODEL), BF16)],
        compiler_params=pltpu.CompilerParams(
            dimension_semantics=("arbitrary", "arbitrary"), vmem_limit_bytes=VMEM_LIMIT),
    )(x2, ln_g, ln_b, cosf, sinf, w_bf)


def _ret_kernel(q_ref, k_ref, v_ref, g_ref, inner_ref, qdec_ref, kdec_ref, cdec_ref, gn_ref,
                o_ref, state_ref):
    c = pl.program_id(2)

    @pl.when(c == 0)
    def _():
        state_ref[...] = jnp.zeros_like(state_ref)

    st = state_ref[...]
    for j in range(RET_SUB):
        rows = slice(j * RET_CHUNK, (j + 1) * RET_CHUNK)
        q = q_ref[rows, :]
        k = k_ref[rows, :]
        v = v_ref[rows, :]
        scores = _dot_nt(q, k) * inner_ref[0]
        o = _dot(scores.astype(BF16), v) + qdec_ref[0] * _dot(q, st.astype(BF16))
        kd = (k.astype(F32) * kdec_ref[0]).astype(BF16)
        kv = lax.dot_general(kd, v, (((0,), (0,)), ((), ())), preferred_element_type=F32)
        st = cdec_ref[0] * st + kv

        mu = jnp.mean(o, -1, keepdims=True)
        d = o - mu
        var = jnp.mean(d * d, -1, keepdims=True)
        on = d * lax.rsqrt(var + EPS) * gn_ref[0]
        o_ref[rows, :] = (g_ref[rows, :].astype(F32) * on).astype(BF16)
    state_ref[...] = st


def _ret_call(proj, inner, qdec, kdec, cdec, gn, batch, seq):
    n = proj.shape[0]
    c = RET_CHUNK
    rows = RET_SUB * c
    nc = seq // rows
    qb = SEG_QR // RET_QK
    kb = SEG_KR // RET_QK
    vb = SEG_VR // RET_V
    gb = SEG_GR // RET_V
    return pl.pallas_call(
        _ret_kernel,
        name="retention",
        grid=(batch, RET_HEADS, nc),
        in_specs=[
            pl.BlockSpec((rows, RET_QK), lambda b, h, i: (b * nc + i, qb + h)),
            pl.BlockSpec((rows, RET_QK), lambda b, h, i: (b * nc + i, kb + h)),
            pl.BlockSpec((rows, RET_V), lambda b, h, i: (b * nc + i, vb + h)),
            pl.BlockSpec((rows, RET_V), lambda b, h, i: (b * nc + i, gb + h)),
            pl.BlockSpec((1, c, c), lambda b, h, i: (h, 0, 0)),
            pl.BlockSpec((1, c, 1), lambda b, h, i: (h, 0, 0)),
            pl.BlockSpec((1, c, 1), lambda b, h, i: (h, 0, 0)),
            pl.BlockSpec((1, 1, RET_V), lambda b, h, i: (h, 0, 0)),
            pl.BlockSpec((1, 1, RET_V), lambda b, h, i: (h, 0, 0)),
        ],
        out_specs=pl.BlockSpec((rows, RET_V), lambda b, h, i: (b * nc + i, h)),
        out_shape=jax.ShapeDtypeStruct((n, RET_HEADS * RET_V), BF16),
        scratch_shapes=[pltpu.VMEM((RET_QK, RET_V), F32)],
        compiler_params=pltpu.CompilerParams(
            dimension_semantics=("arbitrary", "arbitrary", "arbitrary"),
            vmem_limit_bytes=VMEM_LIMIT),
    )(proj, proj, proj, proj, inner, qdec, kdec, cdec, gn)


def _attn_kernel(qi_ref, ki_ref, q_ref, k_ref, vt_ref, tab_ref, lq1_ref, lk1_ref, lq2_ref,
                 lk2_ref, sg_ref, o_ref, qst_ref, m_ref, acc_ref, s_ref, p_ref):
    t = ATTN_T
    hd2 = 2 * DIFF_HD
    p = pl.program_id(2)
    qi = qi_ref[p]
    ki = ki_ref[p]

    @pl.when(ki == 0)
    def _():
        qt = q_ref[...].astype(F32).T
        sub = lax.broadcasted_iota(I32, qt.shape, 0)
        qst_ref[:, 0:t] = jnp.where(sub < DIFF_HD, qt, 0.0).astype(BF16)
        qst_ref[:, t:2 * t] = jnp.where(sub >= DIFF_HD, qt, 0.0).astype(BF16)
        m_ref[...] = jnp.full_like(m_ref, MASK_VALUE)
        acc_ref[...] = jnp.zeros_like(acc_ref)

    def step(bias):
        kk = k_ref[...]
        vt = vt_ref[...]
        for half in range(2):
            cols = slice(half * t, (half + 1) * t)
            s = _dot(kk, qst_ref[:, cols])
            if bias is not None:
                s = s + bias()
            s_ref[half] = s
        for half in range(2):
            cols = slice(half * t, (half + 1) * t)
            mx = _tree(jnp.maximum, [s_ref[half, r:r + 8, :] for r in range(0, t, 8)])
            m_prev = m_ref[:, cols]
            m_new = jnp.maximum(m_prev, jnp.max(mx, 0, keepdims=True))
            alpha = jnp.exp2(m_prev - m_new)
            for r in range(0, t, 16):
                p_ref[half, r:r + 16, :] = jnp.exp2(s_ref[half, r:r + 16, :] - m_new).astype(BF16)
            acc_ref[:, cols] = alpha * acc_ref[:, cols] + _dot(vt, p_ref[half])
            m_ref[:, cols] = m_new

    @pl.when(ki < qi - 1)
    def _():
        step(None)

    @pl.when(ki == qi - 1)
    def _():
        step(lambda: tab_ref[0, 1])

    @pl.when(ki == qi)
    def _():
        step(lambda: tab_ref[0, 0])
        lam = (jnp.exp(jnp.sum(lq1_ref[...] * lk1_ref[...], -1, keepdims=True))
               - jnp.exp(jnp.sum(lq2_ref[...] * lk2_ref[...], -1, keepdims=True)) + LAM_INIT)
        o1 = acc_ref[0:hd2, 0:t] / acc_ref[hd2:hd2 + 1, 0:t]
        o2 = acc_ref[0:hd2, t:2 * t] / acc_ref[hd2:hd2 + 1, t:2 * t]
        o = o1 - lam * o2
        o = o * lax.rsqrt(jnp.mean(o * o, 0, keepdims=True) + EPS) * (sg_ref[...] * (1.0 - LAM_INIT))
        o_ref[...] = o.T.astype(BF16)


def _attn_call(proj, tab, lq1, lk1, lq2, lk2, sg, batch, seq):
    n = proj.shape[0]
    t = ATTN_T
    nq = seq // t
    pairs = [(a, b) for a in range(nq) for b in range(a + 1)]
    qi_tab = jnp.asarray(np.array([a for a, _ in pairs], np.int32))
    ki_tab = jnp.asarray(np.array([b for _, b in pairs], np.int32))
    hd2 = 2 * DIFF_HD
    qb, kb = SEG_QD // hd2, SEG_KD // hd2
    v = proj[:, SEG_VD:SEG_VD + DIFF_HEADS * hd2].reshape(batch, seq, DIFF_HEADS, hd2)
    vt = jnp.transpose(v, (0, 2, 3, 1))
    extra = jnp.zeros((batch, DIFF_HEADS, ATTN_VT_ROWS - hd2, seq), BF16).at[:, :, 0, :].set(1.0)
    vt = jnp.concatenate([vt, extra], axis=2).reshape(batch * DIFF_HEADS * ATTN_VT_ROWS, seq)
    vec = pl.BlockSpec((1, DIFF_HD), lambda b, h, p, qi, ki: (0, 0))
    grid_spec = pltpu.PrefetchScalarGridSpec(
        num_scalar_prefetch=2,
        grid=(batch, DIFF_HEADS, len(pairs)),
        in_specs=[
            pl.BlockSpec((t, hd2), lambda b, h, p, qi, ki: (b * nq + qi[p], qb + h)),
            pl.BlockSpec((t, hd2), lambda b, h, p, qi, ki: (b * nq + ki[p], kb + h)),
            pl.BlockSpec((ATTN_VT_ROWS, t), lambda b, h, p, qi, ki: (b * DIFF_HEADS + h, ki[p])),
            pl.BlockSpec((1, 2, t, t), lambda b, h, p, qi, ki: (h, 0, 0, 0)),
            vec, vec, vec, vec,
            pl.BlockSpec((hd2, 1), lambda b, h, p, qi, ki: (0, 0)),
        ],
        out_specs=pl.BlockSpec((t, hd2), lambda b, h, p, qi, ki: (b * nq + qi[p], h)),
        scratch_shapes=[
            pltpu.VMEM((hd2, 2 * t), BF16),
            pltpu.VMEM((1, 2 * t), F32),
            pltpu.VMEM((ATTN_VT_ROWS, 2 * t), F32),
            pltpu.VMEM((2, t, t), F32),
            pltpu.VMEM((2, t, t), BF16),
        ],
    )
    return pl.pallas_call(
        _attn_kernel,
        name="diffattn",
        grid_spec=grid_spec,
        out_shape=jax.ShapeDtypeStruct((n, DIFF_HEADS * hd2), BF16),
        compiler_params=pltpu.CompilerParams(
            dimension_semantics=("arbitrary", "arbitrary", "arbitrary"),
            vmem_limit_bytes=VMEM_LIMIT),
    )(qi_tab, ki_tab, proj, proj, vt, tab, lq1, lk1, lq2, lk2, sg)


def _route(scores, bias, carry):
    e, tm = scores.shape
    neg = -jnp.inf
    choice = scores + bias
    row_g = lax.broadcasted_iota(I32, (GROUP_SIZE, tm), 0)
    group_scores = []
    for g in range(N_GROUPS):
        blk = choice[g * GROUP_SIZE:(g + 1) * GROUP_SIZE]
        m1 = jnp.max(blk, 0, keepdims=True)
        i1 = jnp.min(jnp.where(blk == m1, row_g, GROUP_SIZE), 0, keepdims=True)
        m2 = jnp.max(jnp.where(row_g == i1, neg, blk), 0, keepdims=True)
        group_scores.append(m1 + m2)
    gsc = jnp.concatenate(group_scores, 0)
    row8 = lax.broadcasted_iota(I32, (N_GROUPS, tm), 0)
    okf = jnp.zeros((N_GROUPS, tm), F32)
    for _ in range(TOPK_GROUPS):
        m = jnp.max(gsc, 0, keepdims=True)
        i = jnp.min(jnp.where(gsc == m, row8, N_GROUPS), 0, keepdims=True)
        hit = row8 == i
        okf = jnp.where(hit, 1.0, okf)
        gsc = jnp.where(hit, neg, gsc)
    masked = jnp.concatenate(
        [jnp.where(jnp.broadcast_to(okf[g:g + 1], (GROUP_SIZE, tm)) > 0.0,
                   choice[g * GROUP_SIZE:(g + 1) * GROUP_SIZE], neg)
         for g in range(N_GROUPS)], 0)
    row = lax.broadcasted_iota(I32, (e, tm), 0)
    idxs, ws = [], []
    multi = jnp.zeros((e, tm), F32)
    for _ in range(TOP_K):
        m = jnp.max(masked, 0, keepdims=True)
        i = jnp.min(jnp.where(masked == m, row, e), 0, keepdims=True)
        hit = row == i
        ws.append(jnp.sum(jnp.where(hit, scores, 0.0), 0, keepdims=True))
        idxs.append(i)
        multi = jnp.where(hit, 1.0, multi)
        masked = jnp.where(hit, neg, masked)
    a = lax.broadcasted_iota(I32, (tm, tm), 0)
    b = lax.broadcasted_iota(I32, (tm, tm), 1)
    upper = jnp.where(a < b, 1.0, 0.0).astype(BF16)
    before = _dot(multi.astype(BF16), upper) + carry
    ranks = [jnp.sum(jnp.where(row == i, before, 0.0), 0, keepdims=True).astype(I32) for i in idxs]
    new_carry = carry + jnp.sum(multi, 1, keepdims=True)
    wsum = ws[0]
    for w in ws[1:]:
        wsum = wsum + w
    ws = [w / wsum * ROUTED_SCALE for w in ws]
    return idxs, ws, ranks, new_carry


def _post_kernel(yr_ref, od_ref, g0_ref, g1_ref, h_ref, p_ref, wr_ref, wd_ref, wo_ref, l1g_ref,
                 l1b_ref, sg_ref, su_ref, sd_ref, pw_ref, pg_ref, rwt_ref, rb_ref,
                 h1_ref, base_ref, idx_ref, w_ref, rank_ref, cnt_ref, carry_ref):
    i = pl.program_id(0)

    @pl.when(i == 0)
    def _():
        carry_ref[...] = jnp.zeros_like(carry_ref)

    y_r = _dot(yr_ref[...], wr_ref[...])
    y_d = _dot(od_ref[...], wd_ref[...])
    merged = g0_ref[...].astype(F32) * y_r + g1_ref[...].astype(F32) * y_d
    mix = _dot(merged.astype(BF16), wo_ref[...])
    h1 = _layer_norm(DN_ALPHA * h_ref[...] + mix, l1g_ref[...], l1b_ref[...])
    for s in range(ROW_SUB):
        h1_ref[_col_block(0, h1.shape[0], s)] = h1[:, s * LANES:(s + 1) * LANES]
    hb = h1.astype(BF16)

    shared = _dot((_silu(_dot(hb, sg_ref[...])) * _dot(hb, su_ref[...])).astype(BF16), sd_ref[...])
    ple = _dot(p_ref[...].astype(BF16), pw_ref[...]) * jax.nn.sigmoid(_dot(hb, pg_ref[...]))
    base_ref[...] = DN_ALPHA * h1 + shared + ple

    h_lo = (h1 - hb.astype(F32)).astype(BF16)
    rw = rwt_ref[...]
    rw_hi = rw.astype(BF16)
    rw_lo = (rw - rw_hi.astype(F32)).astype(BF16)
    logits = _dot_nt(rw_hi, hb) + (_dot_nt(rw_hi, h_lo) + _dot_nt(rw_lo, hb))
    scores = jax.nn.sigmoid(logits)
    idxs, ws, ranks, new_carry = _route(scores, rb_ref[...], carry_ref[...])
    idx_ref[...] = jnp.concatenate(idxs, 0)
    w_ref[...] = jnp.concatenate(ws, 0)
    rank_ref[...] = jnp.concatenate(ranks, 0)
    carry_ref[...] = new_carry
    cnt_ref[...] = new_carry.astype(I32)


def _post_call(yr, od, proj, h, p2, wr, wd, wo, l1g, l1b, sg, su, sd, pw, pg, rwt, rb):
    n = h.shape[0]
    tm = POST_TM
    g0b = SEG_GATE // D_MODEL

    def full(a):
        return pl.BlockSpec(a.shape, lambda i: (0,) * a.ndim)

    row = lambda w: pl.BlockSpec((tm, w), lambda i: (i, 0))
    tok = lambda: pl.BlockSpec((TOP_K, tm), lambda i: (0, i))
    return pl.pallas_call(
        _post_kernel,
        name="post",
        grid=(n // tm,),
        in_specs=[
            row(RET_HEADS * RET_V), row(D_MODEL),
            pl.BlockSpec((tm, D_MODEL), lambda i: (i, g0b)),
            pl.BlockSpec((tm, D_MODEL), lambda i: (i, g0b + 1)),
            row(D_MODEL), row(PLE_DIM),
            full(wr), full(wd), full(wo), full(l1g), full(l1b), full(sg), full(su), full(sd),
            full(pw), full(pg), full(rwt), full(rb),
        ],
        out_specs=[pl.BlockSpec((tm * ROW_SUB, LANES), lambda i: (i, 0)), row(D_MODEL),
                   tok(), tok(), tok(), pl.BlockSpec((N_EXPERTS, 1), lambda i: (0, 0))],
        out_shape=[
            jax.ShapeDtypeStruct((n * ROW_SUB, LANES), F32),
            jax.ShapeDtypeStruct((n, D_MODEL), F32),
            jax.ShapeDtypeStruct((TOP_K, n), I32),
            jax.ShapeDtypeStruct((TOP_K, n), F32),
            jax.ShapeDtypeStruct((TOP_K, n), I32),
            jax.ShapeDtypeStruct((N_EXPERTS, 1), I32),
        ],
        scratch_shapes=[pltpu.VMEM((N_EXPERTS, 1), F32)],
        compiler_params=pltpu.CompilerParams(
            dimension_semantics=("arbitrary",), vmem_limit_bytes=VMEM_LIMIT),
    )(yr, od, proj, proj, h, p2, wr, wd, wo, l1g, l1b, sg, su, sd, pw, pg, rwt, rb)


def _dest_kernel(idx_ref, rank_ref, ps_ref, o_ref):
    idx = idx_ref[...]
    ps = ps_ref[...]
    td = idx.shape[1]
    row = lax.broadcasted_iota(I32, (N_EXPERTS, td), 0)
    outs = []
    for k in range(TOP_K):
        sel = jnp.sum(jnp.where(row == idx[k:k + 1], ps, 0.0), 0, keepdims=True)
        outs.append(sel.astype(I32))
    o_ref[...] = jnp.concatenate(outs, 0) + rank_ref[...]


def _dest_call(idx, rank, pstart_f):
    n = idx.shape[1]
    td = min(DEST_T, n)
    spec = pl.BlockSpec((TOP_K, td), lambda i: (0, i))
    return pl.pallas_call(
        _dest_kernel,
        name="dest",
        grid=(n // td,),
        in_specs=[spec, spec, pl.BlockSpec((N_EXPERTS, 1), lambda i: (0, 0))],
        out_specs=spec,
        out_shape=jax.ShapeDtypeStruct((TOP_K, n), I32),
        compiler_params=pltpu.CompilerParams(dimension_semantics=("arbitrary",)),
    )(idx, rank, pstart_f)


def _expert_kernel(be_ref, rt_ref, rtn_ref, h_hbm, wg_ref, wu_ref, wd_ref, y_ref,
                   buf_ref, wgb_ref, wub_ref, wdb_ref, sem):
    i = pl.program_id(0)
    nb = pl.num_programs(0)
    g = EXPERT_BLOCK

    def wait(s):
        pltpu.make_async_copy(_row_span(h_hbm, 0, g), _row_span(buf_ref, s * g, g), sem.at[s]).wait()

    @pl.when(i == 0)
    def _():
        def body(r, c):
            pltpu.make_async_copy(_one_row(h_hbm, rt_ref[0, 0, r]), _one_row(buf_ref, r),
                                  sem.at[0]).start()
            return c
        lax.fori_loop(0, g, body, 0, unroll=8)

    changed = jnp.logical_or(i == 0, be_ref[i] != be_ref[jnp.maximum(i - 1, 0)])

    @pl.when(changed)
    def _():
        wgb_ref[...] = wg_ref[0].astype(BF16)
        wub_ref[...] = wu_ref[0].astype(BF16)
        wdb_ref[...] = wd_ref[0].astype(BF16)

    def block(slot):
        wait(slot)
        x = jnp.concatenate([buf_ref[_col_block(slot * g, g, s)] for s in range(ROW_SUB)],
                            axis=1).astype(BF16)
        for r in range(g):
            pltpu.make_async_copy(_one_row(h_hbm, rtn_ref[0, 0, r]),
                                  _one_row(buf_ref, (1 - slot) * g + r), sem.at[1 - slot]).start()
        act = _silu(_dot(x, wgb_ref[...])) * _dot(x, wub_ref[...])
        y = _dot(act.astype(BF16), wdb_ref[...])
        for s in range(ROW_SUB):
            y_ref[_col_block(0, g, s)] = y[:, s * LANES:(s + 1) * LANES]

        @pl.when(i == nb - 1)
        def _():
            wait(1 - slot)

    for parity in range(2):
        pl.when(i % 2 == parity)(functools.partial(block, parity))


def _expert_call(block_e, row_tok3, h1, wg, wu, wd):
    nb = block_e.shape[0]
    g = EXPERT_BLOCK
    grid_spec = pltpu.PrefetchScalarGridSpec(
        num_scalar_prefetch=1,
        grid=(nb,),
        in_specs=[
            pl.BlockSpec((1, 1, g), lambda i, be: (i, 0, 0), memory_space=pltpu.SMEM),
            pl.BlockSpec((1, 1, g), lambda i, be: (jnp.minimum(i + 1, nb - 1), 0, 0),
                         memory_space=pltpu.SMEM),
            pl.BlockSpec(memory_space=pl.ANY),
            pl.BlockSpec((1, D_MODEL, D_EXPERT), lambda i, be: (be[i], 0, 0)),
            pl.BlockSpec((1, D_MODEL, D_EXPERT), lambda i, be: (be[i], 0, 0)),
            pl.BlockSpec((1, D_EXPERT, D_MODEL), lambda i, be: (be[i], 0, 0)),
        ],
        out_specs=pl.BlockSpec((g * ROW_SUB, LANES), lambda i, be: (i, 0)),
        scratch_shapes=[
            pltpu.VMEM((2 * g * ROW_SUB, LANES), F32),
            pltpu.VMEM((D_MODEL, D_EXPERT), BF16),
            pltpu.VMEM((D_MODEL, D_EXPERT), BF16),
            pltpu.VMEM((D_EXPERT, D_MODEL), BF16),
            pltpu.SemaphoreType.DMA((2,)),
        ],
    )
    return pl.pallas_call(
        _expert_kernel,
        name="experts",
        grid_spec=grid_spec,
        out_shape=jax.ShapeDtypeStruct((nb * g * ROW_SUB, LANES), F32),
        compiler_params=pltpu.CompilerParams(
            dimension_semantics=("arbitrary",), vmem_limit_bytes=VMEM_LIMIT),
    )(block_e, row_tok3, row_tok3, h1, wg, wu, wd)


def _combine_kernel(d_ref, dn_ref, y_hbm, w_ref, base_ref, g_ref, b_ref, o_ref, buf_ref, sem):
    i = pl.program_id(0)
    nt = pl.num_programs(0)
    tt = COMB_T

    def plane(s, k):
        return (s * TOP_K + k) * tt

    def wait(s):
        for k in range(TOP_K):
            pltpu.make_async_copy(_row_span(y_hbm, 0, tt), _row_span(buf_ref, plane(s, k), tt),
                                  sem.at[s]).wait()

    @pl.when(i == 0)
    def _():
        def body(t, c):
            for k in range(TOP_K):
                pltpu.make_async_copy(_one_row(y_hbm, d_ref[0, k, t]),
                                      _one_row(buf_ref, plane(0, k) + t), sem.at[0]).start()
            return c
        lax.fori_loop(0, tt, body, 0)

    def tile(slot):
        wait(slot)
        for t in range(tt):
            for k in range(TOP_K):
                pltpu.make_async_copy(_one_row(y_hbm, dn_ref[0, k, t]),
                                      _one_row(buf_ref, plane(1 - slot, k) + t),
                                      sem.at[1 - slot]).start()
        w = w_ref[...]
        parts = []
        for s in range(ROW_SUB):
            part = w[:, 0:1] * buf_ref[_col_block(plane(slot, 0), tt, s)]
            for k in range(1, TOP_K):
                part = part + w[:, k:k + 1] * buf_ref[_col_block(plane(slot, k), tt, s)]
            parts.append(part)
        routed = jnp.concatenate(parts, axis=1)
        o_ref[...] = _layer_norm(base_ref[...] + routed, g_ref[...], b_ref[...])

        @pl.when(i == nt - 1)
        def _():
            wait(1 - slot)

    for parity in range(2):
        pl.when(i % 2 == parity)(functools.partial(tile, parity))


def _combine_call(dest3, y, w_t, base, ln_g, ln_b):
    n = base.shape[0]
    tt = COMB_T
    nt = n // tt
    return pl.pallas_call(
        _combine_kernel,
        name="combine",
        grid=(nt,),
        in_specs=[
            pl.BlockSpec((1, TOP_K, tt), lambda i: (i, 0, 0), memory_space=pltpu.SMEM),
            pl.BlockSpec((1, TOP_K, tt), lambda i: (jnp.minimum(i + 1, nt - 1), 0, 0),
                         memory_space=pltpu.SMEM),
            pl.BlockSpec(memory_space=pl.ANY),
            pl.BlockSpec((tt, TOP_K), lambda i: (i, 0)),
            pl.BlockSpec((tt, D_MODEL), lambda i: (i, 0)),
            pl.BlockSpec((1, D_MODEL), lambda i: (0, 0)),
            pl.BlockSpec((1, D_MODEL), lambda i: (0, 0)),
        ],
        out_specs=pl.BlockSpec((tt, D_MODEL), lambda i: (i, 0)),
        out_shape=jax.ShapeDtypeStruct((n, D_MODEL), F32),
        scratch_shapes=[
            pltpu.VMEM((2 * TOP_K * tt * ROW_SUB, LANES), F32),
            pltpu.SemaphoreType.DMA((2,)),
        ],
        compiler_params=pltpu.CompilerParams(
            dimension_semantics=("arbitrary",), vmem_limit_bytes=VMEM_LIMIT),
    )(dest3, dest3, y, w_t, base, ln_g, ln_b)


def _xpos_tables(seq):
    half = RET_QK // 2
    pos = jnp.arange(seq, dtype=F32)
    theta = 1.0 / (XPOS_BASE ** jnp.linspace(0.0, 1.0, half, dtype=F32))
    ang = pos[:, None] * theta[None, :]
    cos = jnp.cos(ang)
    sin = jnp.sin(ang)
    return jnp.concatenate([cos, cos], -1), jnp.concatenate([-sin, sin], -1)


def _decay_tables():
    c = RET_CHUNK
    log_gamma = jnp.log(1.0 - 2.0 ** (-5.0 - jnp.arange(RET_HEADS, dtype=F32)))
    idx = jnp.arange(c, dtype=F32)
    rel = idx[:, None] - idx[None, :]
    inner = jnp.where(rel >= 0, jnp.exp(jnp.maximum(rel, 0.0)[None] * log_gamma[:, None, None]), 0.0)
    q_dec = jnp.exp((idx + 1.0)[None, :] * log_gamma[:, None])
    k_dec = jnp.exp((c - 1.0 - idx)[None, :] * log_gamma[:, None])
    c_dec = jnp.exp(c * log_gamma)
    cdec = jnp.broadcast_to(c_dec[:, None, None], (RET_HEADS, 1, RET_V))
    return inner, q_dec[:, :, None], k_dec[:, :, None], cdec


def _t5_bucket(n):
    max_exact = REL_BUCKETS // 2
    nf = jnp.maximum(n, 1).astype(F32)
    large = max_exact + (jnp.log(nf / max_exact) / math.log(REL_MAX_DIST / max_exact)
                         * (REL_BUCKETS - max_exact)).astype(I32)
    large = jnp.minimum(large, REL_BUCKETS - 1)
    return jnp.where(n < max_exact, n, large)


def _bias_tables(rel_bias, t):
    c = REL_MAX_DIST
    nb = t // c
    far = rel_bias[REL_BUCKETS - 1].astype(F32)
    d = jnp.arange(c)[None, :] - jnp.arange(c)[:, None]
    near0 = (jnp.take(rel_bias, _t5_bucket(jnp.maximum(d, 0)), axis=0).astype(F32) - far) * LOG2E
    near0 = jnp.where((d >= 0)[:, :, None], near0, MASK_VALUE)
    near1 = (jnp.take(rel_bias, _t5_bucket(d + c), axis=0).astype(F32) - far) * LOG2E
    tile0 = jnp.tile(near0, (nb, nb, 1))
    tile1 = jnp.tile(near1, (nb, nb, 1))
    kb = (jnp.arange(t) // c)[:, None, None]
    qb = (jnp.arange(t) // c)[None, :, None]
    diag = jnp.where(qb == kb, tile0,
                     jnp.where(qb == kb + 1, tile1, jnp.where(qb > kb, 0.0, MASK_VALUE)))
    left = jnp.where((qb == 0) & (kb == nb - 1), tile1, 0.0)
    return jnp.transpose(jnp.stack([diag, left], 0), (3, 0, 1, 2))


def kernel(x, p, ln_in_g, ln_in_b, rel_bias, w_in, ret_gn_g, lambda_q1, lambda_k1, lambda_q2,
           lambda_k2, diff_subln_g, w_ret_o, w_diff_o, w_o, ln1_g, ln1_b, router_w, router_bias,
           exp_w_gate, exp_w_up, exp_w_down, sh_w_gate, sh_w_up, sh_w_down, ple_w, ple_gate_w,
           ln2_g, ln2_b):
    batch, seq, d = x.shape
    n = batch * seq
    assert d == D_MODEL and seq % ATTN_T == 0 and seq % PROJ_TM == 0
    assert n % min(DEST_T, n) == 0 and n % COMB_T == 0 and n % POST_TM == 0
    x2 = x.reshape(n, d)
    row = lambda a: a.reshape(1, -1).astype(F32)

    cosf, sinf = _xpos_tables(seq)
    inner, qdec, kdec, cdec = _decay_tables()
    h, proj = _proj_call(x2, row(ln_in_g), row(ln_in_b), cosf, sinf, w_in[0].astype(BF16), seq)

    yr = _ret_call(proj, inner, qdec, kdec, cdec,
                   ret_gn_g[0].reshape(RET_HEADS, 1, RET_V).astype(F32), batch, seq)

    od = _attn_call(proj, _bias_tables(rel_bias, ATTN_T), row(lambda_q1[0]), row(lambda_k1[0]),
                    row(lambda_q2[0]), row(lambda_k2[0]),
                    diff_subln_g[0].reshape(-1, 1).astype(F32), batch, seq)

    h1, base, idx, w, rank, counts = _post_call(
        yr, od, proj, h, p[0].reshape(n, PLE_DIM),
        w_ret_o[0].astype(BF16), w_diff_o[0].astype(BF16), w_o[0].astype(BF16),
        row(ln1_g[0]), row(ln1_b[0]),
        sh_w_gate[0].astype(BF16), sh_w_up[0].astype(BF16), sh_w_down[0].astype(BF16),
        ple_w[0].astype(BF16), ple_gate_w[0].astype(BF16),
        router_w[0].T.astype(F32), router_bias[0].reshape(N_EXPERTS, 1).astype(F32))

    g = EXPERT_BLOCK
    counts = counts.reshape(N_EXPERTS)
    padded = (counts + g - 1) // g * g
    pend = jnp.cumsum(padded)
    pstart = pend - padded
    n_rows = -(-(n * TOP_K + N_EXPERTS * (g - 1)) // g) * g
    n_blocks = n_rows // g
    block_start = jnp.arange(n_blocks, dtype=I32) * g
    block_e = jnp.minimum(
        jnp.sum((pend[None, :] <= block_start[:, None]).astype(I32), axis=1),
        N_EXPERTS - 1).astype(I32)

    dest = _dest_call(idx, rank, pstart.astype(F32).reshape(N_EXPERTS, 1))
    tok = jnp.broadcast_to(jnp.arange(n, dtype=I32)[None, :], (TOP_K, n))
    row_tok = jnp.zeros((n_rows,), I32).at[dest.reshape(-1)].set(
        tok.reshape(-1), unique_indices=True, mode='drop')

    y = _expert_call(block_e, row_tok.reshape(n_blocks, 1, g), h1,
                     exp_w_gate[0], exp_w_up[0], exp_w_down[0])

    dest3 = dest.reshape(TOP_K, n // COMB_T, COMB_T).transpose(1, 0, 2)
    out = _combine_call(dest3, y, w.T, base, row(ln2_g[0]), row(ln2_b[0]))
    return out.reshape(batch, seq, d)
```

```python
import functools
import math

import numpy as np
import jax
import jax.numpy as jnp
from jax import lax
from jax.experimental import pallas as pl
from jax.experimental.pallas import tpu as pltpu

F32 = jnp.float32
BF16 = jnp.bfloat16
I32 = jnp.int32

D_MODEL = 1024
PLE_DIM = 256
RET_HEADS = 8
RET_QK = 128
RET_V = 256
RET_CHUNK = 128
XPOS_BASE = 10000.0
DIFF_HEADS = 8
DIFF_HD = 64
REL_BUCKETS = 32
REL_MAX_DIST = 128
N_EXPERTS = 256
TOP_K = 8
N_GROUPS = 8
GROUP_SIZE = N_EXPERTS // N_GROUPS
TOPK_GROUPS = 4
D_EXPERT = 256
ROUTED_SCALE = 2.5
EXPERT_BLOCK = 128
DEPTH = 1
DN_ALPHA = (2.0 * DEPTH) ** 0.25
EPS = 1e-5
LAM_INIT = 0.8 - 0.6 * math.exp(-0.3 * 0)

SEG_QR, SEG_KR, SEG_VR, SEG_GR, SEG_QD, SEG_KD, SEG_VD, SEG_GATE = (
    0, 1024, 2048, 4096, 6144, 7168, 8192, 9216)
IN_COLS = 11264

LANES = 128
ROW_SUB = D_MODEL // LANES
MASK_VALUE = -1e30
LOG2E = math.log2(math.e)
VMEM_LIMIT = 56 * 1024 * 1024

RET_SUB = 4
PROJ_TM = 1024
PROJ_TN = 1024
ATTN_T = 512
ATTN_KSUB = 2
ATTN_VT_ROWS = 2 * DIFF_HD + 16
POST_TM = 256
DEST_T = 2048
COMB_T = 32
GATHER_AHEAD = 3
GATHER_SLOTS = GATHER_AHEAD + 1


def _layer_norm(x, g, b):
    mu = jnp.mean(x, -1, keepdims=True)
    d = x - mu
    var = jnp.mean(d * d, -1, keepdims=True)
    return d * lax.rsqrt(var + EPS) * g + b


def _silu(x):
    return x * jax.nn.sigmoid(x)


def _dot(a, b):
    return jnp.dot(a, b, preferred_element_type=F32)


def _tree(op, items):
    items = list(items)
    while len(items) > 1:
        items = [op(items[i], items[i + 1]) if i + 1 < len(items) else items[i]
                 for i in range(0, len(items), 2)]
    return items[0]


def _col_block(first_row, rows, s):
    return (pl.ds(first_row * ROW_SUB + s, rows, stride=ROW_SUB), slice(None))


def _one_row(ref, row):
    if isinstance(row, int):
        return ref.at[pl.ds(row * ROW_SUB, ROW_SUB), :]
    return ref.at[pl.ds(pl.multiple_of(row * ROW_SUB, ROW_SUB), ROW_SUB), :]


def _row_span(ref, first_row, rows):
    return ref.at[pl.ds(first_row * ROW_SUB, rows * ROW_SUB), :]


def _dot_nt(a, b):
    return lax.dot_general(a, b, (((1,), (1,)), ((), ())), preferred_element_type=F32)


def _proj_kernel(x_ref, g_ref, b_ref, cos_ref, sin_ref, w_ref, h_ref, o_ref, hb_ref):
    j = pl.program_id(1)

    @pl.when(j == 0)
    def _():
        h = _layer_norm(x_ref[...], g_ref[...], b_ref[...])
        h_ref[...] = h
        hb_ref[...] = h.astype(BF16)

    acc = _dot(hb_ref[...], w_ref[...])

    def rotate(scale):
        c = cos_ref[...]
        s = sin_ref[...]
        for hh in range(PROJ_TN // RET_QK):
            t = acc[:, hh * RET_QK:(hh + 1) * RET_QK]
            v = t * c + pltpu.roll(t, RET_QK // 2, 1) * s
            if scale != 1.0:
                v = v * scale
            o_ref[:, hh * RET_QK:(hh + 1) * RET_QK] = v.astype(BF16)

    @pl.when(j == SEG_QR // PROJ_TN)
    def _():
        rotate(1.0)

    @pl.when(j == SEG_KR // PROJ_TN)
    def _():
        rotate(RET_QK ** -0.5)

    is_gr = (j >= SEG_GR // PROJ_TN) & (j < SEG_QD // PROJ_TN)
    is_qd = j == SEG_QD // PROJ_TN
    is_gate = j >= SEG_GATE // PROJ_TN
    is_plain = ((j >= SEG_VR // PROJ_TN) & (j < SEG_GR // PROJ_TN)) | (
        (j >= SEG_KD // PROJ_TN) & (j < SEG_GATE // PROJ_TN))

    @pl.when(is_plain)
    def _():
        o_ref[...] = acc.astype(BF16)

    @pl.when(is_gr)
    def _():
        o_ref[...] = _silu(acc).astype(BF16)

    @pl.when(is_qd)
    def _():
        o_ref[...] = (acc * (DIFF_HD ** -0.5 * LOG2E)).astype(BF16)

    @pl.when(is_gate)
    def _():
        o_ref[...] = jax.nn.sigmoid(acc).astype(BF16)


def _proj_call(x2, ln_g, ln_b, cosf, sinf, w_bf, seq):
    n = x2.shape[0]
    tm, tn = PROJ_TM, PROJ_TN
    s_tiles = seq // tm
    return pl.pallas_call(
        _proj_kernel,
        name="proj",
        grid=(n // tm, IN_COLS // tn),
        in_specs=[
            pl.BlockSpec((tm, D_MODEL), lambda i, j: (i, 0)),
            pl.BlockSpec((1, D_MODEL), lambda i, j: (0, 0)),
            pl.BlockSpec((1, D_MODEL), lambda i, j: (0, 0)),
            pl.BlockSpec((tm, RET_QK), lambda i, j: (i % s_tiles, 0)),
            pl.BlockSpec((tm, RET_QK), lambda i, j: (i % s_tiles, 0)),
            pl.BlockSpec((D_MODEL, tn), lambda i, j: (0, j)),
        ],
        out_specs=[
            pl.BlockSpec((tm, D_MODEL), lambda i, j: (i, 0)),
            pl.BlockSpec((tm, tn), lambda i, j: (i, j)),
        ],
        out_shape=[
            jax.ShapeDtypeStruct((n, D_MODEL), F32),
            jax.ShapeDtypeStruct((n, IN_COLS), BF16),
        ],
        scratch_shapes=[pltpu.VMEM((tm, D_MODEL), BF16)],
        compiler_params=pltpu.CompilerParams(
            dimension_semantics=("arbitrary", "arbitrary"), vmem_limit_bytes=VMEM_LIMIT),
    )(x2, ln_g, ln_b, cosf, sinf, w_bf)


def _ret_kernel(q_ref, k_ref, v_ref, g_ref, inner_ref, qdec_ref, kdec_ref, cdec_ref, gn_ref,
                o_ref, state_ref):
    c = pl.program_id(2)

    @pl.when(c == 0)
    def _():
        state_ref[...] = jnp.zeros_like(state_ref)

    st = state_ref[...]
    for j in range(RET_SUB):
        rows = slice(j * RET_CHUNK, (j + 1) * RET_CHUNK)
        q = q_ref[rows, :]
        k = k_ref[rows, :]
        v = v_ref[rows, :]
        scores = _dot_nt(q, k) * inner_ref[0]
        o = _dot(scores.astype(BF16), v) + qdec_ref[0] * _dot(q, st.astype(BF16))
        kd = (k.astype(F32) * kdec_ref[0]).astype(BF16)
        kv = lax.dot_general(kd, v, (((0,), (0,)), ((), ())), preferred_element_type=F32)
        st = cdec_ref[0] * st + kv

        mu = jnp.mean(o, -1, keepdims=True)
        d = o - mu
        var = jnp.mean(d * d, -1, keepdims=True)
        on = d * lax.rsqrt(var + EPS) * gn_ref[0]
        o_ref[rows, :] = (g_ref[rows, :].astype(F32) * on).astype(BF16)
    state_ref[...] = st


def _ret_call(proj, inner, qdec, kdec, cdec, gn, batch, seq):
    n = proj.shape[0]
    c = RET_CHUNK
    rows = RET_SUB * c
    nc = seq // rows
    qb = SEG_QR // RET_QK
    kb = SEG_KR // RET_QK
    vb = SEG_VR // RET_V
    gb = SEG_GR // RET_V
    return pl.pallas_call(
        _ret_kernel,
        name="retention",
        grid=(batch, RET_HEADS, nc),
        in_specs=[
            pl.BlockSpec((rows, RET_QK), lambda b, h, i: (b * nc + i, qb + h)),
            pl.BlockSpec((rows, RET_QK), lambda b, h, i: (b * nc + i, kb + h)),
            pl.BlockSpec((rows, RET_V), lambda b, h, i: (b * nc + i, vb + h)),
            pl.BlockSpec((rows, RET_V), lambda b, h, i: (b * nc + i, gb + h)),
            pl.BlockSpec((1, c, c), lambda b, h, i: (h, 0, 0)),
            pl.BlockSpec((1, c, 1), lambda b, h, i: (h, 0, 0)),
            pl.BlockSpec((1, c, 1), lambda b, h, i: (h, 0, 0)),
            pl.BlockSpec((1, 1, RET_V), lambda b, h, i: (h, 0, 0)),
            pl.BlockSpec((1, 1, RET_V), lambda b, h, i: (h, 0, 0)),
        ],
        out_specs=pl.BlockSpec((rows, RET_V), lambda b, h, i: (b * nc + i, h)),
        out_shape=jax.ShapeDtypeStruct((n, RET_HEADS * RET_V), BF16),
        scratch_shapes=[pltpu.VMEM((RET_QK, RET_V), F32)],
        compiler_params=pltpu.CompilerParams(
            dimension_semantics=("arbitrary", "arbitrary", "arbitrary"),
            vmem_limit_bytes=VMEM_LIMIT),
    )(proj, proj, proj, proj, inner, qdec, kdec, cdec, gn)


def _attn_kernel(qi_ref, ki_ref, q_ref, k_ref, vt_ref, tab_ref, lq1_ref, lk1_ref, lq2_ref,
                 lk2_ref, sg_ref, o_ref, qst_ref, m_ref, acc_ref, s_ref, p_ref):
    t = ATTN_T
    hd2 = 2 * DIFF_HD
    p = pl.program_id(2)
    qi = qi_ref[p]
    ki = ki_ref[p]

    @pl.when(ki == 0)
    def _():
        qt = q_ref[...].astype(F32).T
        sub = lax.broadcasted_iota(I32, qt.shape, 0)
        qst_ref[:, 0:t] = jnp.where(sub < DIFF_HD, qt, 0.0).astype(BF16)
        qst_ref[:, t:2 * t] = jnp.where(sub >= DIFF_HD, qt, 0.0).astype(BF16)
        m_ref[...] = jnp.full_like(m_ref, MASK_VALUE)
        acc_ref[...] = jnp.zeros_like(acc_ref)

    def step(bias, sub):
        kk = k_ref[sub * t:(sub + 1) * t, :]
        vt = vt_ref[:, sub * t:(sub + 1) * t]
        for half in range(2):
            cols = slice(half * t, (half + 1) * t)
            s = _dot(kk, qst_ref[:, cols])
            if bias is not None:
                s = s + bias()
            s_ref[half] = s
        for half in range(2):
            cols = slice(half * t, (half + 1) * t)
            mx = _tree(jnp.maximum, [s_ref[half, r:r + 8, :] for r in range(0, t, 8)])
            m_prev = m_ref[:, cols]
            m_new = jnp.maximum(m_prev, jnp.max(mx, 0, keepdims=True))
            alpha = jnp.exp2(m_prev - m_new)
            for r in range(0, t, 16):
                p_ref[half, r:r + 16, :] = jnp.exp2(s_ref[half, r:r + 16, :] - m_new).astype(BF16)
            acc_ref[:, cols] = alpha * acc_ref[:, cols] + _dot(vt, p_ref[half])
            m_ref[:, cols] = m_new

    def finalize():
        lam = (jnp.exp(jnp.sum(lq1_ref[...] * lk1_ref[...], -1, keepdims=True))
               - jnp.exp(jnp.sum(lq2_ref[...] * lk2_ref[...], -1, keepdims=True)) + LAM_INIT)
        o1 = acc_ref[0:hd2, 0:t] / acc_ref[hd2:hd2 + 1, 0:t]
        o2 = acc_ref[0:hd2, t:2 * t] / acc_ref[hd2:hd2 + 1, t:2 * t]
        o = o1 - lam * o2
        o = o * lax.rsqrt(jnp.mean(o * o, 0, keepdims=True) + EPS) * (sg_ref[...] * (1.0 - LAM_INIT))
        o_ref[...] = o.T.astype(BF16)

    for sub in range(ATTN_KSUB):
        ks = ki * ATTN_KSUB + sub
        pl.when(ks < qi - 1)(functools.partial(step, None, sub))
        pl.when(ks == qi - 1)(functools.partial(step, lambda: tab_ref[0, 1], sub))

        @pl.when(ks == qi)
        def _(sub=sub):
            step(lambda: tab_ref[0, 0], sub)
            finalize()


def _attn_call(proj, tab, lq1, lk1, lq2, lk2, sg, batch, seq):
    n = proj.shape[0]
    t = ATTN_T
    nq = seq // t
    tk = ATTN_KSUB * t
    nk = seq // tk
    pairs = [(a, b) for a in range(nq) for b in range(a // ATTN_KSUB + 1)]
    qi_tab = jnp.asarray(np.array([a for a, _ in pairs], np.int32))
    ki_tab = jnp.asarray(np.array([b for _, b in pairs], np.int32))
    hd2 = 2 * DIFF_HD
    qb, kb = SEG_QD // hd2, SEG_KD // hd2
    v = proj[:, SEG_VD:SEG_VD + DIFF_HEADS * hd2].reshape(batch, seq, DIFF_HEADS, hd2)
    vt = jnp.transpose(v, (0, 2, 3, 1))
    extra = jnp.zeros((batch, DIFF_HEADS, ATTN_VT_ROWS - hd2, seq), BF16).at[:, :, 0, :].set(1.0)
    vt = jnp.concatenate([vt, extra], axis=2).reshape(batch * DIFF_HEADS * ATTN_VT_ROWS, seq)
    vec = pl.BlockSpec((1, DIFF_HD), lambda b, h, p, qi, ki: (0, 0))
    grid_spec = pltpu.PrefetchScalarGridSpec(
        num_scalar_prefetch=2,
        grid=(batch, DIFF_HEADS, len(pairs)),
        in_specs=[
            pl.BlockSpec((t, hd2), lambda b, h, p, qi, ki: (b * nq + qi[p], qb + h)),
            pl.BlockSpec((tk, hd2), lambda b, h, p, qi, ki: (b * nk + ki[p], kb + h)),
            pl.BlockSpec((ATTN_VT_ROWS, tk), lambda b, h, p, qi, ki: (b * DIFF_HEADS + h, ki[p])),
            pl.BlockSpec((1, 2, t, t), lambda b, h, p, qi, ki: (h, 0, 0, 0)),
            vec, vec, vec, vec,
            pl.BlockSpec((hd2, 1), lambda b, h, p, qi, ki: (0, 0)),
        ],
        out_specs=pl.BlockSpec((t, hd2), lambda b, h, p, qi, ki: (b * nq + qi[p], h)),
        scratch_shapes=[
            pltpu.VMEM((hd2, 2 * t), BF16),
            pltpu.VMEM((1, 2 * t), F32),
            pltpu.VMEM((ATTN_VT_ROWS, 2 * t), F32),
            pltpu.VMEM((2, t, t), F32),
            pltpu.VMEM((2, t, t), BF16),
        ],
    )
    return pl.pallas_call(
        _attn_kernel,
        name="diffattn",
        grid_spec=grid_spec,
        out_shape=jax.ShapeDtypeStruct((n, DIFF_HEADS * hd2), BF16),
        compiler_params=pltpu.CompilerParams(
            dimension_semantics=("arbitrary", "arbitrary", "arbitrary"),
            vmem_limit_bytes=VMEM_LIMIT),
    )(qi_tab, ki_tab, proj, proj, vt, tab, lq1, lk1, lq2, lk2, sg)


def _route(scores, bias, carry):
    e, tm = scores.shape
    neg = -jnp.inf
    choice = scores + bias
    row_g = lax.broadcasted_iota(I32, (GROUP_SIZE, tm), 0)
    group_scores = []
    for g in range(N_GROUPS):
        blk = choice[g * GROUP_SIZE:(g + 1) * GROUP_SIZE]
        m1 = jnp.max(blk, 0, keepdims=True)
        i1 = jnp.min(jnp.where(blk == m1, row_g, GROUP_SIZE), 0, keepdims=True)
        m2 = jnp.max(jnp.where(row_g == i1, neg, blk), 0, keepdims=True)
        group_scores.append(m1 + m2)
    gsc = jnp.concatenate(group_scores, 0)
    row8 = lax.broadcasted_iota(I32, (N_GROUPS, tm), 0)
    okf = jnp.zeros((N_GROUPS, tm), F32)
    for _ in range(TOPK_GROUPS):
        m = jnp.max(gsc, 0, keepdims=True)
        i = jnp.min(jnp.where(gsc == m, row8, N_GROUPS), 0, keepdims=True)
        hit = row8 == i
        okf = jnp.where(hit, 1.0, okf)
        gsc = jnp.where(hit, neg, gsc)
    masked = jnp.concatenate(
        [jnp.where(jnp.broadcast_to(okf[g:g + 1], (GROUP_SIZE, tm)) > 0.0,
                   choice[g * GROUP_SIZE:(g + 1) * GROUP_SIZE], neg)
         for g in range(N_GROUPS)], 0)
    row = lax.broadcasted_iota(I32, (e, tm), 0)
    idxs, ws = [], []
    multi = jnp.zeros((e, tm), F32)
    for _ in range(TOP_K):
        m = jnp.max(masked, 0, keepdims=True)
        i = jnp.min(jnp.where(masked == m, row, e), 0, keepdims=True)
        hit = row == i
        ws.append(jnp.sum(jnp.where(hit, scores, 0.0), 0, keepdims=True))
        idxs.append(i)
        multi = jnp.where(hit, 1.0, multi)
        masked = jnp.where(hit, neg, masked)
    a = lax.broadcasted_iota(I32, (tm, tm), 0)
    b = lax.broadcasted_iota(I32, (tm, tm), 1)
    upper = jnp.where(a < b, 1.0, 0.0).astype(BF16)
    before = _dot(multi.astype(BF16), upper) + carry
    ranks = [jnp.sum(jnp.where(row == i, before, 0.0), 0, keepdims=True).astype(I32) for i in idxs]
    new_carry = carry + jnp.sum(multi, 1, keepdims=True)
    wsum = ws[0]
    for w in ws[1:]:
        wsum = wsum + w
    ws = [w / wsum * ROUTED_SCALE for w in ws]
    return idxs, ws, ranks, new_carry


def _post_kernel(yr_ref, od_ref, g0_ref, g1_ref, h_ref, p_ref, wr_ref, wd_ref, wo_ref, l1g_ref,
                 l1b_ref, sg_ref, su_ref, sd_ref, pw_ref, pg_ref, rwt_ref, rb_ref,
                 h1_ref, base_ref, idx_ref, w_ref, rank_ref, cnt_ref, carry_ref):
    i = pl.program_id(0)

    @pl.when(i == 0)
    def _():
        carry_ref[...] = jnp.zeros_like(carry_ref)

    y_r = _dot(yr_ref[...], wr_ref[...])
    y_d = _dot(od_ref[...], wd_ref[...])
    merged = g0_ref[...].astype(F32) * y_r + g1_ref[...].astype(F32) * y_d
    mix = _dot(merged.astype(BF16), wo_ref[...])
    h1 = _layer_norm(DN_ALPHA * h_ref[...] + mix, l1g_ref[...], l1b_ref[...])
    for s in range(ROW_SUB):
        h1_ref[_col_block(0, h1.shape[0], s)] = h1[:, s * LANES:(s + 1) * LANES]
    hb = h1.astype(BF16)

    shared = _dot((_silu(_dot(hb, sg_ref[...])) * _dot(hb, su_ref[...])).astype(BF16), sd_ref[...])
    ple = _dot(p_ref[...].astype(BF16), pw_ref[...]) * jax.nn.sigmoid(_dot(hb, pg_ref[...]))
    base_ref[...] = DN_ALPHA * h1 + shared + ple

    h_lo = (h1 - hb.astype(F32)).astype(BF16)
    rw = rwt_ref[...]
    rw_hi = rw.astype(BF16)
    rw_lo = (rw - rw_hi.astype(F32)).astype(BF16)
    logits = _dot_nt(rw_hi, hb) + (_dot_nt(rw_hi, h_lo) + _dot_nt(rw_lo, hb))
    scores = jax.nn.sigmoid(logits)
    idxs, ws, ranks, new_carry = _route(scores, rb_ref[...], carry_ref[...])
    idx_ref[...] = jnp.concatenate(idxs, 0)
    w_ref[...] = jnp.concatenate(ws, 0)
    rank_ref[...] = jnp.concatenate(ranks, 0)
    carry_ref[...] = new_carry
    cnt_ref[...] = new_carry.astype(I32)


def _post_call(yr, od, proj, h, p2, wr, wd, wo, l1g, l1b, sg, su, sd, pw, pg, rwt, rb):
    n = h.shape[0]
    tm = POST_TM
    g0b = SEG_GATE // D_MODEL

    def full(a):
        return pl.BlockSpec(a.shape, lambda i: (0,) * a.ndim)

    row = lambda w: pl.BlockSpec((tm, w), lambda i: (i, 0))
    tok = lambda: pl.BlockSpec((TOP_K, tm), lambda i: (0, i))
    return pl.pallas_call(
        _post_kernel,
        name="post",
        grid=(n // tm,),
        in_specs=[
            row(RET_HEADS * RET_V), row(D_MODEL),
            pl.BlockSpec((tm, D_MODEL), lambda i: (i, g0b)),
            pl.BlockSpec((tm, D_MODEL), lambda i: (i, g0b + 1)),
            row(D_MODEL), row(PLE_DIM),
            full(wr), full(wd), full(wo), full(l1g), full(l1b), full(sg), full(su), full(sd),
            full(pw), full(pg), full(rwt), full(rb),
        ],
        out_specs=[pl.BlockSpec((tm * ROW_SUB, LANES), lambda i: (i, 0)), row(D_MODEL),
                   tok(), tok(), tok(), pl.BlockSpec((N_EXPERTS, 1), lambda i: (0, 0))],
        out_shape=[
            jax.ShapeDtypeStruct((n * ROW_SUB, LANES), F32),
            jax.ShapeDtypeStruct((n, D_MODEL), F32),
            jax.ShapeDtypeStruct((TOP_K, n), I32),
            jax.ShapeDtypeStruct((TOP_K, n), F32),
            jax.ShapeDtypeStruct((TOP_K, n), I32),
            jax.ShapeDtypeStruct((N_EXPERTS, 1), I32),
        ],
        scratch_shapes=[pltpu.VMEM((N_EXPERTS, 1), F32)],
        compiler_params=pltpu.CompilerParams(
            dimension_semantics=("arbitrary",), vmem_limit_bytes=VMEM_LIMIT),
    )(yr, od, proj, proj, h, p2, wr, wd, wo, l1g, l1b, sg, su, sd, pw, pg, rwt, rb)


def _dest_kernel(idx_ref, rank_ref, ps_ref, o_ref):
    idx = idx_ref[...]
    ps = ps_ref[...]
    td = idx.shape[1]
    row = lax.broadcasted_iota(I32, (N_EXPERTS, td), 0)
    outs = []
    for k in range(TOP_K):
        sel = jnp.sum(jnp.where(row == idx[k:k + 1], ps, 0.0), 0, keepdims=True)
        outs.append(sel.astype(I32))
    o_ref[...] = jnp.concatenate(outs, 0) + rank_ref[...]


def _dest_call(idx, rank, pstart_f):
    n = idx.shape[1]
    td = min(DEST_T, n)
    spec = pl.BlockSpec((TOP_K, td), lambda i: (0, i))
    return pl.pallas_call(
        _dest_kernel,
        name="dest",
        grid=(n // td,),
        in_specs=[spec, spec, pl.BlockSpec((N_EXPERTS, 1), lambda i: (0, 0))],
        out_specs=spec,
        out_shape=jax.ShapeDtypeStruct((TOP_K, n), I32),
        compiler_params=pltpu.CompilerParams(dimension_semantics=("arbitrary",)),
    )(idx, rank, pstart_f)


def _expert_kernel(be_ref, rt_ref, rtn_ref, h_hbm, wg_ref, wu_ref, wd_ref, y_ref,
                   buf_ref, wgb_ref, wub_ref, wdb_ref, sem):
    i = pl.program_id(0)
    nb = pl.num_programs(0)
    g = EXPERT_BLOCK

    def wait(s):
        pltpu.make_async_copy(_row_span(h_hbm, 0, g), _row_span(buf_ref, s * g, g), sem.at[s]).wait()

    @pl.when(i == 0)
    def _():
        for j in range(GATHER_AHEAD):
            def body(r, c, j=j):
                pltpu.make_async_copy(_one_row(h_hbm, rt_ref[j, 0, r]),
                                      _one_row(buf_ref, j * g + r), sem.at[j]).start()
                return c
            lax.fori_loop(0, g, body, 0, unroll=8)

    changed = jnp.logical_or(i == 0, be_ref[i] != be_ref[jnp.maximum(i - 1, 0)])

    @pl.when(changed)
    def _():
        wgb_ref[...] = wg_ref[0].astype(BF16)
        wub_ref[...] = wu_ref[0].astype(BF16)
        wdb_ref[...] = wd_ref[0].astype(BF16)

    def block(slot):
        wait(slot)
        x = jnp.concatenate([buf_ref[_col_block(slot * g, g, s)] for s in range(ROW_SUB)],
                            axis=1).astype(BF16)
        ahead = (slot + GATHER_AHEAD) % GATHER_SLOTS
        for r in range(g):
            pltpu.make_async_copy(_one_row(h_hbm, rtn_ref[0, 0, r]),
                                  _one_row(buf_ref, ahead * g + r), sem.at[ahead]).start()
        act = _silu(_dot(x, wgb_ref[...])) * _dot(x, wub_ref[...])
        y = _dot(act.astype(BF16), wdb_ref[...])
        for s in range(ROW_SUB):
            y_ref[_col_block(0, g, s)] = y[:, s * LANES:(s + 1) * LANES]

        @pl.when(i == nb - 1)
        def _():
            for j in range(1, GATHER_AHEAD + 1):
                wait((slot + j) % GATHER_SLOTS)

    for slot in range(GATHER_SLOTS):
        pl.when(i % GATHER_SLOTS == slot)(functools.partial(block, slot))


def _expert_call(block_e, row_tok3, h1, wg, wu, wd):
    nb = block_e.shape[0]
    g = EXPERT_BLOCK
    grid_spec = pltpu.PrefetchScalarGridSpec(
        num_scalar_prefetch=1,
        grid=(nb,),
        in_specs=[
            pl.BlockSpec((GATHER_AHEAD, 1, g), lambda i, be: (0, 0, 0), memory_space=pltpu.SMEM),
            pl.BlockSpec((1, 1, g), lambda i, be: (jnp.minimum(i + GATHER_AHEAD, nb - 1), 0, 0),
                         memory_space=pltpu.SMEM),
            pl.BlockSpec(memory_space=pl.ANY),
            pl.BlockSpec((1, D_MODEL, D_EXPERT), lambda i, be: (be[i], 0, 0)),
            pl.BlockSpec((1, D_MODEL, D_EXPERT), lambda i, be: (be[i], 0, 0)),
            pl.BlockSpec((1, D_EXPERT, D_MODEL), lambda i, be: (be[i], 0, 0)),
        ],
        out_specs=pl.BlockSpec((g * ROW_SUB, LANES), lambda i, be: (i, 0)),
        scratch_shapes=[
            pltpu.VMEM((GATHER_SLOTS * g * ROW_SUB, LANES), F32),
            pltpu.VMEM((D_MODEL, D_EXPERT), BF16),
            pltpu.VMEM((D_MODEL, D_EXPERT), BF16),
            pltpu.VMEM((D_EXPERT, D_MODEL), BF16),
            pltpu.SemaphoreType.DMA((GATHER_SLOTS,)),
        ],
    )
    assert nb > GATHER_AHEAD
    return pl.pallas_call(
        _expert_kernel,
        name="experts",
        grid_spec=grid_spec,
        out_shape=jax.ShapeDtypeStruct((nb * g * ROW_SUB, LANES), F32),
        compiler_params=pltpu.CompilerParams(
            dimension_semantics=("arbitrary",), vmem_limit_bytes=VMEM_LIMIT),
    )(block_e, row_tok3, row_tok3, h1, wg, wu, wd)


def _combine_kernel(d_ref, dn_ref, y_hbm, w_ref, base_ref, g_ref, b_ref, o_ref, buf_ref, sem):
    i = pl.program_id(0)
    nt = pl.num_programs(0)
    tt = COMB_T

    def plane(s, k):
        return (s * TOP_K + k) * tt

    def wait(s):
        for k in range(TOP_K):
            pltpu.make_async_copy(_row_span(y_hbm, 0, tt), _row_span(buf_ref, plane(s, k), tt),
                                  sem.at[s]).wait()

    @pl.when(i == 0)
    def _():
        for j in range(GATHER_AHEAD):
            def body(t, c, j=j):
                for k in range(TOP_K):
                    pltpu.make_async_copy(_one_row(y_hbm, d_ref[j, k, t]),
                                          _one_row(buf_ref, plane(j, k) + t), sem.at[j]).start()
                return c
            lax.fori_loop(0, tt, body, 0)

    def tile(slot):
        wait(slot)
        ahead = (slot + GATHER_AHEAD) % GATHER_SLOTS
        for t in range(tt):
            for k in range(TOP_K):
                pltpu.make_async_copy(_one_row(y_hbm, dn_ref[0, k, t]),
                                      _one_row(buf_ref, plane(ahead, k) + t),
                                      sem.at[ahead]).start()
        w = w_ref[...]
        parts = []
        for s in range(ROW_SUB):
            part = w[:, 0:1] * buf_ref[_col_block(plane(slot, 0), tt, s)]
            for k in range(1, TOP_K):
                part = part + w[:, k:k + 1] * buf_ref[_col_block(plane(slot, k), tt, s)]
            parts.append(part)
        routed = jnp.concatenate(parts, axis=1)
        o_ref[...] = _layer_norm(base_ref[...] + routed, g_ref[...], b_ref[...])

        @pl.when(i == nt - 1)
        def _():
            for j in range(1, GATHER_AHEAD + 1):
                wait((slot + j) % GATHER_SLOTS)

    for slot in range(GATHER_SLOTS):
        pl.when(i % GATHER_SLOTS == slot)(functools.partial(tile, slot))


def _combine_call(dest3, y, w_t, base, ln_g, ln_b):
    n = base.shape[0]
    tt = COMB_T
    nt = n // tt
    return pl.pallas_call(
        _combine_kernel,
        name="combine",
        grid=(nt,),
        in_specs=[
            pl.BlockSpec((GATHER_AHEAD, TOP_K, tt), lambda i: (0, 0, 0), memory_space=pltpu.SMEM),
            pl.BlockSpec((1, TOP_K, tt), lambda i: (jnp.minimum(i + GATHER_AHEAD, nt - 1), 0, 0),
                         memory_space=pltpu.SMEM),
            pl.BlockSpec(memory_space=pl.ANY),
            pl.BlockSpec((tt, TOP_K), lambda i: (i, 0)),
            pl.BlockSpec((tt, D_MODEL), lambda i: (i, 0)),
            pl.BlockSpec((1, D_MODEL), lambda i: (0, 0)),
            pl.BlockSpec((1, D_MODEL), lambda i: (0, 0)),
        ],
        out_specs=pl.BlockSpec((tt, D_MODEL), lambda i: (i, 0)),
        out_shape=jax.ShapeDtypeStruct((n, D_MODEL), F32),
        scratch_shapes=[
            pltpu.VMEM((GATHER_SLOTS * TOP_K * tt * ROW_SUB, LANES), F32),
            pltpu.SemaphoreType.DMA((GATHER_SLOTS,)),
        ],
        compiler_params=pltpu.CompilerParams(
            dimension_semantics=("arbitrary",), vmem_limit_bytes=VMEM_LIMIT),
    )(dest3, dest3, y, w_t, base, ln_g, ln_b)


def _xpos_tables(seq):
    half = RET_QK // 2
    pos = jnp.arange(seq, dtype=F32)
    theta = 1.0 / (XPOS_BASE ** jnp.linspace(0.0, 1.0, half, dtype=F32))
    ang = pos[:, None] * theta[None, :]
    cos = jnp.cos(ang)
    sin = jnp.sin(ang)
    return jnp.concatenate([cos, cos], -1), jnp.concatenate([-sin, sin], -1)


def _decay_tables():
    c = RET_CHUNK
    log_gamma = jnp.log(1.0 - 2.0 ** (-5.0 - jnp.arange(RET_HEADS, dtype=F32)))
    idx = jnp.arange(c, dtype=F32)
    rel = idx[:, None] - idx[None, :]
    inner = jnp.where(rel >= 0, jnp.exp(jnp.maximum(rel, 0.0)[None] * log_gamma[:, None, None]), 0.0)
    q_dec = jnp.exp((idx + 1.0)[None, :] * log_gamma[:, None])
    k_dec = jnp.exp((c - 1.0 - idx)[None, :] * log_gamma[:, None])
    c_dec = jnp.exp(c * log_gamma)
    cdec = jnp.broadcast_to(c_dec[:, None, None], (RET_HEADS, 1, RET_V))
    return inner, q_dec[:, :, None], k_dec[:, :, None], cdec


def _t5_bucket(n):
    max_exact = REL_BUCKETS // 2
    nf = jnp.maximum(n, 1).astype(F32)
    large = max_exact + (jnp.log(nf / max_exact) / math.log(REL_MAX_DIST / max_exact)
                         * (REL_BUCKETS - max_exact)).astype(I32)
    large = jnp.minimum(large, REL_BUCKETS - 1)
    return jnp.where(n < max_exact, n, large)


def _bias_tables(rel_bias, t):
    c = REL_MAX_DIST
    nb = t // c
    far = rel_bias[REL_BUCKETS - 1].astype(F32)
    d = jnp.arange(c)[None, :] - jnp.arange(c)[:, None]
    near0 = (jnp.take(rel_bias, _t5_bucket(jnp.maximum(d, 0)), axis=0).astype(F32) - far) * LOG2E
    near0 = jnp.where((d >= 0)[:, :, None], near0, MASK_VALUE)
    near1 = (jnp.take(rel_bias, _t5_bucket(d + c), axis=0).astype(F32) - far) * LOG2E
    tile0 = jnp.tile(near0, (nb, nb, 1))
    tile1 = jnp.tile(near1, (nb, nb, 1))
    kb = (jnp.arange(t) // c)[:, None, None]
    qb = (jnp.arange(t) // c)[None, :, None]
    diag = jnp.where(qb == kb, tile0,
                     jnp.where(qb == kb + 1, tile1, jnp.where(qb > kb, 0.0, MASK_VALUE)))
    left = jnp.where((qb == 0) & (kb == nb - 1), tile1, 0.0)
    return jnp.transpose(jnp.stack([diag, left], 0), (3, 0, 1, 2))


def kernel(x, p, ln_in_g, ln_in_b, rel_bias, w_in, ret_gn_g, lambda_q1, lambda_k1, lambda_q2,
           lambda_k2, diff_subln_g, w_ret_o, w_diff_o, w_o, ln1_g, ln1_b, router_w, router_bias,
           exp_w_gate, exp_w_up, exp_w_down, sh_w_gate, sh_w_up, sh_w_down, ple_w, ple_gate_w,
           ln2_g, ln2_b):
    batch, seq, d = x.shape
    n = batch * seq
    assert d == D_MODEL and seq % (ATTN_T * ATTN_KSUB) == 0 and seq % PROJ_TM == 0
    assert n % min(DEST_T, n) == 0 and n % COMB_T == 0 and n % POST_TM == 0
    x2 = x.reshape(n, d)
    row = lambda a: a.reshape(1, -1).astype(F32)

    cosf, sinf = _xpos_tables(seq)
    inner, qdec, kdec, cdec = _decay_tables()
    h, proj = _proj_call(x2, row(ln_in_g), row(ln_in_b), cosf, sinf, w_in[0].astype(BF16), seq)

    yr = _ret_call(proj, inner, qdec, kdec, cdec,
                   ret_gn_g[0].reshape(RET_HEADS, 1, RET_V).astype(F32), batch, seq)

    od = _attn_call(proj, _bias_tables(rel_bias, ATTN_T), row(lambda_q1[0]), row(lambda_k1[0]),
                    row(lambda_q2[0]), row(lambda_k2[0]),
                    diff_subln_g[0].reshape(-1, 1).astype(F32), batch, seq)

    h1, base, idx, w, rank, counts = _post_call(
        yr, od, proj, h, p[0].reshape(n, PLE_DIM),
        w_ret_o[0].astype(BF16), w_diff_o[0].astype(BF16), w_o[0].astype(BF16),
        row(ln1_g[0]), row(ln1_b[0]),
        sh_w_gate[0].astype(BF16), sh_w_up[0].astype(BF16), sh_w_down[0].astype(BF16),
        ple_w[0].astype(BF16), ple_gate_w[0].astype(BF16),
        router_w[0].T.astype(F32), router_bias[0].reshape(N_EXPERTS, 1).astype(F32))

    g = EXPERT_BLOCK
    counts = counts.reshape(N_EXPERTS)
    padded = (counts + g - 1) // g * g
    pend = jnp.cumsum(padded)
    pstart = pend - padded
    n_rows = -(-(n * TOP_K + N_EXPERTS * (g - 1)) // g) * g
    n_blocks = n_rows // g
    block_start = jnp.arange(n_blocks, dtype=I32) * g
    block_e = jnp.minimum(
        jnp.sum((pend[None, :] <= block_start[:, None]).astype(I32), axis=1),
        N_EXPERTS - 1).astype(I32)

    dest = _dest_call(idx, rank, pstart.astype(F32).reshape(N_EXPERTS, 1))
    tok = jnp.broadcast_to(jnp.arange(n, dtype=I32)[None, :], (TOP_K, n))
    row_tok = jnp.zeros((n_rows,), I32).at[dest.reshape(-1)].set(
        tok.reshape(-1), unique_indices=True, mode='drop')

    y = _expert_call(block_e, row_tok.reshape(n_blocks, 1, g), h1,
                     exp_w_gate[0], exp_w_up[0], exp_w_down[0])

    dest3 = dest.reshape(TOP_K, n // COMB_T, COMB_T).transpose(1, 0, 2)
    out = _combine_call(dest3, y, w.T, base, row(ln2_g[0]), row(ln2_b[0]))
    return out.reshape(batch, seq, d)
```

```python
import functools
import math

import numpy as np
import jax
import jax.numpy as jnp
from jax import lax
from jax.experimental import pallas as pl
from jax.experimental.pallas import tpu as pltpu

F32 = jnp.float32
BF16 = jnp.bfloat16
I32 = jnp.int32

D_MODEL = 1024
PLE_DIM = 256
RET_HEADS = 8
RET_QK = 128
RET_V = 256
RET_CHUNK = 128
XPOS_BASE = 10000.0
DIFF_HEADS = 8
DIFF_HD = 64
REL_BUCKETS = 32
REL_MAX_DIST = 128
N_EXPERTS = 256
TOP_K = 8
N_GROUPS = 8
GROUP_SIZE = N_EXPERTS // N_GROUPS
TOPK_GROUPS = 4
D_EXPERT = 256
ROUTED_SCALE = 2.5
EXPERT_BLOCK = 256
DEPTH = 1
DN_ALPHA = (2.0 * DEPTH) ** 0.25
EPS = 1e-5
LAM_INIT = 0.8 - 0.6 * math.exp(-0.3 * 0)

SEG_QR, SEG_KR, SEG_VR, SEG_GR, SEG_QD, SEG_KD, SEG_VD, SEG_GATE = (
    0, 1024, 2048, 4096, 6144, 7168, 8192, 9216)
IN_COLS = 11264

LANES = 128
ROW_SUB = D_MODEL // LANES
MASK_VALUE = -1e30
LOG2E = math.log2(math.e)
VMEM_LIMIT = 56 * 1024 * 1024

RET_SUB = 4
PROJ_TM = 1024
PROJ_TN = 1024
ATTN_T = 512
ATTN_KSUB = 2
ATTN_VT_ROWS = 2 * DIFF_HD + 16
POST_TM = 256
DEST_T = 2048
COMB_T = 32
GATHER_AHEAD = 3
GATHER_SLOTS = GATHER_AHEAD + 1


def _layer_norm(x, g, b):
    mu = jnp.mean(x, -1, keepdims=True)
    d = x - mu
    var = jnp.mean(d * d, -1, keepdims=True)
    return d * lax.rsqrt(var + EPS) * g + b


def _silu(x):
    return x * jax.nn.sigmoid(x)


def _dot(a, b):
    return jnp.dot(a, b, preferred_element_type=F32)


def _tree(op, items):
    items = list(items)
    while len(items) > 1:
        items = [op(items[i], items[i + 1]) if i + 1 < len(items) else items[i]
                 for i in range(0, len(items), 2)]
    return items[0]


def _col_block(first_row, rows, s):
    return (pl.ds(first_row * ROW_SUB + s, rows, stride=ROW_SUB), slice(None))


def _one_row(ref, row):
    if isinstance(row, int):
        return ref.at[pl.ds(row * ROW_SUB, ROW_SUB), :]
    return ref.at[pl.ds(pl.multiple_of(row * ROW_SUB, ROW_SUB), ROW_SUB), :]


def _row_span(ref, first_row, rows):
    return ref.at[pl.ds(first_row * ROW_SUB, rows * ROW_SUB), :]


def _dot_nt(a, b):
    return lax.dot_general(a, b, (((1,), (1,)), ((), ())), preferred_element_type=F32)


def _proj_kernel(x_ref, g_ref, b_ref, cos_ref, sin_ref, w_ref, h_ref, o_ref, hb_ref):
    j = pl.program_id(1)

    @pl.when(j == 0)
    def _():
        h = _layer_norm(x_ref[...], g_ref[...], b_ref[...])
        h_ref[...] = h
        hb_ref[...] = h.astype(BF16)

    acc = _dot(hb_ref[...], w_ref[...])

    def rotate(scale):
        c = cos_ref[...]
        s = sin_ref[...]
        for hh in range(PROJ_TN // RET_QK):
            t = acc[:, hh * RET_QK:(hh + 1) * RET_QK]
            v = t * c + pltpu.roll(t, RET_QK // 2, 1) * s
            if scale != 1.0:
                v = v * scale
            o_ref[:, hh * RET_QK:(hh + 1) * RET_QK] = v.astype(BF16)

    @pl.when(j == SEG_QR // PROJ_TN)
    def _():
        rotate(1.0)

    @pl.when(j == SEG_KR // PROJ_TN)
    def _():
        rotate(RET_QK ** -0.5)

    is_gr = (j >= SEG_GR // PROJ_TN) & (j < SEG_QD // PROJ_TN)
    is_qd = j == SEG_QD // PROJ_TN
    is_gate = j >= SEG_GATE // PROJ_TN
    is_plain = ((j >= SEG_VR // PROJ_TN) & (j < SEG_GR // PROJ_TN)) | (
        (j >= SEG_KD // PROJ_TN) & (j < SEG_GATE // PROJ_TN))

    @pl.when(is_plain)
    def _():
        o_ref[...] = acc.astype(BF16)

    @pl.when(is_gr)
    def _():
        o_ref[...] = _silu(acc).astype(BF16)

    @pl.when(is_qd)
    def _():
        o_ref[...] = (acc * (DIFF_HD ** -0.5 * LOG2E)).astype(BF16)

    @pl.when(is_gate)
    def _():
        o_ref[...] = jax.nn.sigmoid(acc).astype(BF16)


def _proj_call(x2, ln_g, ln_b, cosf, sinf, w_bf, seq):
    n = x2.shape[0]
    tm, tn = PROJ_TM, PROJ_TN
    s_tiles = seq // tm
    return pl.pallas_call(
        _proj_kernel,
        name="proj",
        grid=(n // tm, IN_COLS // tn),
        in_specs=[
            pl.BlockSpec((tm, D_MODEL), lambda i, j: (i, 0)),
            pl.BlockSpec((1, D_MODEL), lambda i, j: (0, 0)),
            pl.BlockSpec((1, D_MODEL), lambda i, j: (0, 0)),
            pl.BlockSpec((tm, RET_QK), lambda i, j: (i % s_tiles, 0)),
            pl.BlockSpec((tm, RET_QK), lambda i, j: (i % s_tiles, 0)),
            pl.BlockSpec((D_MODEL, tn), lambda i, j: (0, j)),
        ],
        out_specs=[
            pl.BlockSpec((tm, D_MODEL), lambda i, j: (i, 0)),
            pl.BlockSpec((tm, tn), lambda i, j: (i, j)),
        ],
        out_shape=[
            jax.ShapeDtypeStruct((n, D_MODEL), F32),
            jax.ShapeDtypeStruct((n, IN_COLS), BF16),
        ],
        scratch_shapes=[pltpu.VMEM((tm, D_MODEL), BF16)],
        compiler_params=pltpu.CompilerParams(
            dimension_semantics=("arbitrary", "arbitrary"), vmem_limit_bytes=VMEM_LIMIT),
    )(x2, ln_g, ln_b, cosf, sinf, w_bf)


def _ret_kernel(q_ref, k_ref, v_ref, g_ref, inner_ref, qdec_ref, kdec_ref, cdec_ref, gn_ref,
                o_ref, state_ref):
    c = pl.program_id(2)

    @pl.when(c == 0)
    def _():
        state_ref[...] = jnp.zeros_like(state_ref)

    st = state_ref[...]
    for j in range(RET_SUB):
        rows = slice(j * RET_CHUNK, (j + 1) * RET_CHUNK)
        q = q_ref[rows, :]
        k = k_ref[rows, :]
        v = v_ref[rows, :]
        scores = _dot_nt(q, k) * inner_ref[0]
        o = _dot(scores.astype(BF16), v) + qdec_ref[0] * _dot(q, st.astype(BF16))
        kd = (k.astype(F32) * kdec_ref[0]).astype(BF16)
        kv = lax.dot_general(kd, v, (((0,), (0,)), ((), ())), preferred_element_type=F32)
        st = cdec_ref[0] * st + kv

        mu = jnp.mean(o, -1, keepdims=True)
        d = o - mu
        var = jnp.mean(d * d, -1, keepdims=True)
        on = d * lax.rsqrt(var + EPS) * gn_ref[0]
        o_ref[rows, :] = (g_ref[rows, :].astype(F32) * on).astype(BF16)
    state_ref[...] = st


def _ret_call(proj, inner, qdec, kdec, cdec, gn, batch, seq):
    n = proj.shape[0]
    c = RET_CHUNK
    rows = RET_SUB * c
    nc = seq // rows
    qb = SEG_QR // RET_QK
    kb = SEG_KR // RET_QK
    vb = SEG_VR // RET_V
    gb = SEG_GR // RET_V
    return pl.pallas_call(
        _ret_kernel,
        name="retention",
        grid=(batch, RET_HEADS, nc),
        in_specs=[
            pl.BlockSpec((rows, RET_QK), lambda b, h, i: (b * nc + i, qb + h)),
            pl.BlockSpec((rows, RET_QK), lambda b, h, i: (b * nc + i, kb + h)),
            pl.BlockSpec((rows, RET_V), lambda b, h, i: (b * nc + i, vb + h)),
            pl.BlockSpec((rows, RET_V), lambda b, h, i: (b * nc + i, gb + h)),
            pl.BlockSpec((1, c, c), lambda b, h, i: (h, 0, 0)),
            pl.BlockSpec((1, c, 1), lambda b, h, i: (h, 0, 0)),
            pl.BlockSpec((1, c, 1), lambda b, h, i: (h, 0, 0)),
            pl.BlockSpec((1, 1, RET_V), lambda b, h, i: (h, 0, 0)),
            pl.BlockSpec((1, 1, RET_V), lambda b, h, i: (h, 0, 0)),
        ],
        out_specs=pl.BlockSpec((rows, RET_V), lambda b, h, i: (b * nc + i, h)),
        out_shape=jax.ShapeDtypeStruct((n, RET_HEADS * RET_V), BF16),
        scratch_shapes=[pltpu.VMEM((RET_QK, RET_V), F32)],
        compiler_params=pltpu.CompilerParams(
            dimension_semantics=("arbitrary", "arbitrary", "arbitrary"),
            vmem_limit_bytes=VMEM_LIMIT),
    )(proj, proj, proj, proj, inner, qdec, kdec, cdec, gn)


def _attn_kernel(qi_ref, ki_ref, q_ref, k_ref, vt_ref, tab_ref, lq1_ref, lk1_ref, lq2_ref,
                 lk2_ref, sg_ref, o_ref, qst_ref, m_ref, acc_ref, s_ref, p_ref, mx_ref):
    t = ATTN_T
    hd2 = 2 * DIFF_HD
    p = pl.program_id(2)
    qi = qi_ref[p]
    ki = ki_ref[p]

    @pl.when(ki == 0)
    def _():
        qt = q_ref[...].astype(F32).T
        sub = lax.broadcasted_iota(I32, qt.shape, 0)
        qst_ref[:, 0:t] = jnp.where(sub < DIFF_HD, qt, 0.0).astype(BF16)
        qst_ref[:, t:2 * t] = jnp.where(sub >= DIFF_HD, qt, 0.0).astype(BF16)
        m_ref[...] = jnp.full_like(m_ref, MASK_VALUE)
        acc_ref[...] = jnp.zeros_like(acc_ref)

    def steps(*biases):
        for sub, bias in enumerate(biases):
            kk = k_ref[sub * t:(sub + 1) * t, :]
            for half in range(2):
                cols = slice(half * t, (half + 1) * t)
                s = _dot(kk, qst_ref[:, cols])
                if bias is not None:
                    s = s + tab_ref[0, bias]
                s_ref[sub, half] = s
                mx_ref[sub, half] = _tree(jnp.maximum, [s[r:r + 8, :] for r in range(0, t, 8)])
        for sub in range(len(biases)):
            vt = vt_ref[:, sub * t:(sub + 1) * t]
            for half in range(2):
                cols = slice(half * t, (half + 1) * t)
                m_prev = m_ref[:, cols]
                m_new = jnp.maximum(m_prev, jnp.max(mx_ref[sub, half], 0, keepdims=True))
                alpha = jnp.exp2(m_prev - m_new)
                for r in range(0, t, 64):
                    p_ref[sub, half, r:r + 64, :] = jnp.exp2(
                        s_ref[sub, half, r:r + 64, :] - m_new).astype(BF16)
                acc_ref[:, cols] = alpha * acc_ref[:, cols] + _dot(vt, p_ref[sub, half])
                m_ref[:, cols] = m_new

    def finalize():
        lam = (jnp.exp(jnp.sum(lq1_ref[...] * lk1_ref[...], -1, keepdims=True))
               - jnp.exp(jnp.sum(lq2_ref[...] * lk2_ref[...], -1, keepdims=True)) + LAM_INIT)
        o1 = acc_ref[0:hd2, 0:t] / acc_ref[hd2:hd2 + 1, 0:t]
        o2 = acc_ref[0:hd2, t:2 * t] / acc_ref[hd2:hd2 + 1, t:2 * t]
        o = o1 - lam * o2
        o = o * lax.rsqrt(jnp.mean(o * o, 0, keepdims=True) + EPS) * (sg_ref[...] * (1.0 - LAM_INIT))
        o_ref[...] = o.T.astype(BF16)

    assert ATTN_KSUB == 2
    last = ki * ATTN_KSUB + 1
    pl.when(last < qi - 1)(functools.partial(steps, None, None))
    pl.when(last == qi - 1)(functools.partial(steps, None, 1))

    @pl.when(last == qi)
    def _():
        steps(1, 0)
        finalize()

    @pl.when(last == qi + 1)
    def _():
        steps(0)
        finalize()


def _attn_call(proj, tab, lq1, lk1, lq2, lk2, sg, batch, seq):
    n = proj.shape[0]
    t = ATTN_T
    nq = seq // t
    tk = ATTN_KSUB * t
    nk = seq // tk
    pairs = [(a, b) for a in range(nq) for b in range(a // ATTN_KSUB + 1)]
    qi_tab = jnp.asarray(np.array([a for a, _ in pairs], np.int32))
    ki_tab = jnp.asarray(np.array([b for _, b in pairs], np.int32))
    hd2 = 2 * DIFF_HD
    qb, kb = SEG_QD // hd2, SEG_KD // hd2
    v = proj[:, SEG_VD:SEG_VD + DIFF_HEADS * hd2].reshape(batch, seq, DIFF_HEADS, hd2)
    vt = jnp.transpose(v, (0, 2, 3, 1))
    extra = jnp.zeros((batch, DIFF_HEADS, ATTN_VT_ROWS - hd2, seq), BF16).at[:, :, 0, :].set(1.0)
    vt = jnp.concatenate([vt, extra], axis=2).reshape(batch * DIFF_HEADS * ATTN_VT_ROWS, seq)
    vec = pl.BlockSpec((1, DIFF_HD), lambda b, h, p, qi, ki: (0, 0))
    grid_spec = pltpu.PrefetchScalarGridSpec(
        num_scalar_prefetch=2,
        grid=(batch, DIFF_HEADS, len(pairs)),
        in_specs=[
            pl.BlockSpec((t, hd2), lambda b, h, p, qi, ki: (b * nq + qi[p], qb + h)),
            pl.BlockSpec((tk, hd2), lambda b, h, p, qi, ki: (b * nk + ki[p], kb + h)),
            pl.BlockSpec((ATTN_VT_ROWS, tk), lambda b, h, p, qi, ki: (b * DIFF_HEADS + h, ki[p])),
            pl.BlockSpec((1, 2, t, t), lambda b, h, p, qi, ki: (h, 0, 0, 0)),
            vec, vec, vec, vec,
            pl.BlockSpec((hd2, 1), lambda b, h, p, qi, ki: (0, 0)),
        ],
        out_specs=pl.BlockSpec((t, hd2), lambda b, h, p, qi, ki: (b * nq + qi[p], h)),
        scratch_shapes=[
            pltpu.VMEM((hd2, 2 * t), BF16),
            pltpu.VMEM((1, 2 * t), F32),
            pltpu.VMEM((ATTN_VT_ROWS, 2 * t), F32),
            pltpu.VMEM((ATTN_KSUB, 2, t, t), F32),
            pltpu.VMEM((ATTN_KSUB, 2, t, t), BF16),
            pltpu.VMEM((ATTN_KSUB, 2, 8, t), F32),
        ],
    )
    return pl.pallas_call(
        _attn_kernel,
        name="diffattn",
        grid_spec=grid_spec,
        out_shape=jax.ShapeDtypeStruct((n, DIFF_HEADS * hd2), BF16),
        compiler_params=pltpu.CompilerParams(
            dimension_semantics=("arbitrary", "arbitrary", "arbitrary"),
            vmem_limit_bytes=VMEM_LIMIT),
    )(qi_tab, ki_tab, proj, proj, vt, tab, lq1, lk1, lq2, lk2, sg)


def _route(scores, bias, carry):
    e, tm = scores.shape
    neg = -jnp.inf
    choice = scores + bias
    row_g = lax.broadcasted_iota(I32, (GROUP_SIZE, tm), 0)
    group_scores = []
    for g in range(N_GROUPS):
        blk = choice[g * GROUP_SIZE:(g + 1) * GROUP_SIZE]
        m1 = jnp.max(blk, 0, keepdims=True)
        i1 = jnp.min(jnp.where(blk == m1, row_g, GROUP_SIZE), 0, keepdims=True)
        m2 = jnp.max(jnp.where(row_g == i1, neg, blk), 0, keepdims=True)
        group_scores.append(m1 + m2)
    gsc = jnp.concatenate(group_scores, 0)
    row8 = lax.broadcasted_iota(I32, (N_GROUPS, tm), 0)
    okf = jnp.zeros((N_GROUPS, tm), F32)
    for _ in range(TOPK_GROUPS):
        m = jnp.max(gsc, 0, keepdims=True)
        i = jnp.min(jnp.where(gsc == m, row8, N_GROUPS), 0, keepdims=True)
        hit = row8 == i
        okf = jnp.where(hit, 1.0, okf)
        gsc = jnp.where(hit, neg, gsc)
    masked = jnp.concatenate(
        [jnp.where(jnp.broadcast_to(okf[g:g + 1], (GROUP_SIZE, tm)) > 0.0,
                   choice[g * GROUP_SIZE:(g + 1) * GROUP_SIZE], neg)
         for g in range(N_GROUPS)], 0)
    row = lax.broadcasted_iota(I32, (e, tm), 0)
    idxs, ws = [], []
    multi = jnp.zeros((e, tm), F32)
    for _ in range(TOP_K):
        m = jnp.max(masked, 0, keepdims=True)
        i = jnp.min(jnp.where(masked == m, row, e), 0, keepdims=True)
        hit = row == i
        ws.append(jnp.sum(jnp.where(hit, scores, 0.0), 0, keepdims=True))
        idxs.append(i)
        multi = jnp.where(hit, 1.0, multi)
        masked = jnp.where(hit, neg, masked)
    a = lax.broadcasted_iota(I32, (tm, tm), 0)
    b = lax.broadcasted_iota(I32, (tm, tm), 1)
    upper = jnp.where(a < b, 1.0, 0.0).astype(BF16)
    before = _dot(multi.astype(BF16), upper) + carry
    ranks = [jnp.sum(jnp.where(row == i, before, 0.0), 0, keepdims=True).astype(I32) for i in idxs]
    new_carry = carry + jnp.sum(multi, 1, keepdims=True)
    wsum = ws[0]
    for w in ws[1:]:
        wsum = wsum + w
    ws = [w / wsum * ROUTED_SCALE for w in ws]
    return idxs, ws, ranks, new_carry


def _post_kernel(yr_ref, od_ref, g0_ref, g1_ref, h_ref, p_ref, wr_ref, wd_ref, wo_ref, l1g_ref,
                 l1b_ref, sg_ref, su_ref, sd_ref, pw_ref, pg_ref, rwt_ref, rb_ref,
                 h1_ref, base_ref, idx_ref, w_ref, rank_ref, cnt_ref, carry_ref):
    i = pl.program_id(0)

    @pl.when(i == 0)
    def _():
        carry_ref[...] = jnp.zeros_like(carry_ref)

    y_r = _dot(yr_ref[...], wr_ref[...])
    y_d = _dot(od_ref[...], wd_ref[...])
    merged = g0_ref[...].astype(F32) * y_r + g1_ref[...].astype(F32) * y_d
    mix = _dot(merged.astype(BF16), wo_ref[...])
    h1 = _layer_norm(DN_ALPHA * h_ref[...] + mix, l1g_ref[...], l1b_ref[...])
    for s in range(ROW_SUB):
        h1_ref[_col_block(0, h1.shape[0], s)] = h1[:, s * LANES:(s + 1) * LANES]
    hb = h1.astype(BF16)

    shared = _dot((_silu(_dot(hb, sg_ref[...])) * _dot(hb, su_ref[...])).astype(BF16), sd_ref[...])
    ple = _dot(p_ref[...].astype(BF16), pw_ref[...]) * jax.nn.sigmoid(_dot(hb, pg_ref[...]))
    base_ref[...] = DN_ALPHA * h1 + shared + ple

    h_lo = (h1 - hb.astype(F32)).astype(BF16)
    rw = rwt_ref[...]
    rw_hi = rw.astype(BF16)
    rw_lo = (rw - rw_hi.astype(F32)).astype(BF16)
    logits = _dot_nt(rw_hi, hb) + (_dot_nt(rw_hi, h_lo) + _dot_nt(rw_lo, hb))
    scores = jax.nn.sigmoid(logits)
    idxs, ws, ranks, new_carry = _route(scores, rb_ref[...], carry_ref[...])
    idx_ref[...] = jnp.concatenate(idxs, 0)
    w_ref[...] = jnp.concatenate(ws, 0)
    rank_ref[...] = jnp.concatenate(ranks, 0)
    carry_ref[...] = new_carry
    cnt_ref[...] = new_carry.astype(I32)


def _post_call(yr, od, proj, h, p2, wr, wd, wo, l1g, l1b, sg, su, sd, pw, pg, rwt, rb):
    n = h.shape[0]
    tm = POST_TM
    g0b = SEG_GATE // D_MODEL

    def full(a):
        return pl.BlockSpec(a.shape, lambda i: (0,) * a.ndim)

    row = lambda w: pl.BlockSpec((tm, w), lambda i: (i, 0))
    tok = lambda: pl.BlockSpec((TOP_K, tm), lambda i: (0, i))
    return pl.pallas_call(
        _post_kernel,
        name="post",
        grid=(n // tm,),
        in_specs=[
            row(RET_HEADS * RET_V), row(D_MODEL),
            pl.BlockSpec((tm, D_MODEL), lambda i: (i, g0b)),
            pl.BlockSpec((tm, D_MODEL), lambda i: (i, g0b + 1)),
            row(D_MODEL), row(PLE_DIM),
            full(wr), full(wd), full(wo), full(l1g), full(l1b), full(sg), full(su), full(sd),
            full(pw), full(pg), full(rwt), full(rb),
        ],
        out_specs=[pl.BlockSpec((tm * ROW_SUB, LANES), lambda i: (i, 0)), row(D_MODEL),
                   tok(), tok(), tok(), pl.BlockSpec((N_EXPERTS, 1), lambda i: (0, 0))],
        out_shape=[
            jax.ShapeDtypeStruct((n * ROW_SUB, LANES), F32),
            jax.ShapeDtypeStruct((n, D_MODEL), F32),
            jax.ShapeDtypeStruct((TOP_K, n), I32),
            jax.ShapeDtypeStruct((TOP_K, n), F32),
            jax.ShapeDtypeStruct((TOP_K, n), I32),
            jax.ShapeDtypeStruct((N_EXPERTS, 1), I32),
        ],
        scratch_shapes=[pltpu.VMEM((N_EXPERTS, 1), F32)],
        compiler_params=pltpu.CompilerParams(
            dimension_semantics=("arbitrary",), vmem_limit_bytes=VMEM_LIMIT),
    )(yr, od, proj, proj, h, p2, wr, wd, wo, l1g, l1b, sg, su, sd, pw, pg, rwt, rb)


def _dest_kernel(idx_ref, rank_ref, ps_ref, o_ref):
    idx = idx_ref[...]
    ps = ps_ref[...]
    td = idx.shape[1]
    row = lax.broadcasted_iota(I32, (N_EXPERTS, td), 0)
    outs = []
    for k in range(TOP_K):
        sel = jnp.sum(jnp.where(row == idx[k:k + 1], ps, 0.0), 0, keepdims=True)
        outs.append(sel.astype(I32))
    o_ref[...] = jnp.concatenate(outs, 0) + rank_ref[...]


def _dest_call(idx, rank, pstart_f):
    n = idx.shape[1]
    td = min(DEST_T, n)
    spec = pl.BlockSpec((TOP_K, td), lambda i: (0, i))
    return pl.pallas_call(
        _dest_kernel,
        name="dest",
        grid=(n // td,),
        in_specs=[spec, spec, pl.BlockSpec((N_EXPERTS, 1), lambda i: (0, 0))],
        out_specs=spec,
        out_shape=jax.ShapeDtypeStruct((TOP_K, n), I32),
        compiler_params=pltpu.CompilerParams(dimension_semantics=("arbitrary",)),
    )(idx, rank, pstart_f)


def _expert_kernel(be_ref, nu_ref, rt_ref, rtn_ref, h_hbm, wg_ref, wu_ref, wd_ref, y_ref,
                   buf_ref, wgb_ref, wub_ref, wdb_ref, sem):
    i = pl.program_id(0)
    nb = pl.num_programs(0)
    g = EXPERT_BLOCK

    def wait(s):
        pltpu.make_async_copy(_row_span(h_hbm, 0, g), _row_span(buf_ref, s * g, g), sem.at[s]).wait()

    @pl.when(i == 0)
    def _():
        for j in range(GATHER_AHEAD):
            def body(r, c, j=j):
                pltpu.make_async_copy(_one_row(h_hbm, rt_ref[j, 0, r]),
                                      _one_row(buf_ref, j * g + r), sem.at[j]).start()
                return c
            lax.fori_loop(0, g, body, 0, unroll=8)

    changed = jnp.logical_or(i == 0, be_ref[i] != be_ref[jnp.maximum(i - 1, 0)])

    @pl.when(changed)
    def _():
        wgb_ref[...] = wg_ref[0].astype(BF16)
        wub_ref[...] = wu_ref[0].astype(BF16)
        wdb_ref[...] = wd_ref[0].astype(BF16)

    def block(slot):
        wait(slot)
        x = jnp.concatenate([buf_ref[_col_block(slot * g, g, s)] for s in range(ROW_SUB)],
                            axis=1).astype(BF16)
        ahead = (slot + GATHER_AHEAD) % GATHER_SLOTS
        for r in range(g):
            pltpu.make_async_copy(_one_row(h_hbm, rtn_ref[0, 0, r]),
                                  _one_row(buf_ref, ahead * g + r), sem.at[ahead]).start()
        act = _silu(_dot(x, wgb_ref[...])) * _dot(x, wub_ref[...])
        y = _dot(act.astype(BF16), wdb_ref[...])
        for s in range(ROW_SUB):
            y_ref[_col_block(0, g, s)] = y[:, s * LANES:(s + 1) * LANES]

        drain(slot)

    def drain(slot):
        @pl.when(i == nb - 1)
        def _():
            for j in range(1, GATHER_AHEAD + 1):
                wait((slot + j) % GATHER_SLOTS)

    def idle_block(slot):
        wait(slot)
        ahead = (slot + GATHER_AHEAD) % GATHER_SLOTS

        def body(r, c):
            pltpu.make_async_copy(_one_row(h_hbm, rtn_ref[0, 0, r]),
                                  _one_row(buf_ref, ahead * g + r), sem.at[ahead]).start()
            return c
        lax.fori_loop(0, g, body, 0, unroll=8)
        y_ref[...] = jnp.zeros_like(y_ref)
        drain(slot)

    used = i < nu_ref[0]
    for slot in range(GATHER_SLOTS):
        in_slot = i % GATHER_SLOTS == slot
        pl.when(jnp.logical_and(in_slot, used))(functools.partial(block, slot))
        pl.when(jnp.logical_and(in_slot, jnp.logical_not(used)))(
            functools.partial(idle_block, slot))


def _expert_call(block_e, n_used, row_tok3, h1, wg, wu, wd):
    nb = block_e.shape[0]
    g = EXPERT_BLOCK
    grid_spec = pltpu.PrefetchScalarGridSpec(
        num_scalar_prefetch=2,
        grid=(nb,),
        in_specs=[
            pl.BlockSpec((GATHER_AHEAD, 1, g), lambda i, be, nu: (0, 0, 0),
                         memory_space=pltpu.SMEM),
            pl.BlockSpec((1, 1, g),
                         lambda i, be, nu: (jnp.minimum(i + GATHER_AHEAD, nb - 1), 0, 0),
                         memory_space=pltpu.SMEM),
            pl.BlockSpec(memory_space=pl.ANY),
            pl.BlockSpec((1, D_MODEL, D_EXPERT), lambda i, be, nu: (be[i], 0, 0)),
            pl.BlockSpec((1, D_MODEL, D_EXPERT), lambda i, be, nu: (be[i], 0, 0)),
            pl.BlockSpec((1, D_EXPERT, D_MODEL), lambda i, be, nu: (be[i], 0, 0)),
        ],
        out_specs=pl.BlockSpec((g * ROW_SUB, LANES), lambda i, be, nu: (i, 0)),
        scratch_shapes=[
            pltpu.VMEM((GATHER_SLOTS * g * ROW_SUB, LANES), F32),
            pltpu.VMEM((D_MODEL, D_EXPERT), BF16),
            pltpu.VMEM((D_MODEL, D_EXPERT), BF16),
            pltpu.VMEM((D_EXPERT, D_MODEL), BF16),
            pltpu.SemaphoreType.DMA((GATHER_SLOTS,)),
        ],
    )
    assert nb > GATHER_AHEAD
    return pl.pallas_call(
        _expert_kernel,
        name="experts",
        grid_spec=grid_spec,
        out_shape=jax.ShapeDtypeStruct((nb * g * ROW_SUB, LANES), F32),
        compiler_params=pltpu.CompilerParams(
            dimension_semantics=("arbitrary",), vmem_limit_bytes=VMEM_LIMIT),
    )(block_e, n_used, row_tok3, row_tok3, h1, wg, wu, wd)


def _combine_kernel(d_ref, dn_ref, y_hbm, w_ref, base_ref, g_ref, b_ref, o_ref, buf_ref, sem):
    i = pl.program_id(0)
    nt = pl.num_programs(0)
    tt = COMB_T

    def plane(s, k):
        return (s * TOP_K + k) * tt

    def wait(s):
        for k in range(TOP_K):
            pltpu.make_async_copy(_row_span(y_hbm, 0, tt), _row_span(buf_ref, plane(s, k), tt),
                                  sem.at[s]).wait()

    @pl.when(i == 0)
    def _():
        for j in range(GATHER_AHEAD):
            def body(t, c, j=j):
                for k in range(TOP_K):
                    pltpu.make_async_copy(_one_row(y_hbm, d_ref[j, k, t]),
                                          _one_row(buf_ref, plane(j, k) + t), sem.at[j]).start()
                return c
            lax.fori_loop(0, tt, body, 0)

    def tile(slot):
        wait(slot)
        ahead = (slot + GATHER_AHEAD) % GATHER_SLOTS
        for t in range(tt):
            for k in range(TOP_K):
                pltpu.make_async_copy(_one_row(y_hbm, dn_ref[0, k, t]),
                                      _one_row(buf_ref, plane(ahead, k) + t),
                                      sem.at[ahead]).start()
        w = w_ref[...]
        parts = []
        for s in range(ROW_SUB):
            part = w[:, 0:1] * buf_ref[_col_block(plane(slot, 0), tt, s)]
            for k in range(1, TOP_K):
                part = part + w[:, k:k + 1] * buf_ref[_col_block(plane(slot, k), tt, s)]
            parts.append(part)
        routed = jnp.concatenate(parts, axis=1)
        o_ref[...] = _layer_norm(base_ref[...] + routed, g_ref[...], b_ref[...])

        @pl.when(i == nt - 1)
        def _():
            for j in range(1, GATHER_AHEAD + 1):
                wait((slot + j) % GATHER_SLOTS)

    for slot in range(GATHER_SLOTS):
        pl.when(i % GATHER_SLOTS == slot)(functools.partial(tile, slot))


def _combine_call(dest3, y, w_t, base, ln_g, ln_b):
    n = base.shape[0]
    tt = COMB_T
    nt = n // tt
    return pl.pallas_call(
        _combine_kernel,
        name="combine",
        grid=(nt,),
        in_specs=[
            pl.BlockSpec((GATHER_AHEAD, TOP_K, tt), lambda i: (0, 0, 0), memory_space=pltpu.SMEM),
            pl.BlockSpec((1, TOP_K, tt), lambda i: (jnp.minimum(i + GATHER_AHEAD, nt - 1), 0, 0),
                         memory_space=pltpu.SMEM),
            pl.BlockSpec(memory_space=pl.ANY),
            pl.BlockSpec((tt, TOP_K), lambda i: (i, 0)),
            pl.BlockSpec((tt, D_MODEL), lambda i: (i, 0)),
            pl.BlockSpec((1, D_MODEL), lambda i: (0, 0)),
            pl.BlockSpec((1, D_MODEL), lambda i: (0, 0)),
        ],
        out_specs=pl.BlockSpec((tt, D_MODEL), lambda i: (i, 0)),
        out_shape=jax.ShapeDtypeStruct((n, D_MODEL), F32),
        scratch_shapes=[
            pltpu.VMEM((GATHER_SLOTS * TOP_K * tt * ROW_SUB, LANES), F32),
            pltpu.SemaphoreType.DMA((GATHER_SLOTS,)),
        ],
        compiler_params=pltpu.CompilerParams(
            dimension_semantics=("arbitrary",), vmem_limit_bytes=VMEM_LIMIT),
    )(dest3, dest3, y, w_t, base, ln_g, ln_b)


def _xpos_tables(seq):
    half = RET_QK // 2
    pos = jnp.arange(seq, dtype=F32)
    theta = 1.0 / (XPOS_BASE ** jnp.linspace(0.0, 1.0, half, dtype=F32))
    ang = pos[:, None] * theta[None, :]
    cos = jnp.cos(ang)
    sin = jnp.sin(ang)
    return jnp.concatenate([cos, cos], -1), jnp.concatenate([-sin, sin], -1)


def _decay_tables():
    c = RET_CHUNK
    log_gamma = jnp.log(1.0 - 2.0 ** (-5.0 - jnp.arange(RET_HEADS, dtype=F32)))
    idx = jnp.arange(c, dtype=F32)
    rel = idx[:, None] - idx[None, :]
    inner = jnp.where(rel >= 0, jnp.exp(jnp.maximum(rel, 0.0)[None] * log_gamma[:, None, None]), 0.0)
    q_dec = jnp.exp((idx + 1.0)[None, :] * log_gamma[:, None])
    k_dec = jnp.exp((c - 1.0 - idx)[None, :] * log_gamma[:, None])
    c_dec = jnp.exp(c * log_gamma)
    cdec = jnp.broadcast_to(c_dec[:, None, None], (RET_HEADS, 1, RET_V))
    return inner, q_dec[:, :, None], k_dec[:, :, None], cdec


def _t5_bucket(n):
    max_exact = REL_BUCKETS // 2
    nf = jnp.maximum(n, 1).astype(F32)
    large = max_exact + (jnp.log(nf / max_exact) / math.log(REL_MAX_DIST / max_exact)
                         * (REL_BUCKETS - max_exact)).astype(I32)
    large = jnp.minimum(large, REL_BUCKETS - 1)
    return jnp.where(n < max_exact, n, large)


def _bias_tables(rel_bias, t):
    c = REL_MAX_DIST
    nb = t // c
    far = rel_bias[REL_BUCKETS - 1].astype(F32)
    d = jnp.arange(c)[None, :] - jnp.arange(c)[:, None]
    near0 = (jnp.take(rel_bias, _t5_bucket(jnp.maximum(d, 0)), axis=0).astype(F32) - far) * LOG2E
    near0 = jnp.where((d >= 0)[:, :, None], near0, MASK_VALUE)
    near1 = (jnp.take(rel_bias, _t5_bucket(d + c), axis=0).astype(F32) - far) * LOG2E
    tile0 = jnp.tile(near0, (nb, nb, 1))
    tile1 = jnp.tile(near1, (nb, nb, 1))
    kb = (jnp.arange(t) // c)[:, None, None]
    qb = (jnp.arange(t) // c)[None, :, None]
    diag = jnp.where(qb == kb, tile0,
                     jnp.where(qb == kb + 1, tile1, jnp.where(qb > kb, 0.0, MASK_VALUE)))
    left = jnp.where((qb == 0) & (kb == nb - 1), tile1, 0.0)
    return jnp.transpose(jnp.stack([diag, left], 0), (3, 0, 1, 2))


def kernel(x, p, ln_in_g, ln_in_b, rel_bias, w_in, ret_gn_g, lambda_q1, lambda_k1, lambda_q2,
           lambda_k2, diff_subln_g, w_ret_o, w_diff_o, w_o, ln1_g, ln1_b, router_w, router_bias,
           exp_w_gate, exp_w_up, exp_w_down, sh_w_gate, sh_w_up, sh_w_down, ple_w, ple_gate_w,
           ln2_g, ln2_b):
    batch, seq, d = x.shape
    n = batch * seq
    assert d == D_MODEL and seq % (ATTN_T * ATTN_KSUB) == 0 and seq % PROJ_TM == 0
    assert n % min(DEST_T, n) == 0 and n % COMB_T == 0 and n % POST_TM == 0
    x2 = x.reshape(n, d)
    row = lambda a: a.reshape(1, -1).astype(F32)

    cosf, sinf = _xpos_tables(seq)
    inner, qdec, kdec, cdec = _decay_tables()
    h, proj = _proj_call(x2, row(ln_in_g), row(ln_in_b), cosf, sinf, w_in[0].astype(BF16), seq)

    yr = _ret_call(proj, inner, qdec, kdec, cdec,
                   ret_gn_g[0].reshape(RET_HEADS, 1, RET_V).astype(F32), batch, seq)

    od = _attn_call(proj, _bias_tables(rel_bias, ATTN_T), row(lambda_q1[0]), row(lambda_k1[0]),
                    row(lambda_q2[0]), row(lambda_k2[0]),
                    diff_subln_g[0].reshape(-1, 1).astype(F32), batch, seq)

    h1, base, idx, w, rank, counts = _post_call(
        yr, od, proj, h, p[0].reshape(n, PLE_DIM),
        w_ret_o[0].astype(BF16), w_diff_o[0].astype(BF16), w_o[0].astype(BF16),
        row(ln1_g[0]), row(ln1_b[0]),
        sh_w_gate[0].astype(BF16), sh_w_up[0].astype(BF16), sh_w_down[0].astype(BF16),
        ple_w[0].astype(BF16), ple_gate_w[0].astype(BF16),
        router_w[0].T.astype(F32), router_bias[0].reshape(N_EXPERTS, 1).astype(F32))

    g = EXPERT_BLOCK
    counts = counts.reshape(N_EXPERTS)
    padded = (counts + g - 1) // g * g
    pend = jnp.cumsum(padded)
    pstart = pend - padded
    n_rows = -(-(n * TOP_K + N_EXPERTS * (g - 1)) // g) * g
    n_blocks = n_rows // g
    block_start = jnp.arange(n_blocks, dtype=I32) * g
    block_e = jnp.minimum(
        jnp.sum((pend[None, :] <= block_start[:, None]).astype(I32), axis=1),
        N_EXPERTS - 1).astype(I32)

    dest = _dest_call(idx, rank, pstart.astype(F32).reshape(N_EXPERTS, 1))
    tok = jnp.broadcast_to(jnp.arange(n, dtype=I32)[None, :], (TOP_K, n))
    row_tok = jnp.zeros((n_rows,), I32).at[dest.reshape(-1)].set(
        tok.reshape(-1), unique_indices=True, mode='drop')

    n_used = (pend[N_EXPERTS - 1:] // g).astype(I32)
    y = _expert_call(block_e, n_used, row_tok.reshape(n_blocks, 1, g), h1,
                     exp_w_gate[0], exp_w_up[0], exp_w_down[0])

    dest3 = dest.reshape(TOP_K, n // COMB_T, COMB_T).transpose(1, 0, 2)
    out = _combine_call(dest3, y, w.T, base, row(ln2_g[0]), row(ln2_b[0]))
    return out.reshape(batch, seq, d)
```

```python
import functools
import math

import numpy as np
import jax
import jax.numpy as jnp
from jax import lax
from jax.experimental import pallas as pl
from jax.experimental.pallas import tpu as pltpu

F32 = jnp.float32
BF16 = jnp.bfloat16
I32 = jnp.int32

D_MODEL = 1024
PLE_DIM = 256
RET_HEADS = 8
RET_QK = 128
RET_V = 256
RET_CHUNK = 128
XPOS_BASE = 10000.0
DIFF_HEADS = 8
DIFF_HD = 64
REL_BUCKETS = 32
REL_MAX_DIST = 128
N_EXPERTS = 256
TOP_K = 8
N_GROUPS = 8
GROUP_SIZE = N_EXPERTS // N_GROUPS
TOPK_GROUPS = 4
D_EXPERT = 256
ROUTED_SCALE = 2.5
EXPERT_BLOCK = 256
DEPTH = 1
DN_ALPHA = (2.0 * DEPTH) ** 0.25
EPS = 1e-5
LAM_INIT = 0.8 - 0.6 * math.exp(-0.3 * 0)

SEG_QR, SEG_KR, SEG_VR, SEG_GR, SEG_QD, SEG_KD, SEG_VD, SEG_GATE = (
    0, 1024, 2048, 4096, 6144, 7168, 8192, 9216)
IN_COLS = 11264

LANES = 128
ROW_SUB = D_MODEL // LANES
MASK_VALUE = -1e30
LOG2E = math.log2(math.e)
VMEM_LIMIT = 56 * 1024 * 1024

RET_SUB = 4
PROJ_TM = 1024
PROJ_TN = 1024
ATTN_T = 512
ATTN_KSUB = 2
ATTN_VT_ROWS = 2 * DIFF_HD + 16
POST_TM = 512
DEST_T = 2048
COMB_T = 32
GATHER_DMA_PRIORITY = 1
GATHER_AHEAD = 3
GATHER_SLOTS = GATHER_AHEAD + 1


def _layer_norm(x, g, b):
    mu = jnp.mean(x, -1, keepdims=True)
    d = x - mu
    var = jnp.mean(d * d, -1, keepdims=True)
    return d * lax.rsqrt(var + EPS) * g + b


def _silu(x):
    return x * jax.nn.sigmoid(x)


def _dot(a, b):
    return jnp.dot(a, b, preferred_element_type=F32)


def _tree(op, items):
    items = list(items)
    while len(items) > 1:
        items = [op(items[i], items[i + 1]) if i + 1 < len(items) else items[i]
                 for i in range(0, len(items), 2)]
    return items[0]


def _col_block(first_row, rows, s):
    return (pl.ds(first_row * ROW_SUB + s, rows, stride=ROW_SUB), slice(None))


def _one_row(ref, row):
    if isinstance(row, int):
        return ref.at[pl.ds(row * ROW_SUB, ROW_SUB), :]
    return ref.at[pl.ds(pl.multiple_of(row * ROW_SUB, ROW_SUB), ROW_SUB), :]


def _row_span(ref, first_row, rows):
    return ref.at[pl.ds(first_row * ROW_SUB, rows * ROW_SUB), :]


def _dot_nt(a, b):
    return lax.dot_general(a, b, (((1,), (1,)), ((), ())), preferred_element_type=F32)


def _proj_kernel(x_ref, g_ref, b_ref, cos_ref, sin_ref, w_ref, h_ref, o_ref, hb_ref):
    j = pl.program_id(1)

    @pl.when(j == 0)
    def _():
        h = _layer_norm(x_ref[...], g_ref[...], b_ref[...])
        h_ref[...] = h
        hb_ref[...] = h.astype(BF16)

    acc = _dot(hb_ref[...], w_ref[...])

    def rotate(scale):
        c = cos_ref[...]
        s = sin_ref[...]
        for hh in range(PROJ_TN // RET_QK):
            t = acc[:, hh * RET_QK:(hh + 1) * RET_QK]
            v = t * c + pltpu.roll(t, RET_QK // 2, 1) * s
            if scale != 1.0:
                v = v * scale
            o_ref[:, hh * RET_QK:(hh + 1) * RET_QK] = v.astype(BF16)

    @pl.when(j == SEG_QR // PROJ_TN)
    def _():
        rotate(1.0)

    @pl.when(j == SEG_KR // PROJ_TN)
    def _():
        rotate(RET_QK ** -0.5)

    is_gr = (j >= SEG_GR // PROJ_TN) & (j < SEG_QD // PROJ_TN)
    is_qd = j == SEG_QD // PROJ_TN
    is_gate = j >= SEG_GATE // PROJ_TN
    is_plain = ((j >= SEG_VR // PROJ_TN) & (j < SEG_GR // PROJ_TN)) | (
        (j >= SEG_KD // PROJ_TN) & (j < SEG_GATE // PROJ_TN))

    @pl.when(is_plain)
    def _():
        o_ref[...] = acc.astype(BF16)

    @pl.when(is_gr)
    def _():
        o_ref[...] = _silu(acc).astype(BF16)

    @pl.when(is_qd)
    def _():
        o_ref[...] = (acc * (DIFF_HD ** -0.5 * LOG2E)).astype(BF16)

    @pl.when(is_gate)
    def _():
        o_ref[...] = jax.nn.sigmoid(acc).astype(BF16)


def _proj_call(x2, ln_g, ln_b, cosf, sinf, w_bf, seq):
    n = x2.shape[0]
    tm, tn = PROJ_TM, PROJ_TN
    s_tiles = seq // tm
    return pl.pallas_call(
        _proj_kernel,
        name="proj",
        grid=(n // tm, IN_COLS // tn),
        in_specs=[
            pl.BlockSpec((tm, D_MODEL), lambda i, j: (i, 0)),
            pl.BlockSpec((1, D_MODEL), lambda i, j: (0, 0)),
            pl.BlockSpec((1, D_MODEL), lambda i, j: (0, 0)),
            pl.BlockSpec((tm, RET_QK), lambda i, j: (i % s_tiles, 0)),
            pl.BlockSpec((tm, RET_QK), lambda i, j: (i % s_tiles, 0)),
            pl.BlockSpec((D_MODEL, tn), lambda i, j: (0, j)),
        ],
        out_specs=[
            pl.BlockSpec((tm, D_MODEL), lambda i, j: (i, 0)),
            pl.BlockSpec((tm, tn), lambda i, j: (i, j)),
        ],
        out_shape=[
            jax.ShapeDtypeStruct((n, D_MODEL), F32),
            jax.ShapeDtypeStruct((n, IN_COLS), BF16),
        ],
        scratch_shapes=[pltpu.VMEM((tm, D_MODEL), BF16)],
        compiler_params=pltpu.CompilerParams(
            dimension_semantics=("arbitrary", "arbitrary"), vmem_limit_bytes=VMEM_LIMIT),
    )(x2, ln_g, ln_b, cosf, sinf, w_bf)


def _ret_kernel(q_ref, k_ref, v_ref, g_ref, inner_ref, qdec_ref, kdec_ref, cdec_ref, gn_ref,
                o_ref, state_ref):
    c = pl.program_id(2)

    @pl.when(c == 0)
    def _():
        state_ref[...] = jnp.zeros_like(state_ref)

    st = state_ref[...]
    for j in range(RET_SUB):
        rows = slice(j * RET_CHUNK, (j + 1) * RET_CHUNK)
        q = q_ref[rows, :]
        k = k_ref[rows, :]
        v = v_ref[rows, :]
        scores = _dot_nt(q, k) * inner_ref[0]
        o = _dot(scores.astype(BF16), v) + qdec_ref[0] * _dot(q, st.astype(BF16))
        kd = (k.astype(F32) * kdec_ref[0]).astype(BF16)
        kv = lax.dot_general(kd, v, (((0,), (0,)), ((), ())), preferred_element_type=F32)
        st = cdec_ref[0] * st + kv

        mu = jnp.mean(o, -1, keepdims=True)
        d = o - mu
        var = jnp.mean(d * d, -1, keepdims=True)
        on = d * lax.rsqrt(var + EPS) * gn_ref[0]
        o_ref[rows, :] = (g_ref[rows, :].astype(F32) * on).astype(BF16)
    state_ref[...] = st


def _ret_call(proj, inner, qdec, kdec, cdec, gn, batch, seq):
    n = proj.shape[0]
    c = RET_CHUNK
    rows = RET_SUB * c
    nc = seq // rows
    qb = SEG_QR // RET_QK
    kb = SEG_KR // RET_QK
    vb = SEG_VR // RET_V
    gb = SEG_GR // RET_V
    return pl.pallas_call(
        _ret_kernel,
        name="retention",
        grid=(batch, RET_HEADS, nc),
        in_specs=[
            pl.BlockSpec((rows, RET_QK), lambda b, h, i: (b * nc + i, qb + h)),
            pl.BlockSpec((rows, RET_QK), lambda b, h, i: (b * nc + i, kb + h)),
            pl.BlockSpec((rows, RET_V), lambda b, h, i: (b * nc + i, vb + h)),
            pl.BlockSpec((rows, RET_V), lambda b, h, i: (b * nc + i, gb + h)),
            pl.BlockSpec((1, c, c), lambda b, h, i: (h, 0, 0)),
            pl.BlockSpec((1, c, 1), lambda b, h, i: (h, 0, 0)),
            pl.BlockSpec((1, c, 1), lambda b, h, i: (h, 0, 0)),
            pl.BlockSpec((1, 1, RET_V), lambda b, h, i: (h, 0, 0)),
            pl.BlockSpec((1, 1, RET_V), lambda b, h, i: (h, 0, 0)),
        ],
        out_specs=pl.BlockSpec((rows, RET_V), lambda b, h, i: (b * nc + i, h)),
        out_shape=jax.ShapeDtypeStruct((n, RET_HEADS * RET_V), BF16),
        scratch_shapes=[pltpu.VMEM((RET_QK, RET_V), F32)],
        compiler_params=pltpu.CompilerParams(
            dimension_semantics=("arbitrary", "arbitrary", "arbitrary"),
            vmem_limit_bytes=VMEM_LIMIT),
    )(proj, proj, proj, proj, inner, qdec, kdec, cdec, gn)


def _attn_kernel(qi_ref, ki_ref, q_ref, k_ref, vt_ref, tab_ref, lq1_ref, lk1_ref, lq2_ref,
                 lk2_ref, sg_ref, o_ref, qst_ref, m_ref, acc_ref, s_ref, p_ref, mx_ref):
    t = ATTN_T
    hd2 = 2 * DIFF_HD
    p = pl.program_id(2)
    qi = qi_ref[p]
    ki = ki_ref[p]

    @pl.when(ki == 0)
    def _():
        qt = q_ref[...].astype(F32).T
        sub = lax.broadcasted_iota(I32, qt.shape, 0)
        qst_ref[:, 0:t] = jnp.where(sub < DIFF_HD, qt, 0.0).astype(BF16)
        qst_ref[:, t:2 * t] = jnp.where(sub >= DIFF_HD, qt, 0.0).astype(BF16)
        m_ref[...] = jnp.full_like(m_ref, MASK_VALUE)
        acc_ref[...] = jnp.zeros_like(acc_ref)

    def steps(*biases):
        for sub, bias in enumerate(biases):
            kk = k_ref[sub * t:(sub + 1) * t, :]
            for half in range(2):
                cols = slice(half * t, (half + 1) * t)
                s = _dot(kk, qst_ref[:, cols])
                if bias is not None:
                    s = s + tab_ref[0, bias]
                s_ref[sub, half] = s
                mx_ref[sub, half] = _tree(jnp.maximum, [s[r:r + 8, :] for r in range(0, t, 8)])
        for sub in range(len(biases)):
            vt = vt_ref[:, sub * t:(sub + 1) * t]
            for half in range(2):
                cols = slice(half * t, (half + 1) * t)
                m_prev = m_ref[:, cols]
                m_new = jnp.maximum(m_prev, jnp.max(mx_ref[sub, half], 0, keepdims=True))
                alpha = jnp.exp2(m_prev - m_new)
                for r in range(0, t, 64):
                    p_ref[sub, half, r:r + 64, :] = jnp.exp2(
                        s_ref[sub, half, r:r + 64, :] - m_new).astype(BF16)
                acc_ref[:, cols] = alpha * acc_ref[:, cols] + _dot(vt, p_ref[sub, half])
                m_ref[:, cols] = m_new

    def finalize():
        lam = (jnp.exp(jnp.sum(lq1_ref[...] * lk1_ref[...], -1, keepdims=True))
               - jnp.exp(jnp.sum(lq2_ref[...] * lk2_ref[...], -1, keepdims=True)) + LAM_INIT)
        o1 = acc_ref[0:hd2, 0:t] / acc_ref[hd2:hd2 + 1, 0:t]
        o2 = acc_ref[0:hd2, t:2 * t] / acc_ref[hd2:hd2 + 1, t:2 * t]
        o = o1 - lam * o2
        o = o * lax.rsqrt(jnp.mean(o * o, 0, keepdims=True) + EPS) * (sg_ref[...] * (1.0 - LAM_INIT))
        o_ref[...] = o.T.astype(BF16)

    assert ATTN_KSUB == 2
    last = ki * ATTN_KSUB + 1
    pl.when(last < qi - 1)(functools.partial(steps, None, None))
    pl.when(last == qi - 1)(functools.partial(steps, None, 1))

    @pl.when(last == qi)
    def _():
        steps(1, 0)
        finalize()

    @pl.when(last == qi + 1)
    def _():
        steps(0)
        finalize()


def _attn_call(proj, tab, lq1, lk1, lq2, lk2, sg, batch, seq):
    n = proj.shape[0]
    t = ATTN_T
    nq = seq // t
    tk = ATTN_KSUB * t
    nk = seq // tk
    pairs = [(a, b) for a in range(nq) for b in range(a // ATTN_KSUB + 1)]
    qi_tab = jnp.asarray(np.array([a for a, _ in pairs], np.int32))
    ki_tab = jnp.asarray(np.array([b for _, b in pairs], np.int32))
    hd2 = 2 * DIFF_HD
    qb, kb = SEG_QD // hd2, SEG_KD // hd2
    v = proj[:, SEG_VD:SEG_VD + DIFF_HEADS * hd2].reshape(batch, seq, DIFF_HEADS, hd2)
    vt = jnp.transpose(v, (0, 2, 3, 1))
    extra = jnp.zeros((batch, DIFF_HEADS, ATTN_VT_ROWS - hd2, seq), BF16).at[:, :, 0, :].set(1.0)
    vt = jnp.concatenate([vt, extra], axis=2).reshape(batch * DIFF_HEADS * ATTN_VT_ROWS, seq)
    vec = pl.BlockSpec((1, DIFF_HD), lambda b, h, p, qi, ki: (0, 0))
    grid_spec = pltpu.PrefetchScalarGridSpec(
        num_scalar_prefetch=2,
        grid=(batch, DIFF_HEADS, len(pairs)),
        in_specs=[
            pl.BlockSpec((t, hd2), lambda b, h, p, qi, ki: (b * nq + qi[p], qb + h)),
            pl.BlockSpec((tk, hd2), lambda b, h, p, qi, ki: (b * nk + ki[p], kb + h)),
            pl.BlockSpec((ATTN_VT_ROWS, tk), lambda b, h, p, qi, ki: (b * DIFF_HEADS + h, ki[p])),
            pl.BlockSpec((1, 2, t, t), lambda b, h, p, qi, ki: (h, 0, 0, 0)),
            vec, vec, vec, vec,
            pl.BlockSpec((hd2, 1), lambda b, h, p, qi, ki: (0, 0)),
        ],
        out_specs=pl.BlockSpec((t, hd2), lambda b, h, p, qi, ki: (b * nq + qi[p], h)),
        scratch_shapes=[
            pltpu.VMEM((hd2, 2 * t), BF16),
            pltpu.VMEM((1, 2 * t), F32),
            pltpu.VMEM((ATTN_VT_ROWS, 2 * t), F32),
            pltpu.VMEM((ATTN_KSUB, 2, t, t), F32),
            pltpu.VMEM((ATTN_KSUB, 2, t, t), BF16),
            pltpu.VMEM((ATTN_KSUB, 2, 8, t), F32),
        ],
    )
    return pl.pallas_call(
        _attn_kernel,
        name="diffattn",
        grid_spec=grid_spec,
        out_shape=jax.ShapeDtypeStruct((n, DIFF_HEADS * hd2), BF16),
        compiler_params=pltpu.CompilerParams(
            dimension_semantics=("arbitrary", "arbitrary", "arbitrary"),
            vmem_limit_bytes=VMEM_LIMIT),
    )(qi_tab, ki_tab, proj, proj, vt, tab, lq1, lk1, lq2, lk2, sg)


def _route(scores, bias, carry):
    e, tm = scores.shape
    neg = -jnp.inf
    choice = scores + bias
    row_g = lax.broadcasted_iota(I32, (GROUP_SIZE, tm), 0)
    group_scores = []
    for g in range(N_GROUPS):
        blk = choice[g * GROUP_SIZE:(g + 1) * GROUP_SIZE]
        m1 = jnp.max(blk, 0, keepdims=True)
        i1 = jnp.min(jnp.where(blk == m1, row_g, GROUP_SIZE), 0, keepdims=True)
        m2 = jnp.max(jnp.where(row_g == i1, neg, blk), 0, keepdims=True)
        group_scores.append(m1 + m2)
    gsc = jnp.concatenate(group_scores, 0)
    row8 = lax.broadcasted_iota(I32, (N_GROUPS, tm), 0)
    okf = jnp.zeros((N_GROUPS, tm), F32)
    for _ in range(TOPK_GROUPS):
        m = jnp.max(gsc, 0, keepdims=True)
        i = jnp.min(jnp.where(gsc == m, row8, N_GROUPS), 0, keepdims=True)
        hit = row8 == i
        okf = jnp.where(hit, 1.0, okf)
        gsc = jnp.where(hit, neg, gsc)
    masked = jnp.concatenate(
        [jnp.where(jnp.broadcast_to(okf[g:g + 1], (GROUP_SIZE, tm)) > 0.0,
                   choice[g * GROUP_SIZE:(g + 1) * GROUP_SIZE], neg)
         for g in range(N_GROUPS)], 0)
    row = lax.broadcasted_iota(I32, (e, tm), 0)
    idxs, ws = [], []
    multi = jnp.zeros((e, tm), F32)
    for _ in range(TOP_K):
        m = jnp.max(masked, 0, keepdims=True)
        i = jnp.min(jnp.where(masked == m, row, e), 0, keepdims=True)
        hit = row == i
        ws.append(jnp.sum(jnp.where(hit, scores, 0.0), 0, keepdims=True))
        idxs.append(i)
        multi = jnp.where(hit, 1.0, multi)
        masked = jnp.where(hit, neg, masked)
    a = lax.broadcasted_iota(I32, (tm, tm), 0)
    b = lax.broadcasted_iota(I32, (tm, tm), 1)
    upper = jnp.where(a < b, 1.0, 0.0).astype(BF16)
    before = _dot(multi.astype(BF16), upper) + carry
    ranks = [jnp.sum(jnp.where(row == i, before, 0.0), 0, keepdims=True).astype(I32) for i in idxs]
    new_carry = carry + jnp.sum(multi, 1, keepdims=True)
    wsum = ws[0]
    for w in ws[1:]:
        wsum = wsum + w
    ws = [w / wsum * ROUTED_SCALE for w in ws]
    return idxs, ws, ranks, new_carry


def _post_kernel(yr_ref, od_ref, g0_ref, g1_ref, h_ref, p_ref, wr_ref, wd_ref, wo_ref, l1g_ref,
                 l1b_ref, sg_ref, su_ref, sd_ref, pw_ref, pg_ref, rwt_ref, rb_ref,
                 h1_ref, base_ref, idx_ref, w_ref, rank_ref, cnt_ref, carry_ref):
    i = pl.program_id(0)

    @pl.when(i == 0)
    def _():
        carry_ref[...] = jnp.zeros_like(carry_ref)

    y_r = _dot(yr_ref[...], wr_ref[...])
    y_d = _dot(od_ref[...], wd_ref[...])
    merged = g0_ref[...].astype(F32) * y_r + g1_ref[...].astype(F32) * y_d
    mix = _dot(merged.astype(BF16), wo_ref[...])
    h1 = _layer_norm(DN_ALPHA * h_ref[...] + mix, l1g_ref[...], l1b_ref[...])
    for s in range(ROW_SUB):
        h1_ref[_col_block(0, h1.shape[0], s)] = h1[:, s * LANES:(s + 1) * LANES]
    hb = h1.astype(BF16)

    shared = _dot((_silu(_dot(hb, sg_ref[...])) * _dot(hb, su_ref[...])).astype(BF16), sd_ref[...])
    ple = _dot(p_ref[...].astype(BF16), pw_ref[...]) * jax.nn.sigmoid(_dot(hb, pg_ref[...]))
    base_ref[...] = DN_ALPHA * h1 + shared + ple

    h_lo = (h1 - hb.astype(F32)).astype(BF16)
    rw = rwt_ref[...]
    rw_hi = rw.astype(BF16)
    rw_lo = (rw - rw_hi.astype(F32)).astype(BF16)
    logits = _dot_nt(rw_hi, hb) + (_dot_nt(rw_hi, h_lo) + _dot_nt(rw_lo, hb))
    scores = jax.nn.sigmoid(logits)
    idxs, ws, ranks, new_carry = _route(scores, rb_ref[...], carry_ref[...])
    idx_ref[...] = jnp.concatenate(idxs, 0)
    w_ref[...] = jnp.concatenate(ws, 0)
    rank_ref[...] = jnp.concatenate(ranks, 0)
    carry_ref[...] = new_carry
    cnt_ref[...] = new_carry.astype(I32)


def _post_call(yr, od, proj, h, p2, wr, wd, wo, l1g, l1b, sg, su, sd, pw, pg, rwt, rb):
    n = h.shape[0]
    tm = POST_TM
    g0b = SEG_GATE // D_MODEL

    def full(a):
        return pl.BlockSpec(a.shape, lambda i: (0,) * a.ndim)

    row = lambda w: pl.BlockSpec((tm, w), lambda i: (i, 0))
    tok = lambda: pl.BlockSpec((TOP_K, tm), lambda i: (0, i))
    return pl.pallas_call(
        _post_kernel,
        name="post",
        grid=(n // tm,),
        in_specs=[
            row(RET_HEADS * RET_V), row(D_MODEL),
            pl.BlockSpec((tm, D_MODEL), lambda i: (i, g0b)),
            pl.BlockSpec((tm, D_MODEL), lambda i: (i, g0b + 1)),
            row(D_MODEL), row(PLE_DIM),
            full(wr), full(wd), full(wo), full(l1g), full(l1b), full(sg), full(su), full(sd),
            full(pw), full(pg), full(rwt), full(rb),
        ],
        out_specs=[pl.BlockSpec((tm * ROW_SUB, LANES), lambda i: (i, 0)), row(D_MODEL),
                   tok(), tok(), tok(), pl.BlockSpec((N_EXPERTS, 1), lambda i: (0, 0))],
        out_shape=[
            jax.ShapeDtypeStruct((n * ROW_SUB, LANES), F32),
            jax.ShapeDtypeStruct((n, D_MODEL), F32),
            jax.ShapeDtypeStruct((TOP_K, n), I32),
            jax.ShapeDtypeStruct((TOP_K, n), F32),
            jax.ShapeDtypeStruct((TOP_K, n), I32),
            jax.ShapeDtypeStruct((N_EXPERTS, 1), I32),
        ],
        scratch_shapes=[pltpu.VMEM((N_EXPERTS, 1), F32)],
        compiler_params=pltpu.CompilerParams(
            dimension_semantics=("arbitrary",), vmem_limit_bytes=VMEM_LIMIT),
    )(yr, od, proj, proj, h, p2, wr, wd, wo, l1g, l1b, sg, su, sd, pw, pg, rwt, rb)


def _dest_kernel(idx_ref, rank_ref, ps_ref, o_ref):
    idx = idx_ref[...]
    ps = ps_ref[...]
    td = idx.shape[1]
    row = lax.broadcasted_iota(I32, (N_EXPERTS, td), 0)
    outs = []
    for k in range(TOP_K):
        sel = jnp.sum(jnp.where(row == idx[k:k + 1], ps, 0.0), 0, keepdims=True)
        outs.append(sel.astype(I32))
    o_ref[...] = jnp.concatenate(outs, 0) + rank_ref[...]


def _dest_call(idx, rank, pstart_f):
    n = idx.shape[1]
    td = min(DEST_T, n)
    spec = pl.BlockSpec((TOP_K, td), lambda i: (0, i))
    return pl.pallas_call(
        _dest_kernel,
        name="dest",
        grid=(n // td,),
        in_specs=[spec, spec, pl.BlockSpec((N_EXPERTS, 1), lambda i: (0, 0))],
        out_specs=spec,
        out_shape=jax.ShapeDtypeStruct((TOP_K, n), I32),
        compiler_params=pltpu.CompilerParams(dimension_semantics=("arbitrary",)),
    )(idx, rank, pstart_f)


def _expert_kernel(be_ref, nu_ref, rt_ref, rtn_ref, h_hbm, wg_ref, wu_ref, wd_ref, y_ref,
                   buf_ref, wgb_ref, wub_ref, wdb_ref, sem):
    i = pl.program_id(0)
    nb = pl.num_programs(0)
    g = EXPERT_BLOCK

    def wait(s):
        pltpu.make_async_copy(_row_span(h_hbm, 0, g), _row_span(buf_ref, s * g, g), sem.at[s]).wait()

    @pl.when(i == 0)
    def _():
        for j in range(GATHER_AHEAD):
            def body(r, c, j=j):
                pltpu.make_async_copy(_one_row(h_hbm, rt_ref[j, 0, r]),
                                      _one_row(buf_ref, j * g + r), sem.at[j]).start(priority=GATHER_DMA_PRIORITY)
                return c
            lax.fori_loop(0, g, body, 0, unroll=8)

    changed = jnp.logical_or(i == 0, be_ref[i] != be_ref[jnp.maximum(i - 1, 0)])

    @pl.when(changed)
    def _():
        wgb_ref[...] = wg_ref[0].astype(BF16)
        wub_ref[...] = wu_ref[0].astype(BF16)
        wdb_ref[...] = wd_ref[0].astype(BF16)

    def block(slot, more):
        wait(slot)
        x = jnp.concatenate([buf_ref[_col_block(slot * g, g, s)] for s in range(ROW_SUB)],
                            axis=1).astype(BF16)
        if more:
            ahead = (slot + GATHER_AHEAD) % GATHER_SLOTS
            for r in range(g):
                pltpu.make_async_copy(
                    _one_row(h_hbm, rtn_ref[0, 0, r]), _one_row(buf_ref, ahead * g + r),
                    sem.at[ahead]).start(priority=GATHER_DMA_PRIORITY)
        act = _silu(_dot(x, wgb_ref[...])) * _dot(x, wub_ref[...])
        y = _dot(act.astype(BF16), wdb_ref[...])
        for s in range(ROW_SUB):
            y_ref[_col_block(0, g, s)] = y[:, s * LANES:(s + 1) * LANES]

    used = i < nu_ref[0]
    more = i + GATHER_AHEAD < nu_ref[0]
    for slot in range(GATHER_SLOTS):
        in_slot = jnp.logical_and(i % GATHER_SLOTS == slot, used)
        pl.when(jnp.logical_and(in_slot, more))(functools.partial(block, slot, True))
        pl.when(jnp.logical_and(in_slot, jnp.logical_not(more)))(
            functools.partial(block, slot, False))

    @pl.when(jnp.logical_not(used))
    def _():
        y_ref[...] = jnp.zeros_like(y_ref)


def _expert_call(block_e, n_used, row_tok3, h1, wg, wu, wd):
    nb = block_e.shape[0]
    g = EXPERT_BLOCK
    grid_spec = pltpu.PrefetchScalarGridSpec(
        num_scalar_prefetch=2,
        grid=(nb,),
        in_specs=[
            pl.BlockSpec((GATHER_AHEAD, 1, g), lambda i, be, nu: (0, 0, 0),
                         memory_space=pltpu.SMEM),
            pl.BlockSpec((1, 1, g),
                         lambda i, be, nu: (jnp.minimum(i + GATHER_AHEAD, nb - 1), 0, 0),
                         memory_space=pltpu.SMEM),
            pl.BlockSpec(memory_space=pl.ANY),
            pl.BlockSpec((1, D_MODEL, D_EXPERT), lambda i, be, nu: (be[i], 0, 0)),
            pl.BlockSpec((1, D_MODEL, D_EXPERT), lambda i, be, nu: (be[i], 0, 0)),
            pl.BlockSpec((1, D_EXPERT, D_MODEL), lambda i, be, nu: (be[i], 0, 0)),
        ],
        out_specs=pl.BlockSpec((g * ROW_SUB, LANES), lambda i, be, nu: (i, 0)),
        scratch_shapes=[
            pltpu.VMEM((GATHER_SLOTS * g * ROW_SUB, LANES), F32),
            pltpu.VMEM((D_MODEL, D_EXPERT), BF16),
            pltpu.VMEM((D_MODEL, D_EXPERT), BF16),
            pltpu.VMEM((D_EXPERT, D_MODEL), BF16),
            pltpu.SemaphoreType.DMA((GATHER_SLOTS,)),
        ],
    )
    assert nb > GATHER_AHEAD
    return pl.pallas_call(
        _expert_kernel,
        name="experts",
        grid_spec=grid_spec,
        out_shape=jax.ShapeDtypeStruct((nb * g * ROW_SUB, LANES), F32),
        compiler_params=pltpu.CompilerParams(
            dimension_semantics=("arbitrary",), vmem_limit_bytes=VMEM_LIMIT),
    )(block_e, n_used, row_tok3, row_tok3, h1, wg, wu, wd)


def _combine_kernel(d_ref, dn_ref, y_hbm, w_ref, base_ref, g_ref, b_ref, o_ref, buf_ref, sem):
    i = pl.program_id(0)
    nt = pl.num_programs(0)
    tt = COMB_T

    def plane(s, k):
        return (s * TOP_K + k) * tt

    def wait(s):
        for k in range(TOP_K):
            pltpu.make_async_copy(_row_span(y_hbm, 0, tt), _row_span(buf_ref, plane(s, k), tt),
                                  sem.at[s]).wait()

    @pl.when(i == 0)
    def _():
        for j in range(GATHER_AHEAD):
            def body(t, c, j=j):
                for k in range(TOP_K):
                    pltpu.make_async_copy(_one_row(y_hbm, d_ref[j, k, t]),
                                          _one_row(buf_ref, plane(j, k) + t), sem.at[j]).start(priority=GATHER_DMA_PRIORITY)
                return c
            lax.fori_loop(0, tt, body, 0)

    def tile(slot):
        wait(slot)
        ahead = (slot + GATHER_AHEAD) % GATHER_SLOTS
        for t in range(tt):
            for k in range(TOP_K):
                pltpu.make_async_copy(_one_row(y_hbm, dn_ref[0, k, t]),
                                      _one_row(buf_ref, plane(ahead, k) + t),
                                      sem.at[ahead]).start(priority=GATHER_DMA_PRIORITY)
        w = w_ref[...]
        parts = []
        for s in range(ROW_SUB):
            part = w[:, 0:1] * buf_ref[_col_block(plane(slot, 0), tt, s)]
            for k in range(1, TOP_K):
                part = part + w[:, k:k + 1] * buf_ref[_col_block(plane(slot, k), tt, s)]
            parts.append(part)
        routed = jnp.concatenate(parts, axis=1)
        o_ref[...] = _layer_norm(base_ref[...] + routed, g_ref[...], b_ref[...])

        @pl.when(i == nt - 1)
        def _():
            for j in range(1, GATHER_AHEAD + 1):
                wait((slot + j) % GATHER_SLOTS)

    for slot in range(GATHER_SLOTS):
        pl.when(i % GATHER_SLOTS == slot)(functools.partial(tile, slot))


def _combine_call(dest3, y, w_t, base, ln_g, ln_b):
    n = base.shape[0]
    tt = COMB_T
    nt = n // tt
    return pl.pallas_call(
        _combine_kernel,
        name="combine",
        grid=(nt,),
        in_specs=[
            pl.BlockSpec((GATHER_AHEAD, TOP_K, tt), lambda i: (0, 0, 0), memory_space=pltpu.SMEM),
            pl.BlockSpec((1, TOP_K, tt), lambda i: (jnp.minimum(i + GATHER_AHEAD, nt - 1), 0, 0),
                         memory_space=pltpu.SMEM),
            pl.BlockSpec(memory_space=pl.ANY),
            pl.BlockSpec((tt, TOP_K), lambda i: (i, 0)),
            pl.BlockSpec((tt, D_MODEL), lambda i: (i, 0)),
            pl.BlockSpec((1, D_MODEL), lambda i: (0, 0)),
            pl.BlockSpec((1, D_MODEL), lambda i: (0, 0)),
        ],
        out_specs=pl.BlockSpec((tt, D_MODEL), lambda i: (i, 0)),
        out_shape=jax.ShapeDtypeStruct((n, D_MODEL), F32),
        scratch_shapes=[
            pltpu.VMEM((GATHER_SLOTS * TOP_K * tt * ROW_SUB, LANES), F32),
            pltpu.SemaphoreType.DMA((GATHER_SLOTS,)),
        ],
        compiler_params=pltpu.CompilerParams(
            dimension_semantics=("arbitrary",), vmem_limit_bytes=VMEM_LIMIT),
    )(dest3, dest3, y, w_t, base, ln_g, ln_b)


def _xpos_tables(seq):
    half = RET_QK // 2
    pos = jnp.arange(seq, dtype=F32)
    theta = 1.0 / (XPOS_BASE ** jnp.linspace(0.0, 1.0, half, dtype=F32))
    ang = pos[:, None] * theta[None, :]
    cos = jnp.cos(ang)
    sin = jnp.sin(ang)
    return jnp.concatenate([cos, cos], -1), jnp.concatenate([-sin, sin], -1)


def _decay_tables():
    c = RET_CHUNK
    log_gamma = jnp.log(1.0 - 2.0 ** (-5.0 - jnp.arange(RET_HEADS, dtype=F32)))
    idx = jnp.arange(c, dtype=F32)
    rel = idx[:, None] - idx[None, :]
    inner = jnp.where(rel >= 0, jnp.exp(jnp.maximum(rel, 0.0)[None] * log_gamma[:, None, None]), 0.0)
    q_dec = jnp.exp((idx + 1.0)[None, :] * log_gamma[:, None])
    k_dec = jnp.exp((c - 1.0 - idx)[None, :] * log_gamma[:, None])
    c_dec = jnp.exp(c * log_gamma)
    cdec = jnp.broadcast_to(c_dec[:, None, None], (RET_HEADS, 1, RET_V))
    return inner, q_dec[:, :, None], k_dec[:, :, None], cdec


def _t5_bucket(n):
    max_exact = REL_BUCKETS // 2
    nf = jnp.maximum(n, 1).astype(F32)
    large = max_exact + (jnp.log(nf / max_exact) / math.log(REL_MAX_DIST / max_exact)
                         * (REL_BUCKETS - max_exact)).astype(I32)
    large = jnp.minimum(large, REL_BUCKETS - 1)
    return jnp.where(n < max_exact, n, large)


def _bias_tables(rel_bias, t):
    c = REL_MAX_DIST
    nb = t // c
    far = rel_bias[REL_BUCKETS - 1].astype(F32)
    d = jnp.arange(c)[None, :] - jnp.arange(c)[:, None]
    near0 = (jnp.take(rel_bias, _t5_bucket(jnp.maximum(d, 0)), axis=0).astype(F32) - far) * LOG2E
    near0 = jnp.where((d >= 0)[:, :, None], near0, MASK_VALUE)
    near1 = (jnp.take(rel_bias, _t5_bucket(d + c), axis=0).astype(F32) - far) * LOG2E
    tile0 = jnp.tile(near0, (nb, nb, 1))
    tile1 = jnp.tile(near1, (nb, nb, 1))
    kb = (jnp.arange(t) // c)[:, None, None]
    qb = (jnp.arange(t) // c)[None, :, None]
    diag = jnp.where(qb == kb, tile0,
                     jnp.where(qb == kb + 1, tile1, jnp.where(qb > kb, 0.0, MASK_VALUE)))
    left = jnp.where((qb == 0) & (kb == nb - 1), tile1, 0.0)
    return jnp.transpose(jnp.stack([diag, left], 0), (3, 0, 1, 2))


def kernel(x, p, ln_in_g, ln_in_b, rel_bias, w_in, ret_gn_g, lambda_q1, lambda_k1, lambda_q2,
           lambda_k2, diff_subln_g, w_ret_o, w_diff_o, w_o, ln1_g, ln1_b, router_w, router_bias,
           exp_w_gate, exp_w_up, exp_w_down, sh_w_gate, sh_w_up, sh_w_down, ple_w, ple_gate_w,
           ln2_g, ln2_b):
    batch, seq, d = x.shape
    n = batch * seq
    assert d == D_MODEL and seq % (ATTN_T * ATTN_KSUB) == 0 and seq % PROJ_TM == 0
    assert n % min(DEST_T, n) == 0 and n % COMB_T == 0 and n % POST_TM == 0
    x2 = x.reshape(n, d)
    row = lambda a: a.reshape(1, -1).astype(F32)

    cosf, sinf = _xpos_tables(seq)
    inner, qdec, kdec, cdec = _decay_tables()
    h, proj = _proj_call(x2, row(ln_in_g), row(ln_in_b), cosf, sinf, w_in[0].astype(BF16), seq)

    yr = _ret_call(proj, inner, qdec, kdec, cdec,
                   ret_gn_g[0].reshape(RET_HEADS, 1, RET_V).astype(F32), batch, seq)

    od = _attn_call(proj, _bias_tables(rel_bias, ATTN_T), row(lambda_q1[0]), row(lambda_k1[0]),
                    row(lambda_q2[0]), row(lambda_k2[0]),
                    diff_subln_g[0].reshape(-1, 1).astype(F32), batch, seq)

    h1, base, idx, w, rank, counts = _post_call(
        yr, od, proj, h, p[0].reshape(n, PLE_DIM),
        w_ret_o[0].astype(BF16), w_diff_o[0].astype(BF16), w_o[0].astype(BF16),
        row(ln1_g[0]), row(ln1_b[0]),
        sh_w_gate[0].astype(BF16), sh_w_up[0].astype(BF16), sh_w_down[0].astype(BF16),
        ple_w[0].astype(BF16), ple_gate_w[0].astype(BF16),
        router_w[0].T.astype(F32), router_bias[0].reshape(N_EXPERTS, 1).astype(F32))

    g = EXPERT_BLOCK
    counts = counts.reshape(N_EXPERTS)
    padded = (counts + g - 1) // g * g
    pend = jnp.cumsum(padded)
    pstart = pend - padded
    n_rows = -(-(n * TOP_K + N_EXPERTS * (g - 1)) // g) * g
    n_blocks = n_rows // g
    block_start = jnp.arange(n_blocks, dtype=I32) * g
    block_e = jnp.minimum(
        jnp.sum((pend[None, :] <= block_start[:, None]).astype(I32), axis=1),
        N_EXPERTS - 1).astype(I32)

    dest = _dest_call(idx, rank, pstart.astype(F32).reshape(N_EXPERTS, 1))
    tok = jnp.broadcast_to(jnp.arange(n, dtype=I32)[None, :], (TOP_K, n))
    row_tok = jnp.zeros((n_rows,), I32).at[dest.reshape(-1)].set(
        tok.reshape(-1), unique_indices=True, mode='drop')

    n_used = (pend[N_EXPERTS - 1:] // g).astype(I32)
    assert (n * TOP_K) // g >= GATHER_AHEAD
    y = _expert_call(block_e, n_used, row_tok.reshape(n_blocks, 1, g), h1,
                     exp_w_gate[0], exp_w_up[0], exp_w_down[0])

    dest3 = dest.reshape(TOP_K, n // COMB_T, COMB_T).transpose(1, 0, 2)
    out = _combine_call(dest3, y, w.T, base, row(ln2_g[0]), row(ln2_b[0]))
    return out.reshape(batch, seq, d)
```

```python
import functools
import math

import numpy as np
import jax
import jax.numpy as jnp
from jax import lax
from jax.experimental import pallas as pl
from jax.experimental.pallas import tpu as pltpu

F32 = jnp.float32
BF16 = jnp.bfloat16
I32 = jnp.int32

D_MODEL = 1024
PLE_DIM = 256
RET_HEADS = 8
RET_QK = 128
RET_V = 256
RET_CHUNK = 128
XPOS_BASE = 10000.0
DIFF_HEADS = 8
DIFF_HD = 64
REL_BUCKETS = 32
REL_MAX_DIST = 128
N_EXPERTS = 256
TOP_K = 8
N_GROUPS = 8
GROUP_SIZE = N_EXPERTS // N_GROUPS
TOPK_GROUPS = 4
D_EXPERT = 256
ROUTED_SCALE = 2.5
EXPERT_BLOCK = 256
DEPTH = 1
DN_ALPHA = (2.0 * DEPTH) ** 0.25
EPS = 1e-5
LAM_INIT = 0.8 - 0.6 * math.exp(-0.3 * 0)

SEG_QR, SEG_KR, SEG_VR, SEG_GR, SEG_QD, SEG_KD, SEG_VD, SEG_GATE = (
    0, 1024, 2048, 4096, 6144, 7168, 8192, 9216)
IN_COLS = 11264

LANES = 128
ROW_SUB = D_MODEL // LANES
PACK_SUB = ROW_SUB // 2
MASK_VALUE = -1e30
LOG2E = math.log2(math.e)
VMEM_LIMIT = 56 * 1024 * 1024

RET_SUB = 4
PROJ_TM = 1024
PROJ_TN = 1024
ATTN_T = 512
ATTN_KSUB = 2
ATTN_VT_ROWS = 2 * DIFF_HD + 16
POST_TM = 512
DEST_T = 2048
COMB_T = 32
DMA_QUEUES = 2
GATHER_DMA_PRIORITY = 1
GATHER_AHEAD = 3
GATHER_SLOTS = GATHER_AHEAD + 1


def _layer_norm(x, g, b):
    mu = jnp.mean(x, -1, keepdims=True)
    d = x - mu
    var = jnp.mean(d * d, -1, keepdims=True)
    return d * lax.rsqrt(var + EPS) * g + b


def _silu(x):
    return x * jax.nn.sigmoid(x)


def _dot(a, b):
    return jnp.dot(a, b, preferred_element_type=F32)


def _tree(op, items):
    items = list(items)
    while len(items) > 1:
        items = [op(items[i], items[i + 1]) if i + 1 < len(items) else items[i]
                 for i in range(0, len(items), 2)]
    return items[0]


def _col_block(first_row, rows, s, sub=ROW_SUB):
    return (pl.ds(first_row * sub + s, rows, stride=sub), slice(None))


def _one_row(ref, row, sub=ROW_SUB):
    if isinstance(row, int):
        return ref.at[pl.ds(row * sub, sub), :]
    return ref.at[pl.ds(pl.multiple_of(row * sub, sub), sub), :]


def _row_span(ref, first_row, rows, sub=ROW_SUB):
    return ref.at[pl.ds(first_row * sub, rows * sub), :]


def _pack_bf16_pairs(y):
    half = y.shape[1] // 2
    lo = lax.bitcast_convert_type(y[:, :half].astype(BF16).astype(F32), jnp.uint32)
    hi = lax.bitcast_convert_type(y[:, half:].astype(BF16).astype(F32), jnp.uint32)
    return (lo >> 16) | hi


def _unpack_bf16_pairs(u):
    lo = lax.bitcast_convert_type(u << 16, F32)
    hi = lax.bitcast_convert_type(u & jnp.uint32(0xFFFF0000), F32)
    return lo, hi


def _dot_nt(a, b):
    return lax.dot_general(a, b, (((1,), (1,)), ((), ())), preferred_element_type=F32)


def _proj_kernel(x_ref, g_ref, b_ref, cos_ref, sin_ref, w_ref, h_ref, o_ref, hb_ref):
    j = pl.program_id(1)

    @pl.when(j == 0)
    def _():
        h = _layer_norm(x_ref[...], g_ref[...], b_ref[...])
        h_ref[...] = h
        hb_ref[...] = h.astype(BF16)

    acc = _dot(hb_ref[...], w_ref[...])

    def rotate(scale):
        c = cos_ref[...]
        s = sin_ref[...]
        for hh in range(PROJ_TN // RET_QK):
            t = acc[:, hh * RET_QK:(hh + 1) * RET_QK]
            v = t * c + pltpu.roll(t, RET_QK // 2, 1) * s
            if scale != 1.0:
                v = v * scale
            o_ref[:, hh * RET_QK:(hh + 1) * RET_QK] = v.astype(BF16)

    @pl.when(j == SEG_QR // PROJ_TN)
    def _():
        rotate(1.0)

    @pl.when(j == SEG_KR // PROJ_TN)
    def _():
        rotate(RET_QK ** -0.5)

    is_gr = (j >= SEG_GR // PROJ_TN) & (j < SEG_QD // PROJ_TN)
    is_qd = j == SEG_QD // PROJ_TN
    is_gate = j >= SEG_GATE // PROJ_TN
    is_plain = ((j >= SEG_VR // PROJ_TN) & (j < SEG_GR // PROJ_TN)) | (
        (j >= SEG_KD // PROJ_TN) & (j < SEG_GATE // PROJ_TN))

    @pl.when(is_plain)
    def _():
        o_ref[...] = acc.astype(BF16)

    @pl.when(is_gr)
    def _():
        o_ref[...] = _silu(acc).astype(BF16)

    @pl.when(is_qd)
    def _():
        o_ref[...] = (acc * (DIFF_HD ** -0.5 * LOG2E)).astype(BF16)

    @pl.when(is_gate)
    def _():
        o_ref[...] = jax.nn.sigmoid(acc).astype(BF16)


def _proj_call(x2, ln_g, ln_b, cosf, sinf, w_bf, seq):
    n = x2.shape[0]
    tm, tn = PROJ_TM, PROJ_TN
    s_tiles = seq // tm
    return pl.pallas_call(
        _proj_kernel,
        name="proj",
        grid=(n // tm, IN_COLS // tn),
        in_specs=[
            pl.BlockSpec((tm, D_MODEL), lambda i, j: (i, 0)),
            pl.BlockSpec((1, D_MODEL), lambda i, j: (0, 0)),
            pl.BlockSpec((1, D_MODEL), lambda i, j: (0, 0)),
            pl.BlockSpec((tm, RET_QK), lambda i, j: (i % s_tiles, 0)),
            pl.BlockSpec((tm, RET_QK), lambda i, j: (i % s_tiles, 0)),
            pl.BlockSpec((D_MODEL, tn), lambda i, j: (0, j)),
        ],
        out_specs=[
            pl.BlockSpec((tm, D_MODEL), lambda i, j: (i, 0)),
            pl.BlockSpec((tm, tn), lambda i, j: (i, j)),
        ],
        out_shape=[
            jax.ShapeDtypeStruct((n, D_MODEL), F32),
            jax.ShapeDtypeStruct((n, IN_COLS), BF16),
        ],
        scratch_shapes=[pltpu.VMEM((tm, D_MODEL), BF16)],
        compiler_params=pltpu.CompilerParams(
            dimension_semantics=("arbitrary", "arbitrary"), vmem_limit_bytes=VMEM_LIMIT),
    )(x2, ln_g, ln_b, cosf, sinf, w_bf)


def _ret_kernel(q_ref, k_ref, v_ref, g_ref, inner_ref, qdec_ref, kdec_ref, cdec_ref, gn_ref,
                o_ref, state_ref):
    c = pl.program_id(2)

    @pl.when(c == 0)
    def _():
        state_ref[...] = jnp.zeros_like(state_ref)

    st = state_ref[...]
    for j in range(RET_SUB):
        rows = slice(j * RET_CHUNK, (j + 1) * RET_CHUNK)
        q = q_ref[rows, :]
        k = k_ref[rows, :]
        v = v_ref[rows, :]
        scores = _dot_nt(q, k) * inner_ref[0]
        o = _dot(scores.astype(BF16), v) + qdec_ref[0] * _dot(q, st.astype(BF16))
        kd = (k.astype(F32) * kdec_ref[0]).astype(BF16)
        kv = lax.dot_general(kd, v, (((0,), (0,)), ((), ())), preferred_element_type=F32)
        st = cdec_ref[0] * st + kv

        mu = jnp.mean(o, -1, keepdims=True)
        d = o - mu
        var = jnp.mean(d * d, -1, keepdims=True)
        on = d * lax.rsqrt(var + EPS) * gn_ref[0]
        o_ref[rows, :] = (g_ref[rows, :].astype(F32) * on).astype(BF16)
    state_ref[...] = st


def _ret_call(proj, inner, qdec, kdec, cdec, gn, batch, seq):
    n = proj.shape[0]
    c = RET_CHUNK
    rows = RET_SUB * c
    nc = seq // rows
    qb = SEG_QR // RET_QK
    kb = SEG_KR // RET_QK
    vb = SEG_VR // RET_V
    gb = SEG_GR // RET_V
    return pl.pallas_call(
        _ret_kernel,
        name="retention",
        grid=(batch, RET_HEADS, nc),
        in_specs=[
            pl.BlockSpec((rows, RET_QK), lambda b, h, i: (b * nc + i, qb + h)),
            pl.BlockSpec((rows, RET_QK), lambda b, h, i: (b * nc + i, kb + h)),
            pl.BlockSpec((rows, RET_V), lambda b, h, i: (b * nc + i, vb + h)),
            pl.BlockSpec((rows, RET_V), lambda b, h, i: (b * nc + i, gb + h)),
            pl.BlockSpec((1, c, c), lambda b, h, i: (h, 0, 0)),
            pl.BlockSpec((1, c, 1), lambda b, h, i: (h, 0, 0)),
            pl.BlockSpec((1, c, 1), lambda b, h, i: (h, 0, 0)),
            pl.BlockSpec((1, 1, RET_V), lambda b, h, i: (h, 0, 0)),
            pl.BlockSpec((1, 1, RET_V), lambda b, h, i: (h, 0, 0)),
        ],
        out_specs=pl.BlockSpec((rows, RET_V), lambda b, h, i: (b * nc + i, h)),
        out_shape=jax.ShapeDtypeStruct((n, RET_HEADS * RET_V), BF16),
        scratch_shapes=[pltpu.VMEM((RET_QK, RET_V), F32)],
        compiler_params=pltpu.CompilerParams(
            dimension_semantics=("arbitrary", "arbitrary", "arbitrary"),
            vmem_limit_bytes=VMEM_LIMIT),
    )(proj, proj, proj, proj, inner, qdec, kdec, cdec, gn)


def _attn_kernel(qi_ref, ki_ref, q_ref, k_ref, vt_ref, tab_ref, lq1_ref, lk1_ref, lq2_ref,
                 lk2_ref, sg_ref, o_ref, qst_ref, m_ref, acc_ref, s_ref, p_ref, mx_ref):
    t = ATTN_T
    hd2 = 2 * DIFF_HD
    p = pl.program_id(2)
    qi = qi_ref[p]
    ki = ki_ref[p]

    @pl.when(ki == 0)
    def _():
        qt = q_ref[...].astype(F32).T
        sub = lax.broadcasted_iota(I32, qt.shape, 0)
        qst_ref[:, 0:t] = jnp.where(sub < DIFF_HD, qt, 0.0).astype(BF16)
        qst_ref[:, t:2 * t] = jnp.where(sub >= DIFF_HD, qt, 0.0).astype(BF16)
        m_ref[...] = jnp.full_like(m_ref, MASK_VALUE)
        acc_ref[...] = jnp.zeros_like(acc_ref)

    def steps(*biases):
        for sub, bias in enumerate(biases):
            kk = k_ref[sub * t:(sub + 1) * t, :]
            for half in range(2):
                cols = slice(half * t, (half + 1) * t)
                s = _dot(kk, qst_ref[:, cols])
                if bias is not None:
                    s = s + tab_ref[0, bias]
                s_ref[sub, half] = s
                mx_ref[sub, half] = _tree(jnp.maximum, [s[r:r + 8, :] for r in range(0, t, 8)])
        for sub in range(len(biases)):
            vt = vt_ref[:, sub * t:(sub + 1) * t]
            for half in range(2):
                cols = slice(half * t, (half + 1) * t)
                m_prev = m_ref[:, cols]
                m_new = jnp.maximum(m_prev, jnp.max(mx_ref[sub, half], 0, keepdims=True))
                alpha = jnp.exp2(m_prev - m_new)
                for r in range(0, t, 64):
                    p_ref[sub, half, r:r + 64, :] = jnp.exp2(
                        s_ref[sub, half, r:r + 64, :] - m_new).astype(BF16)
                acc_ref[:, cols] = alpha * acc_ref[:, cols] + _dot(vt, p_ref[sub, half])
                m_ref[:, cols] = m_new

    def finalize():
        lam = (jnp.exp(jnp.sum(lq1_ref[...] * lk1_ref[...], -1, keepdims=True))
               - jnp.exp(jnp.sum(lq2_ref[...] * lk2_ref[...], -1, keepdims=True)) + LAM_INIT)
        o1 = acc_ref[0:hd2, 0:t] / acc_ref[hd2:hd2 + 1, 0:t]
        o2 = acc_ref[0:hd2, t:2 * t] / acc_ref[hd2:hd2 + 1, t:2 * t]
        o = o1 - lam * o2
        o = o * lax.rsqrt(jnp.mean(o * o, 0, keepdims=True) + EPS) * (sg_ref[...] * (1.0 - LAM_INIT))
        o_ref[...] = o.T.astype(BF16)

    assert ATTN_KSUB == 2
    last = ki * ATTN_KSUB + 1
    pl.when(last < qi - 1)(functools.partial(steps, None, None))
    pl.when(last == qi - 1)(functools.partial(steps, None, 1))

    @pl.when(last == qi)
    def _():
        steps(1, 0)
        finalize()

    @pl.when(last == qi + 1)
    def _():
        steps(0)
        finalize()


def _attn_call(proj, tab, lq1, lk1, lq2, lk2, sg, batch, seq):
    n = proj.shape[0]
    t = ATTN_T
    nq = seq // t
    tk = ATTN_KSUB * t
    nk = seq // tk
    pairs = [(a, b) for a in range(nq) for b in range(a // ATTN_KSUB + 1)]
    qi_tab = jnp.asarray(np.array([a for a, _ in pairs], np.int32))
    ki_tab = jnp.asarray(np.array([b for _, b in pairs], np.int32))
    hd2 = 2 * DIFF_HD
    qb, kb = SEG_QD // hd2, SEG_KD // hd2
    v = proj[:, SEG_VD:SEG_VD + DIFF_HEADS * hd2].reshape(batch, seq, DIFF_HEADS, hd2)
    vt = jnp.transpose(v, (0, 2, 3, 1))
    extra = jnp.zeros((batch, DIFF_HEADS, ATTN_VT_ROWS - hd2, seq), BF16).at[:, :, 0, :].set(1.0)
    vt = jnp.concatenate([vt, extra], axis=2).reshape(batch * DIFF_HEADS * ATTN_VT_ROWS, seq)
    vec = pl.BlockSpec((1, DIFF_HD), lambda b, h, p, qi, ki: (0, 0))
    grid_spec = pltpu.PrefetchScalarGridSpec(
        num_scalar_prefetch=2,
        grid=(batch, DIFF_HEADS, len(pairs)),
        in_specs=[
            pl.BlockSpec((t, hd2), lambda b, h, p, qi, ki: (b * nq + qi[p], qb + h)),
            pl.BlockSpec((tk, hd2), lambda b, h, p, qi, ki: (b * nk + ki[p], kb + h)),
            pl.BlockSpec((ATTN_VT_ROWS, tk), lambda b, h, p, qi, ki: (b * DIFF_HEADS + h, ki[p])),
            pl.BlockSpec((1, 2, t, t), lambda b, h, p, qi, ki: (h, 0, 0, 0)),
            vec, vec, vec, vec,
            pl.BlockSpec((hd2, 1), lambda b, h, p, qi, ki: (0, 0)),
        ],
        out_specs=pl.BlockSpec((t, hd2), lambda b, h, p, qi, ki: (b * nq + qi[p], h)),
        scratch_shapes=[
            pltpu.VMEM((hd2, 2 * t), BF16),
            pltpu.VMEM((1, 2 * t), F32),
            pltpu.VMEM((ATTN_VT_ROWS, 2 * t), F32),
            pltpu.VMEM((ATTN_KSUB, 2, t, t), F32),
            pltpu.VMEM((ATTN_KSUB, 2, t, t), BF16),
            pltpu.VMEM((ATTN_KSUB, 2, 8, t), F32),
        ],
    )
    return pl.pallas_call(
        _attn_kernel,
        name="diffattn",
        grid_spec=grid_spec,
        out_shape=jax.ShapeDtypeStruct((n, DIFF_HEADS * hd2), BF16),
        compiler_params=pltpu.CompilerParams(
            dimension_semantics=("arbitrary", "arbitrary", "arbitrary"),
            vmem_limit_bytes=VMEM_LIMIT),
    )(qi_tab, ki_tab, proj, proj, vt, tab, lq1, lk1, lq2, lk2, sg)


def _route(scores, bias, carry):
    e, tm = scores.shape
    neg = -jnp.inf
    choice = scores + bias
    row_g = lax.broadcasted_iota(I32, (GROUP_SIZE, tm), 0)
    group_scores = []
    for g in range(N_GROUPS):
        blk = choice[g * GROUP_SIZE:(g + 1) * GROUP_SIZE]
        m1 = jnp.max(blk, 0, keepdims=True)
        i1 = jnp.min(jnp.where(blk == m1, row_g, GROUP_SIZE), 0, keepdims=True)
        m2 = jnp.max(jnp.where(row_g == i1, neg, blk), 0, keepdims=True)
        group_scores.append(m1 + m2)
    gsc = jnp.concatenate(group_scores, 0)
    row8 = lax.broadcasted_iota(I32, (N_GROUPS, tm), 0)
    okf = jnp.zeros((N_GROUPS, tm), F32)
    for _ in range(TOPK_GROUPS):
        m = jnp.max(gsc, 0, keepdims=True)
        i = jnp.min(jnp.where(gsc == m, row8, N_GROUPS), 0, keepdims=True)
        hit = row8 == i
        okf = jnp.where(hit, 1.0, okf)
        gsc = jnp.where(hit, neg, gsc)
    masked = jnp.concatenate(
        [jnp.where(jnp.broadcast_to(okf[g:g + 1], (GROUP_SIZE, tm)) > 0.0,
                   choice[g * GROUP_SIZE:(g + 1) * GROUP_SIZE], neg)
         for g in range(N_GROUPS)], 0)
    row = lax.broadcasted_iota(I32, (e, tm), 0)
    idxs, ws = [], []
    multi = jnp.zeros((e, tm), F32)
    for _ in range(TOP_K):
        m = jnp.max(masked, 0, keepdims=True)
        i = jnp.min(jnp.where(masked == m, row, e), 0, keepdims=True)
        hit = row == i
        ws.append(jnp.sum(jnp.where(hit, scores, 0.0), 0, keepdims=True))
        idxs.append(i)
        multi = jnp.where(hit, 1.0, multi)
        masked = jnp.where(hit, neg, masked)
    a = lax.broadcasted_iota(I32, (tm, tm), 0)
    b = lax.broadcasted_iota(I32, (tm, tm), 1)
    upper = jnp.where(a < b, 1.0, 0.0).astype(BF16)
    before = _dot(multi.astype(BF16), upper) + carry
    ranks = [jnp.sum(jnp.where(row == i, before, 0.0), 0, keepdims=True).astype(I32) for i in idxs]
    new_carry = carry + jnp.sum(multi, 1, keepdims=True)
    wsum = ws[0]
    for w in ws[1:]:
        wsum = wsum + w
    ws = [w / wsum * ROUTED_SCALE for w in ws]
    return idxs, ws, ranks, new_carry


def _post_kernel(yr_ref, od_ref, g0_ref, g1_ref, h_ref, p_ref, wr_ref, wd_ref, wo_ref, l1g_ref,
                 l1b_ref, sg_ref, su_ref, sd_ref, pw_ref, pg_ref, rwt_ref, rb_ref,
                 h1_ref, base_ref, idx_ref, w_ref, rank_ref, cnt_ref, carry_ref):
    i = pl.program_id(0)

    @pl.when(i == 0)
    def _():
        carry_ref[...] = jnp.zeros_like(carry_ref)

    y_r = _dot(yr_ref[...], wr_ref[...])
    y_d = _dot(od_ref[...], wd_ref[...])
    merged = g0_ref[...].astype(F32) * y_r + g1_ref[...].astype(F32) * y_d
    mix = _dot(merged.astype(BF16), wo_ref[...])
    h1 = _layer_norm(DN_ALPHA * h_ref[...] + mix, l1g_ref[...], l1b_ref[...])
    for s in range(ROW_SUB):
        h1_ref[_col_block(0, h1.shape[0], s)] = h1[:, s * LANES:(s + 1) * LANES]
    hb = h1.astype(BF16)

    shared = _dot((_silu(_dot(hb, sg_ref[...])) * _dot(hb, su_ref[...])).astype(BF16), sd_ref[...])
    ple = _dot(p_ref[...].astype(BF16), pw_ref[...]) * jax.nn.sigmoid(_dot(hb, pg_ref[...]))
    base_ref[...] = DN_ALPHA * h1 + shared + ple

    h_lo = (h1 - hb.astype(F32)).astype(BF16)
    rw = rwt_ref[...]
    rw_hi = rw.astype(BF16)
    rw_lo = (rw - rw_hi.astype(F32)).astype(BF16)
    logits = _dot_nt(rw_hi, hb) + (_dot_nt(rw_hi, h_lo) + _dot_nt(rw_lo, hb))
    scores = jax.nn.sigmoid(logits)
    idxs, ws, ranks, new_carry = _route(scores, rb_ref[...], carry_ref[...])
    idx_ref[...] = jnp.concatenate(idxs, 0)
    w_ref[...] = jnp.concatenate(ws, 0)
    rank_ref[...] = jnp.concatenate(ranks, 0)
    carry_ref[...] = new_carry
    cnt_ref[...] = new_carry.astype(I32)


def _post_call(yr, od, proj, h, p2, wr, wd, wo, l1g, l1b, sg, su, sd, pw, pg, rwt, rb):
    n = h.shape[0]
    tm = POST_TM
    g0b = SEG_GATE // D_MODEL

    def full(a):
        return pl.BlockSpec(a.shape, lambda i: (0,) * a.ndim)

    row = lambda w: pl.BlockSpec((tm, w), lambda i: (i, 0))
    tok = lambda: pl.BlockSpec((TOP_K, tm), lambda i: (0, i))
    return pl.pallas_call(
        _post_kernel,
        name="post",
        grid=(n // tm,),
        in_specs=[
            row(RET_HEADS * RET_V), row(D_MODEL),
            pl.BlockSpec((tm, D_MODEL), lambda i: (i, g0b)),
            pl.BlockSpec((tm, D_MODEL), lambda i: (i, g0b + 1)),
            row(D_MODEL), row(PLE_DIM),
            full(wr), full(wd), full(wo), full(l1g), full(l1b), full(sg), full(su), full(sd),
            full(pw), full(pg), full(rwt), full(rb),
        ],
        out_specs=[pl.BlockSpec((tm * ROW_SUB, LANES), lambda i: (i, 0)), row(D_MODEL),
                   tok(), tok(), tok(), pl.BlockSpec((N_EXPERTS, 1), lambda i: (0, 0))],
        out_shape=[
            jax.ShapeDtypeStruct((n * ROW_SUB, LANES), F32),
            jax.ShapeDtypeStruct((n, D_MODEL), F32),
            jax.ShapeDtypeStruct((TOP_K, n), I32),
            jax.ShapeDtypeStruct((TOP_K, n), F32),
            jax.ShapeDtypeStruct((TOP_K, n), I32),
            jax.ShapeDtypeStruct((N_EXPERTS, 1), I32),
        ],
        scratch_shapes=[pltpu.VMEM((N_EXPERTS, 1), F32)],
        compiler_params=pltpu.CompilerParams(
            dimension_semantics=("arbitrary",), vmem_limit_bytes=VMEM_LIMIT),
    )(yr, od, proj, proj, h, p2, wr, wd, wo, l1g, l1b, sg, su, sd, pw, pg, rwt, rb)


def _dest_kernel(idx_ref, rank_ref, ps_ref, o_ref):
    idx = idx_ref[...]
    ps = ps_ref[...]
    td = idx.shape[1]
    row = lax.broadcasted_iota(I32, (N_EXPERTS, td), 0)
    outs = []
    for k in range(TOP_K):
        sel = jnp.sum(jnp.where(row == idx[k:k + 1], ps, 0.0), 0, keepdims=True)
        outs.append(sel.astype(I32))
    o_ref[...] = jnp.concatenate(outs, 0) + rank_ref[...]


def _dest_call(idx, rank, pstart_f):
    n = idx.shape[1]
    td = min(DEST_T, n)
    spec = pl.BlockSpec((TOP_K, td), lambda i: (0, i))
    return pl.pallas_call(
        _dest_kernel,
        name="dest",
        grid=(n // td,),
        in_specs=[spec, spec, pl.BlockSpec((N_EXPERTS, 1), lambda i: (0, 0))],
        out_specs=spec,
        out_shape=jax.ShapeDtypeStruct((TOP_K, n), I32),
        compiler_params=pltpu.CompilerParams(dimension_semantics=("arbitrary",)),
    )(idx, rank, pstart_f)


def _expert_kernel(be_ref, nu_ref, rt_ref, rtn_ref, h_hbm, wg_ref, wu_ref, wd_ref, y_ref,
                   buf_ref, wgb_ref, wub_ref, wdb_ref, sem):
    i = pl.program_id(0)
    nb = pl.num_programs(0)
    g = EXPERT_BLOCK

    def wait(s):
        pltpu.make_async_copy(_row_span(h_hbm, 0, g), _row_span(buf_ref, s * g, g), sem.at[s]).wait()

    @pl.when(i == 0)
    def _():
        for j in range(GATHER_AHEAD):
            def body(r, c, j=j):
                pltpu.make_async_copy(_one_row(h_hbm, rt_ref[j, 0, r]),
                                      _one_row(buf_ref, j * g + r), sem.at[j]).start(priority=GATHER_DMA_PRIORITY)
                return c
            lax.fori_loop(0, g, body, 0, unroll=8)

    changed = jnp.logical_or(i == 0, be_ref[i] != be_ref[jnp.maximum(i - 1, 0)])

    @pl.when(changed)
    def _():
        wgb_ref[...] = wg_ref[0].astype(BF16)
        wub_ref[...] = wu_ref[0].astype(BF16)
        wdb_ref[...] = wd_ref[0].astype(BF16)

    def block(slot, more):
        wait(slot)
        x = jnp.concatenate([buf_ref[_col_block(slot * g, g, s)] for s in range(ROW_SUB)],
                            axis=1).astype(BF16)
        if more:
            ahead = (slot + GATHER_AHEAD) % GATHER_SLOTS
            for r in range(g):
                pltpu.make_async_copy(
                    _one_row(h_hbm, rtn_ref[0, 0, r]), _one_row(buf_ref, ahead * g + r),
                    sem.at[ahead]).start(priority=GATHER_DMA_PRIORITY)
        act = _silu(_dot(x, wgb_ref[...])) * _dot(x, wub_ref[...])
        packed = _pack_bf16_pairs(_dot(act.astype(BF16), wdb_ref[...]))
        for s in range(PACK_SUB):
            y_ref[_col_block(0, g, s, PACK_SUB)] = packed[:, s * LANES:(s + 1) * LANES]

    used = i < nu_ref[0]
    more = i + GATHER_AHEAD < nu_ref[0]
    for slot in range(GATHER_SLOTS):
        in_slot = jnp.logical_and(i % GATHER_SLOTS == slot, used)
        pl.when(jnp.logical_and(in_slot, more))(functools.partial(block, slot, True))
        pl.when(jnp.logical_and(in_slot, jnp.logical_not(more)))(
            functools.partial(block, slot, False))

    @pl.when(jnp.logical_not(used))
    def _():
        y_ref[...] = jnp.zeros_like(y_ref)


def _expert_call(block_e, n_used, row_tok3, h1, wg, wu, wd):
    nb = block_e.shape[0]
    g = EXPERT_BLOCK
    grid_spec = pltpu.PrefetchScalarGridSpec(
        num_scalar_prefetch=2,
        grid=(nb,),
        in_specs=[
            pl.BlockSpec((GATHER_AHEAD, 1, g), lambda i, be, nu: (0, 0, 0),
                         memory_space=pltpu.SMEM),
            pl.BlockSpec((1, 1, g),
                         lambda i, be, nu: (jnp.minimum(i + GATHER_AHEAD, nb - 1), 0, 0),
                         memory_space=pltpu.SMEM),
            pl.BlockSpec(memory_space=pl.ANY),
            pl.BlockSpec((1, D_MODEL, D_EXPERT), lambda i, be, nu: (be[i], 0, 0)),
            pl.BlockSpec((1, D_MODEL, D_EXPERT), lambda i, be, nu: (be[i], 0, 0)),
            pl.BlockSpec((1, D_EXPERT, D_MODEL), lambda i, be, nu: (be[i], 0, 0)),
        ],
        out_specs=pl.BlockSpec((g * PACK_SUB, LANES), lambda i, be, nu: (i, 0)),
        scratch_shapes=[
            pltpu.VMEM((GATHER_SLOTS * g * ROW_SUB, LANES), F32),
            pltpu.VMEM((D_MODEL, D_EXPERT), BF16),
            pltpu.VMEM((D_MODEL, D_EXPERT), BF16),
            pltpu.VMEM((D_EXPERT, D_MODEL), BF16),
            pltpu.SemaphoreType.DMA((GATHER_SLOTS,)),
        ],
    )
    assert nb > GATHER_AHEAD
    return pl.pallas_call(
        _expert_kernel,
        name="experts",
        grid_spec=grid_spec,
        out_shape=jax.ShapeDtypeStruct((nb * g * PACK_SUB, LANES), jnp.uint32),
        compiler_params=pltpu.CompilerParams(
            dimension_semantics=("arbitrary",), vmem_limit_bytes=VMEM_LIMIT),
    )(block_e, n_used, row_tok3, row_tok3, h1, wg, wu, wd)


def _combine_kernel(d_ref, dn_ref, y_hbm, w_ref, base_ref, g_ref, b_ref, o_ref, buf_ref, sem):
    i = pl.program_id(0)
    nt = pl.num_programs(0)
    tt = COMB_T

    def plane(s, k):
        return (s * TOP_K + k) * tt

    def wait(s):
        for k in range(TOP_K):
            pltpu.make_async_copy(_row_span(y_hbm, 0, tt, PACK_SUB),
                                  _row_span(buf_ref, plane(s, k), tt, PACK_SUB), sem.at[s]).wait()

    @pl.when(i == 0)
    def _():
        for j in range(GATHER_AHEAD):
            def body(t, c, j=j):
                for k in range(TOP_K):
                    pltpu.make_async_copy(
                        _one_row(y_hbm, d_ref[j, k, t], PACK_SUB),
                        _one_row(buf_ref, plane(j, k) + t, PACK_SUB),
                        sem.at[j]).start(priority=k % DMA_QUEUES)
                return c
            lax.fori_loop(0, tt, body, 0)

    def tile(slot):
        wait(slot)
        ahead = (slot + GATHER_AHEAD) % GATHER_SLOTS
        for t in range(tt):
            for k in range(TOP_K):
                pltpu.make_async_copy(_one_row(y_hbm, dn_ref[0, k, t], PACK_SUB),
                                      _one_row(buf_ref, plane(ahead, k) + t, PACK_SUB),
                                      sem.at[ahead]).start(priority=k % DMA_QUEUES)
        w = w_ref[...]
        lows, highs = [], []
        for s in range(PACK_SUB):
            lo_acc = hi_acc = None
            for k in range(TOP_K):
                lo, hi = _unpack_bf16_pairs(
                    buf_ref[_col_block(plane(slot, k), tt, s, PACK_SUB)])
                wk = w[:, k:k + 1]
                lo_acc = wk * lo if lo_acc is None else lo_acc + wk * lo
                hi_acc = wk * hi if hi_acc is None else hi_acc + wk * hi
            lows.append(lo_acc)
            highs.append(hi_acc)
        routed = jnp.concatenate(lows + highs, axis=1)
        o_ref[...] = _layer_norm(base_ref[...] + routed, g_ref[...], b_ref[...])

        @pl.when(i == nt - 1)
        def _():
            for j in range(1, GATHER_AHEAD + 1):
                wait((slot + j) % GATHER_SLOTS)

    for slot in range(GATHER_SLOTS):
        pl.when(i % GATHER_SLOTS == slot)(functools.partial(tile, slot))


def _combine_call(dest3, y, w_t, base, ln_g, ln_b):
    n = base.shape[0]
    tt = COMB_T
    nt = n // tt
    return pl.pallas_call(
        _combine_kernel,
        name="combine",
        grid=(nt,),
        in_specs=[
            pl.BlockSpec((GATHER_AHEAD, TOP_K, tt), lambda i: (0, 0, 0), memory_space=pltpu.SMEM),
            pl.BlockSpec((1, TOP_K, tt), lambda i: (jnp.minimum(i + GATHER_AHEAD, nt - 1), 0, 0),
                         memory_space=pltpu.SMEM),
            pl.BlockSpec(memory_space=pl.ANY),
            pl.BlockSpec((tt, TOP_K), lambda i: (i, 0)),
            pl.BlockSpec((tt, D_MODEL), lambda i: (i, 0)),
            pl.BlockSpec((1, D_MODEL), lambda i: (0, 0)),
            pl.BlockSpec((1, D_MODEL), lambda i: (0, 0)),
        ],
        out_specs=pl.BlockSpec((tt, D_MODEL), lambda i: (i, 0)),
        out_shape=jax.ShapeDtypeStruct((n, D_MODEL), F32),
        scratch_shapes=[
            pltpu.VMEM((GATHER_SLOTS * TOP_K * tt * PACK_SUB, LANES), jnp.uint32),
            pltpu.SemaphoreType.DMA((GATHER_SLOTS,)),
        ],
        compiler_params=pltpu.CompilerParams(
            dimension_semantics=("arbitrary",), vmem_limit_bytes=VMEM_LIMIT),
    )(dest3, dest3, y, w_t, base, ln_g, ln_b)


def _xpos_tables(seq):
    half = RET_QK // 2
    pos = jnp.arange(seq, dtype=F32)
    theta = 1.0 / (XPOS_BASE ** jnp.linspace(0.0, 1.0, half, dtype=F32))
    ang = pos[:, None] * theta[None, :]
    cos = jnp.cos(ang)
    sin = jnp.sin(ang)
    return jnp.concatenate([cos, cos], -1), jnp.concatenate([-sin, sin], -1)


def _decay_tables():
    c = RET_CHUNK
    log_gamma = jnp.log(1.0 - 2.0 ** (-5.0 - jnp.arange(RET_HEADS, dtype=F32)))
    idx = jnp.arange(c, dtype=F32)
    rel = idx[:, None] - idx[None, :]
    inner = jnp.where(rel >= 0, jnp.exp(jnp.maximum(rel, 0.0)[None] * log_gamma[:, None, None]), 0.0)
    q_dec = jnp.exp((idx + 1.0)[None, :] * log_gamma[:, None])
    k_dec = jnp.exp((c - 1.0 - idx)[None, :] * log_gamma[:, None])
    c_dec = jnp.exp(c * log_gamma)
    cdec = jnp.broadcast_to(c_dec[:, None, None], (RET_HEADS, 1, RET_V))
    return inner, q_dec[:, :, None], k_dec[:, :, None], cdec


def _t5_bucket(n):
    max_exact = REL_BUCKETS // 2
    nf = jnp.maximum(n, 1).astype(F32)
    large = max_exact + (jnp.log(nf / max_exact) / math.log(REL_MAX_DIST / max_exact)
                         * (REL_BUCKETS - max_exact)).astype(I32)
    large = jnp.minimum(large, REL_BUCKETS - 1)
    return jnp.where(n < max_exact, n, large)


def _bias_tables(rel_bias, t):
    c = REL_MAX_DIST
    nb = t // c
    far = rel_bias[REL_BUCKETS - 1].astype(F32)
    d = jnp.arange(c)[None, :] - jnp.arange(c)[:, None]
    near0 = (jnp.take(rel_bias, _t5_bucket(jnp.maximum(d, 0)), axis=0).astype(F32) - far) * LOG2E
    near0 = jnp.where((d >= 0)[:, :, None], near0, MASK_VALUE)
    near1 = (jnp.take(rel_bias, _t5_bucket(d + c), axis=0).astype(F32) - far) * LOG2E
    tile0 = jnp.tile(near0, (nb, nb, 1))
    tile1 = jnp.tile(near1, (nb, nb, 1))
    kb = (jnp.arange(t) // c)[:, None, None]
    qb = (jnp.arange(t) // c)[None, :, None]
    diag = jnp.where(qb == kb, tile0,
                     jnp.where(qb == kb + 1, tile1, jnp.where(qb > kb, 0.0, MASK_VALUE)))
    left = jnp.where((qb == 0) & (kb == nb - 1), tile1, 0.0)
    return jnp.transpose(jnp.stack([diag, left], 0), (3, 0, 1, 2))


def kernel(x, p, ln_in_g, ln_in_b, rel_bias, w_in, ret_gn_g, lambda_q1, lambda_k1, lambda_q2,
           lambda_k2, diff_subln_g, w_ret_o, w_diff_o, w_o, ln1_g, ln1_b, router_w, router_bias,
           exp_w_gate, exp_w_up, exp_w_down, sh_w_gate, sh_w_up, sh_w_down, ple_w, ple_gate_w,
           ln2_g, ln2_b):
    batch, seq, d = x.shape
    n = batch * seq
    assert d == D_MODEL and seq % (ATTN_T * ATTN_KSUB) == 0 and seq % PROJ_TM == 0
    assert n % min(DEST_T, n) == 0 and n % COMB_T == 0 and n % POST_TM == 0
    x2 = x.reshape(n, d)
    row = lambda a: a.reshape(1, -1).astype(F32)

    cosf, sinf = _xpos_tables(seq)
    inner, qdec, kdec, cdec = _decay_tables()
    h, proj = _proj_call(x2, row(ln_in_g), row(ln_in_b), cosf, sinf, w_in[0].astype(BF16), seq)

    yr = _ret_call(proj, inner, qdec, kdec, cdec,
                   ret_gn_g[0].reshape(RET_HEADS, 1, RET_V).astype(F32), batch, seq)

    od = _attn_call(proj, _bias_tables(rel_bias, ATTN_T), row(lambda_q1[0]), row(lambda_k1[0]),
                    row(lambda_q2[0]), row(lambda_k2[0]),
                    diff_subln_g[0].reshape(-1, 1).astype(F32), batch, seq)

    h1, base, idx, w, rank, counts = _post_call(
        yr, od, proj, h, p[0].reshape(n, PLE_DIM),
        w_ret_o[0].astype(BF16), w_diff_o[0].astype(BF16), w_o[0].astype(BF16),
        row(ln1_g[0]), row(ln1_b[0]),
        sh_w_gate[0].astype(BF16), sh_w_up[0].astype(BF16), sh_w_down[0].astype(BF16),
        ple_w[0].astype(BF16), ple_gate_w[0].astype(BF16),
        router_w[0].T.astype(F32), router_bias[0].reshape(N_EXPERTS, 1).astype(F32))

    g = EXPERT_BLOCK
    counts = counts.reshape(N_EXPERTS)
    padded = (counts + g - 1) // g * g
    pend = jnp.cumsum(padded)
    pstart = pend - padded
    n_rows = -(-(n * TOP_K + N_EXPERTS * (g - 1)) // g) * g
    n_blocks = n_rows // g
    block_start = jnp.arange(n_blocks, dtype=I32) * g
    block_e = jnp.minimum(
        jnp.sum((pend[None, :] <= block_start[:, None]).astype(I32), axis=1),
        N_EXPERTS - 1).astype(I32)

    dest = _dest_call(idx, rank, pstart.astype(F32).reshape(N_EXPERTS, 1))
    tok = jnp.broadcast_to(jnp.arange(n, dtype=I32)[None, :], (TOP_K, n))
    row_tok = jnp.zeros((n_rows,), I32).at[dest.reshape(-1)].set(
        tok.reshape(-1), unique_indices=True, mode='drop')

    n_used = (pend[N_EXPERTS - 1:] // g).astype(I32)
    assert (n * TOP_K) // g >= GATHER_AHEAD
    y = _expert_call(block_e, n_used, row_tok.reshape(n_blocks, 1, g), h1,
                     exp_w_gate[0], exp_w_up[0], exp_w_down[0])

    dest3 = dest.reshape(TOP_K, n // COMB_T, COMB_T).transpose(1, 0, 2)
    out = _combine_call(dest3, y, w.T, base, row(ln2_g[0]), row(ln2_b[0]))
    return out.reshape(batch, seq, d)
```

```python
import functools
import math

import numpy as np
import jax
import jax.numpy as jnp
from jax import lax
from jax.experimental import pallas as pl
from jax.experimental.pallas import tpu as pltpu

F32 = jnp.float32
BF16 = jnp.bfloat16
I32 = jnp.int32

D_MODEL = 1024
PLE_DIM = 256
RET_HEADS = 8
RET_QK = 128
RET_V = 256
RET_CHUNK = 128
XPOS_BASE = 10000.0
DIFF_HEADS = 8
DIFF_HD = 64
REL_BUCKETS = 32
REL_MAX_DIST = 128
N_EXPERTS = 256
TOP_K = 8
N_GROUPS = 8
GROUP_SIZE = N_EXPERTS // N_GROUPS
TOPK_GROUPS = 4
D_EXPERT = 256
ROUTED_SCALE = 2.5
EXPERT_BLOCK = 256
DEPTH = 1
DN_ALPHA = (2.0 * DEPTH) ** 0.25
EPS = 1e-5
LAM_INIT = 0.8 - 0.6 * math.exp(-0.3 * 0)

SEG_QR, SEG_KR, SEG_VR, SEG_GR, SEG_QD, SEG_KD, SEG_VD, SEG_GATE = (
    0, 1024, 2048, 4096, 6144, 7168, 8192, 9216)
IN_COLS = 11264

LANES = 128
ROW_SUB = D_MODEL // LANES
PACK_SUB = ROW_SUB // 2
MASK_VALUE = -1e30
LOG2E = math.log2(math.e)
VMEM_LIMIT = 56 * 1024 * 1024

RET_SUB = 4
PROJ_TM = 1024
PROJ_TN = 1024
ATTN_T = 512
ATTN_KSUB = 2
ATTN_VT_ROWS = 2 * DIFF_HD + 16
POST_TM = 512
DEST_T = 2048
COMB_T = 32
DMA_QUEUES = 2
GATHER_DMA_PRIORITY = 1
GATHER_AHEAD = 3
GATHER_SLOTS = GATHER_AHEAD + 1


def _layer_norm(x, g, b):
    mu = jnp.mean(x, -1, keepdims=True)
    d = x - mu
    var = jnp.mean(d * d, -1, keepdims=True)
    return d * lax.rsqrt(var + EPS) * g + b


def _silu(x):
    return x * jax.nn.sigmoid(x)


def _dot(a, b):
    return jnp.dot(a, b, preferred_element_type=F32)


def _tree(op, items):
    items = list(items)
    while len(items) > 1:
        items = [op(items[i], items[i + 1]) if i + 1 < len(items) else items[i]
                 for i in range(0, len(items), 2)]
    return items[0]


def _col_block(first_row, rows, s, sub=ROW_SUB):
    return (pl.ds(first_row * sub + s, rows, stride=sub), slice(None))


def _one_row(ref, row, sub=ROW_SUB):
    if isinstance(row, int):
        return ref.at[pl.ds(row * sub, sub), :]
    return ref.at[pl.ds(pl.multiple_of(row * sub, sub), sub), :]


def _row_span(ref, first_row, rows, sub=ROW_SUB):
    return ref.at[pl.ds(first_row * sub, rows * sub), :]


def _pack_bf16_pairs(y):
    half = y.shape[1] // 2
    lo = lax.bitcast_convert_type(y[:, :half].astype(BF16).astype(F32), jnp.uint32)
    hi = lax.bitcast_convert_type(y[:, half:].astype(BF16).astype(F32), jnp.uint32)
    return (lo >> 16) | hi


def _unpack_bf16_pairs(u):
    lo = lax.bitcast_convert_type(u << 16, F32)
    hi = lax.bitcast_convert_type(u & jnp.uint32(0xFFFF0000), F32)
    return lo, hi


def _dot_nt(a, b):
    return lax.dot_general(a, b, (((1,), (1,)), ((), ())), preferred_element_type=F32)


def _proj_kernel(x_ref, g_ref, b_ref, cos_ref, sin_ref, w_ref, h_ref, o_ref, hb_ref):
    j = pl.program_id(1)

    @pl.when(j == 0)
    def _():
        h = _layer_norm(x_ref[...], g_ref[...], b_ref[...])
        h_ref[...] = h
        hb_ref[...] = h.astype(BF16)

    acc = _dot(hb_ref[...], w_ref[...])

    def rotate(scale):
        c = cos_ref[...]
        s = sin_ref[...]
        for hh in range(PROJ_TN // RET_QK):
            t = acc[:, hh * RET_QK:(hh + 1) * RET_QK]
            v = t * c + pltpu.roll(t, RET_QK // 2, 1) * s
            if scale != 1.0:
                v = v * scale
            o_ref[:, hh * RET_QK:(hh + 1) * RET_QK] = v.astype(BF16)

    @pl.when(j == SEG_QR // PROJ_TN)
    def _():
        rotate(1.0)

    @pl.when(j == SEG_KR // PROJ_TN)
    def _():
        rotate(RET_QK ** -0.5)

    is_gr = (j >= SEG_GR // PROJ_TN) & (j < SEG_QD // PROJ_TN)
    is_qd = j == SEG_QD // PROJ_TN
    is_gate = j >= SEG_GATE // PROJ_TN
    is_plain = ((j >= SEG_VR // PROJ_TN) & (j < SEG_GR // PROJ_TN)) | (
        (j >= SEG_KD // PROJ_TN) & (j < SEG_GATE // PROJ_TN))

    @pl.when(is_plain)
    def _():
        o_ref[...] = acc.astype(BF16)

    @pl.when(is_gr)
    def _():
        o_ref[...] = _silu(acc).astype(BF16)

    @pl.when(is_qd)
    def _():
        o_ref[...] = (acc * (DIFF_HD ** -0.5 * LOG2E)).astype(BF16)

    @pl.when(is_gate)
    def _():
        o_ref[...] = jax.nn.sigmoid(acc).astype(BF16)


def _proj_call(x2, ln_g, ln_b, cosf, sinf, w_bf, seq):
    n = x2.shape[0]
    tm, tn = PROJ_TM, PROJ_TN
    s_tiles = seq // tm
    return pl.pallas_call(
        _proj_kernel,
        name="proj",
        grid=(n // tm, IN_COLS // tn),
        in_specs=[
            pl.BlockSpec((tm, D_MODEL), lambda i, j: (i, 0)),
            pl.BlockSpec((1, D_MODEL), lambda i, j: (0, 0)),
            pl.BlockSpec((1, D_MODEL), lambda i, j: (0, 0)),
            pl.BlockSpec((tm, RET_QK), lambda i, j: (i % s_tiles, 0)),
            pl.BlockSpec((tm, RET_QK), lambda i, j: (i % s_tiles, 0)),
            pl.BlockSpec((D_MODEL, tn), lambda i, j: (0, j)),
        ],
        out_specs=[
            pl.BlockSpec((tm, D_MODEL), lambda i, j: (i, 0)),
            pl.BlockSpec((tm, tn), lambda i, j: (i, j)),
        ],
        out_shape=[
            jax.ShapeDtypeStruct((n, D_MODEL), F32),
            jax.ShapeDtypeStruct((n, IN_COLS), BF16),
        ],
        scratch_shapes=[pltpu.VMEM((tm, D_MODEL), BF16)],
        compiler_params=pltpu.CompilerParams(
            dimension_semantics=("arbitrary", "arbitrary"), vmem_limit_bytes=VMEM_LIMIT),
    )(x2, ln_g, ln_b, cosf, sinf, w_bf)


def _ret_kernel(q_ref, k_ref, v_ref, g_ref, inner_ref, qdec_ref, kdec_ref, cdec_ref, gn_ref,
                o_ref, state_ref):
    c = pl.program_id(2)

    @pl.when(c == 0)
    def _():
        state_ref[...] = jnp.zeros_like(state_ref)

    st = state_ref[...]
    for j in range(RET_SUB):
        rows = slice(j * RET_CHUNK, (j + 1) * RET_CHUNK)
        q = q_ref[rows, :]
        k = k_ref[rows, :]
        v = v_ref[rows, :]
        scores = _dot_nt(q, k) * inner_ref[0]
        o = _dot(scores.astype(BF16), v) + qdec_ref[0] * _dot(q, st.astype(BF16))
        kd = (k.astype(F32) * kdec_ref[0]).astype(BF16)
        kv = lax.dot_general(kd, v, (((0,), (0,)), ((), ())), preferred_element_type=F32)
        st = cdec_ref[0] * st + kv

        mu = jnp.mean(o, -1, keepdims=True)
        d = o - mu
        var = jnp.mean(d * d, -1, keepdims=True)
        on = d * lax.rsqrt(var + EPS) * gn_ref[0]
        o_ref[rows, :] = (g_ref[rows, :].astype(F32) * on).astype(BF16)
    state_ref[...] = st


def _ret_call(proj, inner, qdec, kdec, cdec, gn, batch, seq):
    n = proj.shape[0]
    c = RET_CHUNK
    rows = RET_SUB * c
    nc = seq // rows
    qb = SEG_QR // RET_QK
    kb = SEG_KR // RET_QK
    vb = SEG_VR // RET_V
    gb = SEG_GR // RET_V
    return pl.pallas_call(
        _ret_kernel,
        name="retention",
        grid=(batch, RET_HEADS, nc),
        in_specs=[
            pl.BlockSpec((rows, RET_QK), lambda b, h, i: (b * nc + i, qb + h)),
            pl.BlockSpec((rows, RET_QK), lambda b, h, i: (b * nc + i, kb + h)),
            pl.BlockSpec((rows, RET_V), lambda b, h, i: (b * nc + i, vb + h)),
            pl.BlockSpec((rows, RET_V), lambda b, h, i: (b * nc + i, gb + h)),
            pl.BlockSpec((1, c, c), lambda b, h, i: (h, 0, 0)),
            pl.BlockSpec((1, c, 1), lambda b, h, i: (h, 0, 0)),
            pl.BlockSpec((1, c, 1), lambda b, h, i: (h, 0, 0)),
            pl.BlockSpec((1, 1, RET_V), lambda b, h, i: (h, 0, 0)),
            pl.BlockSpec((1, 1, RET_V), lambda b, h, i: (h, 0, 0)),
        ],
        out_specs=pl.BlockSpec((rows, RET_V), lambda b, h, i: (b * nc + i, h)),
        out_shape=jax.ShapeDtypeStruct((n, RET_HEADS * RET_V), BF16),
        scratch_shapes=[pltpu.VMEM((RET_QK, RET_V), F32)],
        compiler_params=pltpu.CompilerParams(
            dimension_semantics=("arbitrary", "arbitrary", "arbitrary"),
            vmem_limit_bytes=VMEM_LIMIT),
    )(proj, proj, proj, proj, inner, qdec, kdec, cdec, gn)


def _attn_kernel(qi_ref, ki_ref, q_ref, k_ref, vt_ref, tab_ref, lq1_ref, lk1_ref, lq2_ref,
                 lk2_ref, sg_ref, o_ref, qst_ref, m_ref, acc_ref, s_ref, p_ref, mx_ref):
    t = ATTN_T
    hd2 = 2 * DIFF_HD
    p = pl.program_id(2)
    qi = qi_ref[p]
    ki = ki_ref[p]

    @pl.when(ki == 0)
    def _():
        qt = q_ref[...].astype(F32).T
        sub = lax.broadcasted_iota(I32, qt.shape, 0)
        qst_ref[:, 0:t] = jnp.where(sub < DIFF_HD, qt, 0.0).astype(BF16)
        qst_ref[:, t:2 * t] = jnp.where(sub >= DIFF_HD, qt, 0.0).astype(BF16)
        m_ref[...] = jnp.full_like(m_ref, MASK_VALUE)
        acc_ref[...] = jnp.zeros_like(acc_ref)

    def steps(*biases):
        for sub, bias in enumerate(biases):
            kk = k_ref[sub * t:(sub + 1) * t, :]
            for half in range(2):
                cols = slice(half * t, (half + 1) * t)
                s = _dot(kk, qst_ref[:, cols])
                if bias is not None:
                    s = s + tab_ref[0, bias]
                s_ref[sub, half] = s
                mx_ref[sub, half] = _tree(jnp.maximum, [s[r:r + 8, :] for r in range(0, t, 8)])
        for sub in range(len(biases)):
            vt = vt_ref[:, sub * t:(sub + 1) * t]
            for half in range(2):
                cols = slice(half * t, (half + 1) * t)
                m_prev = m_ref[:, cols]
                m_new = jnp.maximum(m_prev, jnp.max(mx_ref[sub, half], 0, keepdims=True))
                alpha = jnp.exp2(m_prev - m_new)
                for r in range(0, t, 64):
                    p_ref[sub, half, r:r + 64, :] = jnp.exp2(
                        s_ref[sub, half, r:r + 64, :] - m_new).astype(BF16)
                acc_ref[:, cols] = alpha * acc_ref[:, cols] + _dot(vt, p_ref[sub, half])
                m_ref[:, cols] = m_new

    def finalize():
        lam = (jnp.exp(jnp.sum(lq1_ref[...] * lk1_ref[...], -1, keepdims=True))
               - jnp.exp(jnp.sum(lq2_ref[...] * lk2_ref[...], -1, keepdims=True)) + LAM_INIT)
        o1 = acc_ref[0:hd2, 0:t] / acc_ref[hd2:hd2 + 1, 0:t]
        o2 = acc_ref[0:hd2, t:2 * t] / acc_ref[hd2:hd2 + 1, t:2 * t]
        o = o1 - lam * o2
        o = o * lax.rsqrt(jnp.mean(o * o, 0, keepdims=True) + EPS) * (sg_ref[...] * (1.0 - LAM_INIT))
        o_ref[...] = o.T.astype(BF16)

    assert ATTN_KSUB == 2
    last = ki * ATTN_KSUB + 1
    pl.when(last < qi - 1)(functools.partial(steps, None, None))
    pl.when(last == qi - 1)(functools.partial(steps, None, 1))

    @pl.when(last == qi)
    def _():
        steps(1, 0)
        finalize()

    @pl.when(last == qi + 1)
    def _():
        steps(0)
        finalize()


def _attn_call(proj, tab, lq1, lk1, lq2, lk2, sg, batch, seq):
    n = proj.shape[0]
    t = ATTN_T
    nq = seq // t
    tk = ATTN_KSUB * t
    nk = seq // tk
    pairs = [(a, b) for a in range(nq) for b in range(a // ATTN_KSUB + 1)]
    qi_tab = jnp.asarray(np.array([a for a, _ in pairs], np.int32))
    ki_tab = jnp.asarray(np.array([b for _, b in pairs], np.int32))
    hd2 = 2 * DIFF_HD
    qb, kb = SEG_QD // hd2, SEG_KD // hd2
    v = proj[:, SEG_VD:SEG_VD + DIFF_HEADS * hd2].reshape(batch, seq, DIFF_HEADS, hd2)
    vt = jnp.transpose(v, (0, 2, 3, 1))
    extra = jnp.zeros((batch, DIFF_HEADS, ATTN_VT_ROWS - hd2, seq), BF16).at[:, :, 0, :].set(1.0)
    vt = jnp.concatenate([vt, extra], axis=2).reshape(batch * DIFF_HEADS * ATTN_VT_ROWS, seq)
    vec = pl.BlockSpec((1, DIFF_HD), lambda b, h, p, qi, ki: (0, 0))
    grid_spec = pltpu.PrefetchScalarGridSpec(
        num_scalar_prefetch=2,
        grid=(batch, DIFF_HEADS, len(pairs)),
        in_specs=[
            pl.BlockSpec((t, hd2), lambda b, h, p, qi, ki: (b * nq + qi[p], qb + h)),
            pl.BlockSpec((tk, hd2), lambda b, h, p, qi, ki: (b * nk + ki[p], kb + h)),
            pl.BlockSpec((ATTN_VT_ROWS, tk), lambda b, h, p, qi, ki: (b * DIFF_HEADS + h, ki[p])),
            pl.BlockSpec((1, 2, t, t), lambda b, h, p, qi, ki: (h, 0, 0, 0)),
            vec, vec, vec, vec,
            pl.BlockSpec((hd2, 1), lambda b, h, p, qi, ki: (0, 0)),
        ],
        out_specs=pl.BlockSpec((t, hd2), lambda b, h, p, qi, ki: (b * nq + qi[p], h)),
        scratch_shapes=[
            pltpu.VMEM((hd2, 2 * t), BF16),
            pltpu.VMEM((1, 2 * t), F32),
            pltpu.VMEM((ATTN_VT_ROWS, 2 * t), F32),
            pltpu.VMEM((ATTN_KSUB, 2, t, t), F32),
            pltpu.VMEM((ATTN_KSUB, 2, t, t), BF16),
            pltpu.VMEM((ATTN_KSUB, 2, 8, t), F32),
        ],
    )
    return pl.pallas_call(
        _attn_kernel,
        name="diffattn",
        grid_spec=grid_spec,
        out_shape=jax.ShapeDtypeStruct((n, DIFF_HEADS * hd2), BF16),
        compiler_params=pltpu.CompilerParams(
            dimension_semantics=("arbitrary", "arbitrary", "arbitrary"),
            vmem_limit_bytes=VMEM_LIMIT),
    )(qi_tab, ki_tab, proj, proj, vt, tab, lq1, lk1, lq2, lk2, sg)


def _route(scores, bias, carry):
    e, tm = scores.shape
    neg = -jnp.inf
    choice = scores + bias
    row_g = lax.broadcasted_iota(I32, (GROUP_SIZE, tm), 0)
    group_scores = []
    for g in range(N_GROUPS):
        blk = choice[g * GROUP_SIZE:(g + 1) * GROUP_SIZE]
        m1 = jnp.max(blk, 0, keepdims=True)
        i1 = jnp.min(jnp.where(blk == m1, row_g, GROUP_SIZE), 0, keepdims=True)
        m2 = jnp.max(jnp.where(row_g == i1, neg, blk), 0, keepdims=True)
        group_scores.append(m1 + m2)
    gsc = jnp.concatenate(group_scores, 0)
    row8 = lax.broadcasted_iota(I32, (N_GROUPS, tm), 0)
    okf = jnp.zeros((N_GROUPS, tm), F32)
    for _ in range(TOPK_GROUPS):
        m = jnp.max(gsc, 0, keepdims=True)
        i = jnp.min(jnp.where(gsc == m, row8, N_GROUPS), 0, keepdims=True)
        hit = row8 == i
        okf = jnp.where(hit, 1.0, okf)
        gsc = jnp.where(hit, neg, gsc)
    masked = jnp.concatenate(
        [jnp.where(jnp.broadcast_to(okf[g:g + 1], (GROUP_SIZE, tm)) > 0.0,
                   choice[g * GROUP_SIZE:(g + 1) * GROUP_SIZE], neg)
         for g in range(N_GROUPS)], 0)
    row = lax.broadcasted_iota(I32, (e, tm), 0)
    idxs, ws = [], []
    multi = jnp.zeros((e, tm), F32)
    for _ in range(TOP_K):
        m = jnp.max(masked, 0, keepdims=True)
        i = jnp.min(jnp.where(masked == m, row, e), 0, keepdims=True)
        hit = row == i
        ws.append(jnp.sum(jnp.where(hit, scores, 0.0), 0, keepdims=True))
        idxs.append(i)
        multi = jnp.where(hit, 1.0, multi)
        masked = jnp.where(hit, neg, masked)
    a = lax.broadcasted_iota(I32, (tm, tm), 0)
    b = lax.broadcasted_iota(I32, (tm, tm), 1)
    upper = jnp.where(a < b, 1.0, 0.0).astype(BF16)
    before = _dot(multi.astype(BF16), upper) + carry
    ranks = [jnp.sum(jnp.where(row == i, before, 0.0), 0, keepdims=True).astype(I32) for i in idxs]
    new_carry = carry + jnp.sum(multi, 1, keepdims=True)
    wsum = ws[0]
    for w in ws[1:]:
        wsum = wsum + w
    ws = [w / wsum * ROUTED_SCALE for w in ws]
    return idxs, ws, ranks, new_carry


def _post_kernel(yr_ref, od_ref, g0_ref, g1_ref, h_ref, p_ref, wr_ref, wd_ref, wo_ref, l1g_ref,
                 l1b_ref, sg_ref, su_ref, sd_ref, pw_ref, pg_ref, rwt_ref, rb_ref,
                 h1_ref, base_ref, idx_ref, w_ref, rank_ref, cnt_ref, carry_ref):
    i = pl.program_id(0)

    @pl.when(i == 0)
    def _():
        carry_ref[...] = jnp.zeros_like(carry_ref)

    y_r = _dot(yr_ref[...], wr_ref[...])
    y_d = _dot(od_ref[...], wd_ref[...])
    merged = g0_ref[...].astype(F32) * y_r + g1_ref[...].astype(F32) * y_d
    mix = _dot(merged.astype(BF16), wo_ref[...])
    h1 = _layer_norm(DN_ALPHA * h_ref[...] + mix, l1g_ref[...], l1b_ref[...])
    for s in range(ROW_SUB):
        h1_ref[_col_block(0, h1.shape[0], s)] = h1[:, s * LANES:(s + 1) * LANES]
    hb = h1.astype(BF16)

    shared = _dot((_silu(_dot(hb, sg_ref[...])) * _dot(hb, su_ref[...])).astype(BF16), sd_ref[...])
    ple = _dot(p_ref[...].astype(BF16), pw_ref[...]) * jax.nn.sigmoid(_dot(hb, pg_ref[...]))
    base_ref[...] = DN_ALPHA * h1 + shared + ple

    h_lo = (h1 - hb.astype(F32)).astype(BF16)
    rw = rwt_ref[...]
    rw_hi = rw.astype(BF16)
    rw_lo = (rw - rw_hi.astype(F32)).astype(BF16)
    logits = _dot_nt(rw_hi, hb) + (_dot_nt(rw_hi, h_lo) + _dot_nt(rw_lo, hb))
    scores = jax.nn.sigmoid(logits)
    idxs, ws, ranks, new_carry = _route(scores, rb_ref[...], carry_ref[...])
    idx_ref[...] = jnp.concatenate(idxs, 0)
    w_ref[...] = jnp.concatenate(ws, 0)
    rank_ref[...] = jnp.concatenate(ranks, 0)
    carry_ref[...] = new_carry
    cnt_ref[...] = new_carry.astype(I32)


def _post_call(yr, od, proj, h, p2, wr, wd, wo, l1g, l1b, sg, su, sd, pw, pg, rwt, rb):
    n = h.shape[0]
    tm = POST_TM
    g0b = SEG_GATE // D_MODEL

    def full(a):
        return pl.BlockSpec(a.shape, lambda i: (0,) * a.ndim)

    row = lambda w: pl.BlockSpec((tm, w), lambda i: (i, 0))
    tok = lambda: pl.BlockSpec((TOP_K, tm), lambda i: (0, i))
    return pl.pallas_call(
        _post_kernel,
        name="post",
        grid=(n // tm,),
        in_specs=[
            row(RET_HEADS * RET_V), row(D_MODEL),
            pl.BlockSpec((tm, D_MODEL), lambda i: (i, g0b)),
            pl.BlockSpec((tm, D_MODEL), lambda i: (i, g0b + 1)),
            row(D_MODEL), row(PLE_DIM),
            full(wr), full(wd), full(wo), full(l1g), full(l1b), full(sg), full(su), full(sd),
            full(pw), full(pg), full(rwt), full(rb),
        ],
        out_specs=[pl.BlockSpec((tm * ROW_SUB, LANES), lambda i: (i, 0)), row(D_MODEL),
                   tok(), tok(), tok(), pl.BlockSpec((N_EXPERTS, 1), lambda i: (0, 0))],
        out_shape=[
            jax.ShapeDtypeStruct((n * ROW_SUB, LANES), F32),
            jax.ShapeDtypeStruct((n, D_MODEL), F32),
            jax.ShapeDtypeStruct((TOP_K, n), I32),
            jax.ShapeDtypeStruct((TOP_K, n), F32),
            jax.ShapeDtypeStruct((TOP_K, n), I32),
            jax.ShapeDtypeStruct((N_EXPERTS, 1), I32),
        ],
        scratch_shapes=[pltpu.VMEM((N_EXPERTS, 1), F32)],
        compiler_params=pltpu.CompilerParams(
            dimension_semantics=("arbitrary",), vmem_limit_bytes=VMEM_LIMIT),
    )(yr, od, proj, proj, h, p2, wr, wd, wo, l1g, l1b, sg, su, sd, pw, pg, rwt, rb)


def _dest_kernel(idx_ref, rank_ref, ps_ref, o_ref):
    idx = idx_ref[...]
    ps = ps_ref[...]
    td = idx.shape[1]
    row = lax.broadcasted_iota(I32, (N_EXPERTS, td), 0)
    outs = []
    for k in range(TOP_K):
        sel = jnp.sum(jnp.where(row == idx[k:k + 1], ps, 0.0), 0, keepdims=True)
        outs.append(sel.astype(I32))
    o_ref[...] = jnp.concatenate(outs, 0) + rank_ref[...]


def _dest_call(idx, rank, pstart_f):
    n = idx.shape[1]
    td = min(DEST_T, n)
    spec = pl.BlockSpec((TOP_K, td), lambda i: (0, i))
    return pl.pallas_call(
        _dest_kernel,
        name="dest",
        grid=(n // td,),
        in_specs=[spec, spec, pl.BlockSpec((N_EXPERTS, 1), lambda i: (0, 0))],
        out_specs=spec,
        out_shape=jax.ShapeDtypeStruct((TOP_K, n), I32),
        compiler_params=pltpu.CompilerParams(dimension_semantics=("arbitrary",)),
    )(idx, rank, pstart_f)


def _expert_kernel(be_ref, nu_ref, wslot_ref, nexte_ref, rt_ref, rtn_ref, h_hbm, wg_hbm, wu_hbm,
                   wd_hbm, y_ref, buf_ref, wgf_ref, wuf_ref, wdf_ref, wgb_ref, wub_ref, wdb_ref,
                   sem, wsem):
    i = pl.program_id(0)
    g = EXPERT_BLOCK

    def weight_copies(e, s):
        return [pltpu.make_async_copy(src.at[e], dst.at[s], wsem.at[s])
                for src, dst in ((wg_hbm, wgf_ref), (wu_hbm, wuf_ref), (wd_hbm, wdf_ref))]

    def wait(s):
        pltpu.make_async_copy(_row_span(h_hbm, 0, g), _row_span(buf_ref, s * g, g), sem.at[s]).wait()

    @pl.when(i == 0)
    def _():
        for j in range(GATHER_AHEAD):
            def body(r, c, j=j):
                pltpu.make_async_copy(_one_row(h_hbm, rt_ref[j, 0, r]),
                                      _one_row(buf_ref, j * g + r), sem.at[j]).start(priority=GATHER_DMA_PRIORITY)
                return c
            lax.fori_loop(0, g, body, 0, unroll=8)

        for c in weight_copies(be_ref[0], 0):
            c.start()

    used = i < nu_ref[0]
    first = jnp.logical_and(
        used, jnp.logical_or(i == 0, be_ref[i] != be_ref[jnp.maximum(i - 1, 0)]))
    for s in range(2):
        @pl.when(jnp.logical_and(first, wslot_ref[i] == s))
        def _(s=s):
            for c in weight_copies(be_ref[i], s):
                c.wait()
            wgb_ref[...] = wgf_ref[s].astype(BF16)
            wub_ref[...] = wuf_ref[s].astype(BF16)
            wdb_ref[...] = wdf_ref[s].astype(BF16)

            @pl.when(nexte_ref[i] >= 0)
            def _():
                for c in weight_copies(nexte_ref[i], 1 - s):
                    c.start()

    def block(slot, more):
        wait(slot)
        x = jnp.concatenate([buf_ref[_col_block(slot * g, g, s)] for s in range(ROW_SUB)],
                            axis=1).astype(BF16)
        if more:
            ahead = (slot + GATHER_AHEAD) % GATHER_SLOTS
            for r in range(g):
                pltpu.make_async_copy(
                    _one_row(h_hbm, rtn_ref[0, 0, r]), _one_row(buf_ref, ahead * g + r),
                    sem.at[ahead]).start(priority=GATHER_DMA_PRIORITY)
        act = _silu(_dot(x, wgb_ref[...])) * _dot(x, wub_ref[...])
        packed = _pack_bf16_pairs(_dot(act.astype(BF16), wdb_ref[...]))
        for s in range(PACK_SUB):
            y_ref[_col_block(0, g, s, PACK_SUB)] = packed[:, s * LANES:(s + 1) * LANES]

    more = i + GATHER_AHEAD < nu_ref[0]
    for slot in range(GATHER_SLOTS):
        in_slot = jnp.logical_and(i % GATHER_SLOTS == slot, used)
        pl.when(jnp.logical_and(in_slot, more))(functools.partial(block, slot, True))
        pl.when(jnp.logical_and(in_slot, jnp.logical_not(more)))(
            functools.partial(block, slot, False))

    @pl.when(jnp.logical_not(used))
    def _():
        y_ref[...] = jnp.zeros_like(y_ref)


def _expert_call(block_e, n_used, w_slot, next_e, row_tok3, h1, wg, wu, wd):
    nb = block_e.shape[0]
    g = EXPERT_BLOCK
    grid_spec = pltpu.PrefetchScalarGridSpec(
        num_scalar_prefetch=4,
        grid=(nb,),
        in_specs=[
            pl.BlockSpec((GATHER_AHEAD, 1, g), lambda i, *_: (0, 0, 0), memory_space=pltpu.SMEM),
            pl.BlockSpec((1, 1, g), lambda i, *_: (jnp.minimum(i + GATHER_AHEAD, nb - 1), 0, 0),
                         memory_space=pltpu.SMEM),
            pl.BlockSpec(memory_space=pl.ANY),
            pl.BlockSpec(memory_space=pl.ANY),
            pl.BlockSpec(memory_space=pl.ANY),
            pl.BlockSpec(memory_space=pl.ANY),
        ],
        out_specs=pl.BlockSpec((g * PACK_SUB, LANES), lambda i, *_: (i, 0)),
        scratch_shapes=[
            pltpu.VMEM((GATHER_SLOTS * g * ROW_SUB, LANES), F32),
            pltpu.VMEM((2, D_MODEL, D_EXPERT), F32),
            pltpu.VMEM((2, D_MODEL, D_EXPERT), F32),
            pltpu.VMEM((2, D_EXPERT, D_MODEL), F32),
            pltpu.VMEM((D_MODEL, D_EXPERT), BF16),
            pltpu.VMEM((D_MODEL, D_EXPERT), BF16),
            pltpu.VMEM((D_EXPERT, D_MODEL), BF16),
            pltpu.SemaphoreType.DMA((GATHER_SLOTS,)),
            pltpu.SemaphoreType.DMA((2,)),
        ],
    )
    assert nb > GATHER_AHEAD
    return pl.pallas_call(
        _expert_kernel,
        name="experts",
        grid_spec=grid_spec,
        out_shape=jax.ShapeDtypeStruct((nb * g * PACK_SUB, LANES), jnp.uint32),
        compiler_params=pltpu.CompilerParams(
            dimension_semantics=("arbitrary",), vmem_limit_bytes=VMEM_LIMIT),
    )(block_e, n_used, w_slot, next_e, row_tok3, row_tok3, h1, wg, wu, wd)


def _combine_kernel(d_ref, dn_ref, y_hbm, w_ref, base_ref, g_ref, b_ref, o_ref, buf_ref, sem):
    i = pl.program_id(0)
    nt = pl.num_programs(0)
    tt = COMB_T

    def plane(s, k):
        return (s * TOP_K + k) * tt

    def wait(s):
        for k in range(TOP_K):
            pltpu.make_async_copy(_row_span(y_hbm, 0, tt, PACK_SUB),
                                  _row_span(buf_ref, plane(s, k), tt, PACK_SUB), sem.at[s]).wait()

    @pl.when(i == 0)
    def _():
        for j in range(GATHER_AHEAD):
            def body(t, c, j=j):
                for k in range(TOP_K):
                    pltpu.make_async_copy(
                        _one_row(y_hbm, d_ref[j, k, t], PACK_SUB),
                        _one_row(buf_ref, plane(j, k) + t, PACK_SUB),
                        sem.at[j]).start(priority=k % DMA_QUEUES)
                return c
            lax.fori_loop(0, tt, body, 0)

    def tile(slot):
        wait(slot)
        ahead = (slot + GATHER_AHEAD) % GATHER_SLOTS
        for t in range(tt):
            for k in range(TOP_K):
                pltpu.make_async_copy(_one_row(y_hbm, dn_ref[0, k, t], PACK_SUB),
                                      _one_row(buf_ref, plane(ahead, k) + t, PACK_SUB),
                                      sem.at[ahead]).start(priority=k % DMA_QUEUES)
        w = w_ref[...]
        lows, highs = [], []
        for s in range(PACK_SUB):
            lo_acc = hi_acc = None
            for k in range(TOP_K):
                lo, hi = _unpack_bf16_pairs(
                    buf_ref[_col_block(plane(slot, k), tt, s, PACK_SUB)])
                wk = w[:, k:k + 1]
                lo_acc = wk * lo if lo_acc is None else lo_acc + wk * lo
                hi_acc = wk * hi if hi_acc is None else hi_acc + wk * hi
            lows.append(lo_acc)
            highs.append(hi_acc)
        routed = jnp.concatenate(lows + highs, axis=1)
        o_ref[...] = _layer_norm(base_ref[...] + routed, g_ref[...], b_ref[...])

        @pl.when(i == nt - 1)
        def _():
            for j in range(1, GATHER_AHEAD + 1):
                wait((slot + j) % GATHER_SLOTS)

    for slot in range(GATHER_SLOTS):
        pl.when(i % GATHER_SLOTS == slot)(functools.partial(tile, slot))


def _combine_call(dest3, y, w_t, base, ln_g, ln_b):
    n = base.shape[0]
    tt = COMB_T
    nt = n // tt
    return pl.pallas_call(
        _combine_kernel,
        name="combine",
        grid=(nt,),
        in_specs=[
            pl.BlockSpec((GATHER_AHEAD, TOP_K, tt), lambda i: (0, 0, 0), memory_space=pltpu.SMEM),
            pl.BlockSpec((1, TOP_K, tt), lambda i: (jnp.minimum(i + GATHER_AHEAD, nt - 1), 0, 0),
                         memory_space=pltpu.SMEM),
            pl.BlockSpec(memory_space=pl.ANY),
            pl.BlockSpec((tt, TOP_K), lambda i: (i, 0)),
            pl.BlockSpec((tt, D_MODEL), lambda i: (i, 0)),
            pl.BlockSpec((1, D_MODEL), lambda i: (0, 0)),
            pl.BlockSpec((1, D_MODEL), lambda i: (0, 0)),
        ],
        out_specs=pl.BlockSpec((tt, D_MODEL), lambda i: (i, 0)),
        out_shape=jax.ShapeDtypeStruct((n, D_MODEL), F32),
        scratch_shapes=[
            pltpu.VMEM((GATHER_SLOTS * TOP_K * tt * PACK_SUB, LANES), jnp.uint32),
            pltpu.SemaphoreType.DMA((GATHER_SLOTS,)),
        ],
        compiler_params=pltpu.CompilerParams(
            dimension_semantics=("arbitrary",), vmem_limit_bytes=VMEM_LIMIT),
    )(dest3, dest3, y, w_t, base, ln_g, ln_b)


def _xpos_tables(seq):
    half = RET_QK // 2
    pos = jnp.arange(seq, dtype=F32)
    theta = 1.0 / (XPOS_BASE ** jnp.linspace(0.0, 1.0, half, dtype=F32))
    ang = pos[:, None] * theta[None, :]
    cos = jnp.cos(ang)
    sin = jnp.sin(ang)
    return jnp.concatenate([cos, cos], -1), jnp.concatenate([-sin, sin], -1)


def _decay_tables():
    c = RET_CHUNK
    log_gamma = jnp.log(1.0 - 2.0 ** (-5.0 - jnp.arange(RET_HEADS, dtype=F32)))
    idx = jnp.arange(c, dtype=F32)
    rel = idx[:, None] - idx[None, :]
    inner = jnp.where(rel >= 0, jnp.exp(jnp.maximum(rel, 0.0)[None] * log_gamma[:, None, None]), 0.0)
    q_dec = jnp.exp((idx + 1.0)[None, :] * log_gamma[:, None])
    k_dec = jnp.exp((c - 1.0 - idx)[None, :] * log_gamma[:, None])
    c_dec = jnp.exp(c * log_gamma)
    cdec = jnp.broadcast_to(c_dec[:, None, None], (RET_HEADS, 1, RET_V))
    return inner, q_dec[:, :, None], k_dec[:, :, None], cdec


def _t5_bucket(n):
    max_exact = REL_BUCKETS // 2
    nf = jnp.maximum(n, 1).astype(F32)
    large = max_exact + (jnp.log(nf / max_exact) / math.log(REL_MAX_DIST / max_exact)
                         * (REL_BUCKETS - max_exact)).astype(I32)
    large = jnp.minimum(large, REL_BUCKETS - 1)
    return jnp.where(n < max_exact, n, large)


def _bias_tables(rel_bias, t):
    c = REL_MAX_DIST
    nb = t // c
    far = rel_bias[REL_BUCKETS - 1].astype(F32)
    d = jnp.arange(c)[None, :] - jnp.arange(c)[:, None]
    def lookup(dist):
        onehot = jax.nn.one_hot(_t5_bucket(dist), REL_BUCKETS, dtype=F32)
        return jnp.einsum('ijk,kh->ijh', onehot, rel_bias.astype(F32),
                          precision=lax.Precision.HIGHEST)

    near0 = (lookup(jnp.maximum(d, 0)) - far) * LOG2E
    near0 = jnp.where((d >= 0)[:, :, None], near0, MASK_VALUE)
    near1 = (lookup(d + c) - far) * LOG2E
    tile0 = jnp.tile(near0, (nb, nb, 1))
    tile1 = jnp.tile(near1, (nb, nb, 1))
    kb = (jnp.arange(t) // c)[:, None, None]
    qb = (jnp.arange(t) // c)[None, :, None]
    diag = jnp.where(qb == kb, tile0,
                     jnp.where(qb == kb + 1, tile1, jnp.where(qb > kb, 0.0, MASK_VALUE)))
    left = jnp.where((qb == 0) & (kb == nb - 1), tile1, 0.0)
    return jnp.transpose(jnp.stack([diag, left], 0), (3, 0, 1, 2))


def kernel(x, p, ln_in_g, ln_in_b, rel_bias, w_in, ret_gn_g, lambda_q1, lambda_k1, lambda_q2,
           lambda_k2, diff_subln_g, w_ret_o, w_diff_o, w_o, ln1_g, ln1_b, router_w, router_bias,
           exp_w_gate, exp_w_up, exp_w_down, sh_w_gate, sh_w_up, sh_w_down, ple_w, ple_gate_w,
           ln2_g, ln2_b):
    batch, seq, d = x.shape
    n = batch * seq
    assert d == D_MODEL and seq % (ATTN_T * ATTN_KSUB) == 0 and seq % PROJ_TM == 0
    assert n % min(DEST_T, n) == 0 and n % COMB_T == 0 and n % POST_TM == 0
    x2 = x.reshape(n, d)
    row = lambda a: a.reshape(1, -1).astype(F32)

    cosf, sinf = _xpos_tables(seq)
    inner, qdec, kdec, cdec = _decay_tables()
    h, proj = _proj_call(x2, row(ln_in_g), row(ln_in_b), cosf, sinf, w_in[0].astype(BF16), seq)

    yr = _ret_call(proj, inner, qdec, kdec, cdec,
                   ret_gn_g[0].reshape(RET_HEADS, 1, RET_V).astype(F32), batch, seq)

    od = _attn_call(proj, _bias_tables(rel_bias, ATTN_T), row(lambda_q1[0]), row(lambda_k1[0]),
                    row(lambda_q2[0]), row(lambda_k2[0]),
                    diff_subln_g[0].reshape(-1, 1).astype(F32), batch, seq)

    h1, base, idx, w, rank, counts = _post_call(
        yr, od, proj, h, p[0].reshape(n, PLE_DIM),
        w_ret_o[0].astype(BF16), w_diff_o[0].astype(BF16), w_o[0].astype(BF16),
        row(ln1_g[0]), row(ln1_b[0]),
        sh_w_gate[0].astype(BF16), sh_w_up[0].astype(BF16), sh_w_down[0].astype(BF16),
        ple_w[0].astype(BF16), ple_gate_w[0].astype(BF16),
        router_w[0].T.astype(F32), router_bias[0].reshape(N_EXPERTS, 1).astype(F32))

    g = EXPERT_BLOCK
    counts = counts.reshape(N_EXPERTS)
    padded = (counts + g - 1) // g * g
    pend = jnp.cumsum(padded)
    pstart = pend - padded
    n_rows = -(-(n * TOP_K + N_EXPERTS * (g - 1)) // g) * g
    n_blocks = n_rows // g
    block_start = jnp.arange(n_blocks, dtype=I32) * g
    block_e = jnp.minimum(
        jnp.sum((pend[None, :] <= block_start[:, None]).astype(I32), axis=1),
        N_EXPERTS - 1).astype(I32)

    dest = _dest_call(idx, rank, pstart.astype(F32).reshape(N_EXPERTS, 1))
    tok = jnp.broadcast_to(jnp.arange(n, dtype=I32)[None, :], (TOP_K, n))
    row_tok = jnp.zeros((n_rows,), I32).at[dest.reshape(-1)].set(
        tok.reshape(-1), unique_indices=True, mode='drop')

    n_used = (pend[N_EXPERTS - 1:] // g).astype(I32)
    assert (n * TOP_K) // g >= GATHER_AHEAD
    present = padded > 0
    ids = jnp.arange(N_EXPERTS, dtype=I32)
    at_or_after = lax.cummin(jnp.where(present, ids, N_EXPERTS)[::-1])[::-1]
    after = jnp.concatenate([at_or_after[1:], jnp.full((1,), N_EXPERTS, I32)])
    after = jnp.where(after >= N_EXPERTS, -1, after)
    order = jnp.cumsum(present.astype(I32)) - 1
    w_slot = (jnp.take(order, block_e) % 2).astype(I32)
    next_e = jnp.take(after, block_e).astype(I32)
    y = _expert_call(block_e, n_used, w_slot, next_e, row_tok.reshape(n_blocks, 1, g), h1,
                     exp_w_gate[0], exp_w_up[0], exp_w_down[0])

    dest3 = dest.reshape(TOP_K, n // COMB_T, COMB_T).transpose(1, 0, 2)
    out = _combine_call(dest3, y, w.T, base, row(ln2_g[0]), row(ln2_b[0]))
    return out.reshape(batch, seq, d)
```

```python
import functools
import math

import numpy as np
import jax
import jax.numpy as jnp
from jax import lax
from jax.experimental import pallas as pl
from jax.experimental.pallas import tpu as pltpu

F32 = jnp.float32
BF16 = jnp.bfloat16
I32 = jnp.int32

D_MODEL = 1024
PLE_DIM = 256
RET_HEADS = 8
RET_QK = 128
RET_V = 256
RET_CHUNK = 128
XPOS_BASE = 10000.0
DIFF_HEADS = 8
DIFF_HD = 64
REL_BUCKETS = 32
REL_MAX_DIST = 128
N_EXPERTS = 256
TOP_K = 8
N_GROUPS = 8
GROUP_SIZE = N_EXPERTS // N_GROUPS
TOPK_GROUPS = 4
D_EXPERT = 256
ROUTED_SCALE = 2.5
EXPERT_BLOCK = 256
DEPTH = 1
DN_ALPHA = (2.0 * DEPTH) ** 0.25
EPS = 1e-5
LAM_INIT = 0.8 - 0.6 * math.exp(-0.3 * 0)

SEG_QR, SEG_KR, SEG_VR, SEG_GR, SEG_QD, SEG_KD, SEG_VD, SEG_GATE = (
    0, 1024, 2048, 4096, 6144, 7168, 8192, 9216)
IN_COLS = 11264

LANES = 128
ROW_SUB = D_MODEL // LANES
PACK_SUB = ROW_SUB // 2
MASK_VALUE = -1e30
LOG2E = math.log2(math.e)
VMEM_LIMIT = 56 * 1024 * 1024

RET_SUB = 4
PROJ_TM = 1024
PROJ_TN = 1024
ATTN_T = 512
ATTN_KSUB = 2
ATTN_VT_ROWS = 2 * DIFF_HD + 16
POST_TM = 512
DEST_T = 2048
COMB_T = 32
DMA_QUEUES = 2
GATHER_DMA_PRIORITY = 1
WEIGHT_DMA_PRIORITY = 1
GATHER_AHEAD = 3
GATHER_SLOTS = GATHER_AHEAD + 1


def _layer_norm(x, g, b):
    mu = jnp.mean(x, -1, keepdims=True)
    d = x - mu
    var = jnp.mean(d * d, -1, keepdims=True)
    return d * lax.rsqrt(var + EPS) * g + b


def _silu(x):
    return x * jax.nn.sigmoid(x)


def _dot(a, b):
    return jnp.dot(a, b, preferred_element_type=F32)


def _tree(op, items):
    items = list(items)
    while len(items) > 1:
        items = [op(items[i], items[i + 1]) if i + 1 < len(items) else items[i]
                 for i in range(0, len(items), 2)]
    return items[0]


def _col_block(first_row, rows, s, sub=ROW_SUB):
    return (pl.ds(first_row * sub + s, rows, stride=sub), slice(None))


def _one_row(ref, row, sub=ROW_SUB):
    if isinstance(row, int):
        return ref.at[pl.ds(row * sub, sub), :]
    return ref.at[pl.ds(pl.multiple_of(row * sub, sub), sub), :]


def _row_span(ref, first_row, rows, sub=ROW_SUB):
    return ref.at[pl.ds(first_row * sub, rows * sub), :]


def _pack_bf16_pairs(y):
    half = y.shape[1] // 2
    lo = lax.bitcast_convert_type(y[:, :half].astype(BF16).astype(F32), jnp.uint32)
    hi = lax.bitcast_convert_type(y[:, half:].astype(BF16).astype(F32), jnp.uint32)
    return (lo >> 16) | hi


def _unpack_bf16_pairs(u):
    lo = lax.bitcast_convert_type(u << 16, F32)
    hi = lax.bitcast_convert_type(u & jnp.uint32(0xFFFF0000), F32)
    return lo, hi


def _dot_nt(a, b):
    return lax.dot_general(a, b, (((1,), (1,)), ((), ())), preferred_element_type=F32)


def _proj_kernel(x_ref, g_ref, b_ref, cos_ref, sin_ref, w_ref, h_ref, o_ref, hb_ref):
    j = pl.program_id(1)

    @pl.when(j == 0)
    def _():
        h = _layer_norm(x_ref[...], g_ref[...], b_ref[...])
        h_ref[...] = h
        hb_ref[...] = h.astype(BF16)

    acc = _dot(hb_ref[...], w_ref[...])

    def rotate(scale):
        c = cos_ref[...]
        s = sin_ref[...]
        for hh in range(PROJ_TN // RET_QK):
            t = acc[:, hh * RET_QK:(hh + 1) * RET_QK]
            v = t * c + pltpu.roll(t, RET_QK // 2, 1) * s
            if scale != 1.0:
                v = v * scale
            o_ref[:, hh * RET_QK:(hh + 1) * RET_QK] = v.astype(BF16)

    @pl.when(j == SEG_QR // PROJ_TN)
    def _():
        rotate(1.0)

    @pl.when(j == SEG_KR // PROJ_TN)
    def _():
        rotate(RET_QK ** -0.5)

    is_gr = (j >= SEG_GR // PROJ_TN) & (j < SEG_QD // PROJ_TN)
    is_qd = j == SEG_QD // PROJ_TN
    is_gate = j >= SEG_GATE // PROJ_TN
    is_plain = ((j >= SEG_VR // PROJ_TN) & (j < SEG_GR // PROJ_TN)) | (
        (j >= SEG_KD // PROJ_TN) & (j < SEG_GATE // PROJ_TN))

    @pl.when(is_plain)
    def _():
        o_ref[...] = acc.astype(BF16)

    @pl.when(is_gr)
    def _():
        o_ref[...] = _silu(acc).astype(BF16)

    @pl.when(is_qd)
    def _():
        o_ref[...] = (acc * (DIFF_HD ** -0.5 * LOG2E)).astype(BF16)

    @pl.when(is_gate)
    def _():
        o_ref[...] = jax.nn.sigmoid(acc).astype(BF16)


def _proj_call(x2, ln_g, ln_b, cosf, sinf, w_bf, seq):
    n = x2.shape[0]
    tm, tn = PROJ_TM, PROJ_TN
    s_tiles = seq // tm
    return pl.pallas_call(
        _proj_kernel,
        name="proj",
        grid=(n // tm, IN_COLS // tn),
        in_specs=[
            pl.BlockSpec((tm, D_MODEL), lambda i, j: (i, 0)),
            pl.BlockSpec((1, D_MODEL), lambda i, j: (0, 0)),
            pl.BlockSpec((1, D_MODEL), lambda i, j: (0, 0)),
            pl.BlockSpec((tm, RET_QK), lambda i, j: (i % s_tiles, 0)),
            pl.BlockSpec((tm, RET_QK), lambda i, j: (i % s_tiles, 0)),
            pl.BlockSpec((D_MODEL, tn), lambda i, j: (0, j)),
        ],
        out_specs=[
            pl.BlockSpec((tm, D_MODEL), lambda i, j: (i, 0)),
            pl.BlockSpec((tm, tn), lambda i, j: (i, j)),
        ],
        out_shape=[
            jax.ShapeDtypeStruct((n, D_MODEL), F32),
            jax.ShapeDtypeStruct((n, IN_COLS), BF16),
        ],
        scratch_shapes=[pltpu.VMEM((tm, D_MODEL), BF16)],
        compiler_params=pltpu.CompilerParams(
            dimension_semantics=("arbitrary", "arbitrary"), vmem_limit_bytes=VMEM_LIMIT),
    )(x2, ln_g, ln_b, cosf, sinf, w_bf)


def _ret_kernel(q_ref, k_ref, v_ref, g_ref, inner_ref, qdec_ref, kdec_ref, cdec_ref, gn_ref,
                o_ref, state_ref):
    c = pl.program_id(2)

    @pl.when(c == 0)
    def _():
        state_ref[...] = jnp.zeros_like(state_ref)

    st = state_ref[...]
    for j in range(RET_SUB):
        rows = slice(j * RET_CHUNK, (j + 1) * RET_CHUNK)
        q = q_ref[rows, :]
        k = k_ref[rows, :]
        v = v_ref[rows, :]
        scores = _dot_nt(q, k) * inner_ref[0]
        o = _dot(scores.astype(BF16), v) + qdec_ref[0] * _dot(q, st.astype(BF16))
        kd = (k.astype(F32) * kdec_ref[0]).astype(BF16)
        kv = lax.dot_general(kd, v, (((0,), (0,)), ((), ())), preferred_element_type=F32)
        st = cdec_ref[0] * st + kv

        mu = jnp.mean(o, -1, keepdims=True)
        d = o - mu
        var = jnp.mean(d * d, -1, keepdims=True)
        on = d * lax.rsqrt(var + EPS) * gn_ref[0]
        o_ref[rows, :] = (g_ref[rows, :].astype(F32) * on).astype(BF16)
    state_ref[...] = st


def _ret_call(proj, inner, qdec, kdec, cdec, gn, batch, seq):
    n = proj.shape[0]
    c = RET_CHUNK
    rows = RET_SUB * c
    nc = seq // rows
    qb = SEG_QR // RET_QK
    kb = SEG_KR // RET_QK
    vb = SEG_VR // RET_V
    gb = SEG_GR // RET_V
    return pl.pallas_call(
        _ret_kernel,
        name="retention",
        grid=(batch, RET_HEADS, nc),
        in_specs=[
            pl.BlockSpec((rows, RET_QK), lambda b, h, i: (b * nc + i, qb + h)),
            pl.BlockSpec((rows, RET_QK), lambda b, h, i: (b * nc + i, kb + h)),
            pl.BlockSpec((rows, RET_V), lambda b, h, i: (b * nc + i, vb + h)),
            pl.BlockSpec((rows, RET_V), lambda b, h, i: (b * nc + i, gb + h)),
            pl.BlockSpec((1, c, c), lambda b, h, i: (h, 0, 0)),
            pl.BlockSpec((1, c, 1), lambda b, h, i: (h, 0, 0)),
            pl.BlockSpec((1, c, 1), lambda b, h, i: (h, 0, 0)),
            pl.BlockSpec((1, 1, RET_V), lambda b, h, i: (h, 0, 0)),
            pl.BlockSpec((1, 1, RET_V), lambda b, h, i: (h, 0, 0)),
        ],
        out_specs=pl.BlockSpec((rows, RET_V), lambda b, h, i: (b * nc + i, h)),
        out_shape=jax.ShapeDtypeStruct((n, RET_HEADS * RET_V), BF16),
        scratch_shapes=[pltpu.VMEM((RET_QK, RET_V), F32)],
        compiler_params=pltpu.CompilerParams(
            dimension_semantics=("arbitrary", "arbitrary", "arbitrary"),
            vmem_limit_bytes=VMEM_LIMIT),
    )(proj, proj, proj, proj, inner, qdec, kdec, cdec, gn)


def _attn_kernel(qi_ref, ki_ref, q_ref, k_ref, vt_ref, tab_ref, lq1_ref, lk1_ref, lq2_ref,
                 lk2_ref, sg_ref, o_ref, qst_ref, m_ref, acc_ref, s_ref, p_ref, mx_ref):
    t = ATTN_T
    hd2 = 2 * DIFF_HD
    p = pl.program_id(2)
    qi = qi_ref[p]
    ki = ki_ref[p]

    @pl.when(ki == 0)
    def _():
        qt = q_ref[...].astype(F32).T
        sub = lax.broadcasted_iota(I32, qt.shape, 0)
        qst_ref[:, 0:t] = jnp.where(sub < DIFF_HD, qt, 0.0).astype(BF16)
        qst_ref[:, t:2 * t] = jnp.where(sub >= DIFF_HD, qt, 0.0).astype(BF16)
        m_ref[...] = jnp.full_like(m_ref, MASK_VALUE)
        acc_ref[...] = jnp.zeros_like(acc_ref)

    def steps(*biases):
        for sub, bias in enumerate(biases):
            kk = k_ref[sub * t:(sub + 1) * t, :]
            for half in range(2):
                cols = slice(half * t, (half + 1) * t)
                s = _dot(kk, qst_ref[:, cols])
                if bias is not None:
                    s = s + tab_ref[0, bias]
                s_ref[sub, half] = s
                mx_ref[sub, half] = _tree(jnp.maximum, [s[r:r + 8, :] for r in range(0, t, 8)])
        for sub in range(len(biases)):
            vt = vt_ref[:, sub * t:(sub + 1) * t]
            for half in range(2):
                cols = slice(half * t, (half + 1) * t)
                m_prev = m_ref[:, cols]
                m_new = jnp.maximum(m_prev, jnp.max(mx_ref[sub, half], 0, keepdims=True))
                alpha = jnp.exp2(m_prev - m_new)
                for r in range(0, t, 64):
                    p_ref[sub, half, r:r + 64, :] = jnp.exp2(
                        s_ref[sub, half, r:r + 64, :] - m_new).astype(BF16)
                acc_ref[:, cols] = alpha * acc_ref[:, cols] + _dot(vt, p_ref[sub, half])
                m_ref[:, cols] = m_new

    def finalize():
        lam = (jnp.exp(jnp.sum(lq1_ref[...] * lk1_ref[...], -1, keepdims=True))
               - jnp.exp(jnp.sum(lq2_ref[...] * lk2_ref[...], -1, keepdims=True)) + LAM_INIT)
        o1 = acc_ref[0:hd2, 0:t] / acc_ref[hd2:hd2 + 1, 0:t]
        o2 = acc_ref[0:hd2, t:2 * t] / acc_ref[hd2:hd2 + 1, t:2 * t]
        o = o1 - lam * o2
        o = o * lax.rsqrt(jnp.mean(o * o, 0, keepdims=True) + EPS) * (sg_ref[...] * (1.0 - LAM_INIT))
        o_ref[...] = o.T.astype(BF16)

    assert ATTN_KSUB == 2
    last = ki * ATTN_KSUB + 1
    pl.when(last < qi - 1)(functools.partial(steps, None, None))
    pl.when(last == qi - 1)(functools.partial(steps, None, 1))

    @pl.when(last == qi)
    def _():
        steps(1, 0)
        finalize()

    @pl.when(last == qi + 1)
    def _():
        steps(0)
        finalize()


def _attn_call(proj, tab, lq1, lk1, lq2, lk2, sg, batch, seq):
    n = proj.shape[0]
    t = ATTN_T
    nq = seq // t
    tk = ATTN_KSUB * t
    nk = seq // tk
    pairs = [(a, b) for a in range(nq) for b in range(a // ATTN_KSUB + 1)]
    qi_tab = jnp.asarray(np.array([a for a, _ in pairs], np.int32))
    ki_tab = jnp.asarray(np.array([b for _, b in pairs], np.int32))
    hd2 = 2 * DIFF_HD
    qb, kb = SEG_QD // hd2, SEG_KD // hd2
    v = proj[:, SEG_VD:SEG_VD + DIFF_HEADS * hd2].reshape(batch, seq, DIFF_HEADS, hd2)
    vt = jnp.transpose(v, (0, 2, 3, 1))
    extra = jnp.zeros((batch, DIFF_HEADS, ATTN_VT_ROWS - hd2, seq), BF16).at[:, :, 0, :].set(1.0)
    vt = jnp.concatenate([vt, extra], axis=2).reshape(batch * DIFF_HEADS * ATTN_VT_ROWS, seq)
    vec = pl.BlockSpec((1, DIFF_HD), lambda b, h, p, qi, ki: (0, 0))
    grid_spec = pltpu.PrefetchScalarGridSpec(
        num_scalar_prefetch=2,
        grid=(batch, DIFF_HEADS, len(pairs)),
        in_specs=[
            pl.BlockSpec((t, hd2), lambda b, h, p, qi, ki: (b * nq + qi[p], qb + h)),
            pl.BlockSpec((tk, hd2), lambda b, h, p, qi, ki: (b * nk + ki[p], kb + h)),
            pl.BlockSpec((ATTN_VT_ROWS, tk), lambda b, h, p, qi, ki: (b * DIFF_HEADS + h, ki[p])),
            pl.BlockSpec((1, 2, t, t), lambda b, h, p, qi, ki: (h, 0, 0, 0)),
            vec, vec, vec, vec,
            pl.BlockSpec((hd2, 1), lambda b, h, p, qi, ki: (0, 0)),
        ],
        out_specs=pl.BlockSpec((t, hd2), lambda b, h, p, qi, ki: (b * nq + qi[p], h)),
        scratch_shapes=[
            pltpu.VMEM((hd2, 2 * t), BF16),
            pltpu.VMEM((1, 2 * t), F32),
            pltpu.VMEM((ATTN_VT_ROWS, 2 * t), F32),
            pltpu.VMEM((ATTN_KSUB, 2, t, t), F32),
            pltpu.VMEM((ATTN_KSUB, 2, t, t), BF16),
            pltpu.VMEM((ATTN_KSUB, 2, 8, t), F32),
        ],
    )
    return pl.pallas_call(
        _attn_kernel,
        name="diffattn",
        grid_spec=grid_spec,
        out_shape=jax.ShapeDtypeStruct((n, DIFF_HEADS * hd2), BF16),
        compiler_params=pltpu.CompilerParams(
            dimension_semantics=("arbitrary", "arbitrary", "arbitrary"),
            vmem_limit_bytes=VMEM_LIMIT),
    )(qi_tab, ki_tab, proj, proj, vt, tab, lq1, lk1, lq2, lk2, sg)


def _route(scores, bias, carry):
    e, tm = scores.shape
    neg = -jnp.inf
    choice = scores + bias
    row_g = lax.broadcasted_iota(I32, (GROUP_SIZE, tm), 0)
    group_scores = []
    for g in range(N_GROUPS):
        blk = choice[g * GROUP_SIZE:(g + 1) * GROUP_SIZE]
        m1 = jnp.max(blk, 0, keepdims=True)
        i1 = jnp.min(jnp.where(blk == m1, row_g, GROUP_SIZE), 0, keepdims=True)
        m2 = jnp.max(jnp.where(row_g == i1, neg, blk), 0, keepdims=True)
        group_scores.append(m1 + m2)
    gsc = jnp.concatenate(group_scores, 0)
    row8 = lax.broadcasted_iota(I32, (N_GROUPS, tm), 0)
    okf = jnp.zeros((N_GROUPS, tm), F32)
    for _ in range(TOPK_GROUPS):
        m = jnp.max(gsc, 0, keepdims=True)
        i = jnp.min(jnp.where(gsc == m, row8, N_GROUPS), 0, keepdims=True)
        hit = row8 == i
        okf = jnp.where(hit, 1.0, okf)
        gsc = jnp.where(hit, neg, gsc)
    masked = jnp.concatenate(
        [jnp.where(jnp.broadcast_to(okf[g:g + 1], (GROUP_SIZE, tm)) > 0.0,
                   choice[g * GROUP_SIZE:(g + 1) * GROUP_SIZE], neg)
         for g in range(N_GROUPS)], 0)
    row = lax.broadcasted_iota(I32, (e, tm), 0)
    idxs, ws = [], []
    multi = jnp.zeros((e, tm), F32)
    for _ in range(TOP_K):
        m = jnp.max(masked, 0, keepdims=True)
        i = jnp.min(jnp.where(masked == m, row, e), 0, keepdims=True)
        hit = row == i
        ws.append(jnp.sum(jnp.where(hit, scores, 0.0), 0, keepdims=True))
        idxs.append(i)
        multi = jnp.where(hit, 1.0, multi)
        masked = jnp.where(hit, neg, masked)
    a = lax.broadcasted_iota(I32, (tm, tm), 0)
    b = lax.broadcasted_iota(I32, (tm, tm), 1)
    upper = jnp.where(a < b, 1.0, 0.0).astype(BF16)
    before = _dot(multi.astype(BF16), upper) + carry
    ranks = [jnp.sum(jnp.where(row == i, before, 0.0), 0, keepdims=True).astype(I32) for i in idxs]
    new_carry = carry + jnp.sum(multi, 1, keepdims=True)
    wsum = ws[0]
    for w in ws[1:]:
        wsum = wsum + w
    ws = [w / wsum * ROUTED_SCALE for w in ws]
    return idxs, ws, ranks, new_carry


def _post_kernel(yr_ref, od_ref, g0_ref, g1_ref, h_ref, p_ref, wr_ref, wd_ref, wo_ref, l1g_ref,
                 l1b_ref, sg_ref, su_ref, sd_ref, pw_ref, pg_ref, rwt_ref, rb_ref,
                 h1_ref, base_ref, idx_ref, w_ref, rank_ref, cnt_ref, carry_ref):
    i = pl.program_id(0)

    @pl.when(i == 0)
    def _():
        carry_ref[...] = jnp.zeros_like(carry_ref)

    y_r = _dot(yr_ref[...], wr_ref[...])
    y_d = _dot(od_ref[...], wd_ref[...])
    merged = g0_ref[...].astype(F32) * y_r + g1_ref[...].astype(F32) * y_d
    mix = _dot(merged.astype(BF16), wo_ref[...])
    h1 = _layer_norm(DN_ALPHA * h_ref[...] + mix, l1g_ref[...], l1b_ref[...])
    for s in range(ROW_SUB):
        h1_ref[_col_block(0, h1.shape[0], s)] = h1[:, s * LANES:(s + 1) * LANES]
    hb = h1.astype(BF16)

    shared = _dot((_silu(_dot(hb, sg_ref[...])) * _dot(hb, su_ref[...])).astype(BF16), sd_ref[...])
    ple = _dot(p_ref[...].astype(BF16), pw_ref[...]) * jax.nn.sigmoid(_dot(hb, pg_ref[...]))
    base_ref[...] = DN_ALPHA * h1 + shared + ple

    h_lo = (h1 - hb.astype(F32)).astype(BF16)
    rw = rwt_ref[...]
    rw_hi = rw.astype(BF16)
    rw_lo = (rw - rw_hi.astype(F32)).astype(BF16)
    logits = _dot_nt(rw_hi, hb) + (_dot_nt(rw_hi, h_lo) + _dot_nt(rw_lo, hb))
    scores = jax.nn.sigmoid(logits)
    idxs, ws, ranks, new_carry = _route(scores, rb_ref[...], carry_ref[...])
    idx_ref[...] = jnp.concatenate(idxs, 0)
    w_ref[...] = jnp.concatenate(ws, 0)
    rank_ref[...] = jnp.concatenate(ranks, 0)
    carry_ref[...] = new_carry
    cnt_ref[...] = new_carry.astype(I32)


def _post_call(yr, od, proj, h, p2, wr, wd, wo, l1g, l1b, sg, su, sd, pw, pg, rwt, rb):
    n = h.shape[0]
    tm = POST_TM
    g0b = SEG_GATE // D_MODEL

    def full(a):
        return pl.BlockSpec(a.shape, lambda i: (0,) * a.ndim)

    row = lambda w: pl.BlockSpec((tm, w), lambda i: (i, 0))
    tok = lambda: pl.BlockSpec((TOP_K, tm), lambda i: (0, i))
    return pl.pallas_call(
        _post_kernel,
        name="post",
        grid=(n // tm,),
        in_specs=[
            row(RET_HEADS * RET_V), row(D_MODEL),
            pl.BlockSpec((tm, D_MODEL), lambda i: (i, g0b)),
            pl.BlockSpec((tm, D_MODEL), lambda i: (i, g0b + 1)),
            row(D_MODEL), row(PLE_DIM),
            full(wr), full(wd), full(wo), full(l1g), full(l1b), full(sg), full(su), full(sd),
            full(pw), full(pg), full(rwt), full(rb),
        ],
        out_specs=[pl.BlockSpec((tm * ROW_SUB, LANES), lambda i: (i, 0)), row(D_MODEL),
                   tok(), tok(), tok(), pl.BlockSpec((N_EXPERTS, 1), lambda i: (0, 0))],
        out_shape=[
            jax.ShapeDtypeStruct((n * ROW_SUB, LANES), F32),
            jax.ShapeDtypeStruct((n, D_MODEL), F32),
            jax.ShapeDtypeStruct((TOP_K, n), I32),
            jax.ShapeDtypeStruct((TOP_K, n), F32),
            jax.ShapeDtypeStruct((TOP_K, n), I32),
            jax.ShapeDtypeStruct((N_EXPERTS, 1), I32),
        ],
        scratch_shapes=[pltpu.VMEM((N_EXPERTS, 1), F32)],
        compiler_params=pltpu.CompilerParams(
            dimension_semantics=("arbitrary",), vmem_limit_bytes=VMEM_LIMIT),
    )(yr, od, proj, proj, h, p2, wr, wd, wo, l1g, l1b, sg, su, sd, pw, pg, rwt, rb)


def _dest_kernel(idx_ref, rank_ref, ps_ref, o_ref):
    idx = idx_ref[...]
    ps = ps_ref[...]
    td = idx.shape[1]
    row = lax.broadcasted_iota(I32, (N_EXPERTS, td), 0)
    outs = []
    for k in range(TOP_K):
        sel = jnp.sum(jnp.where(row == idx[k:k + 1], ps, 0.0), 0, keepdims=True)
        outs.append(sel.astype(I32))
    o_ref[...] = jnp.concatenate(outs, 0) + rank_ref[...]


def _dest_call(idx, rank, pstart_f):
    n = idx.shape[1]
    td = min(DEST_T, n)
    spec = pl.BlockSpec((TOP_K, td), lambda i: (0, i))
    return pl.pallas_call(
        _dest_kernel,
        name="dest",
        grid=(n // td,),
        in_specs=[spec, spec, pl.BlockSpec((N_EXPERTS, 1), lambda i: (0, 0))],
        out_specs=spec,
        out_shape=jax.ShapeDtypeStruct((TOP_K, n), I32),
        compiler_params=pltpu.CompilerParams(dimension_semantics=("arbitrary",)),
    )(idx, rank, pstart_f)


def _expert_kernel(be_ref, nu_ref, wslot_ref, nexte_ref, rt_ref, rtn_ref, h_hbm, wg_hbm, wu_hbm,
                   wd_hbm, y_ref, buf_ref, wgf_ref, wuf_ref, wdf_ref, wgb_ref, wub_ref, wdb_ref,
                   sem, wsem):
    i = pl.program_id(0)
    g = EXPERT_BLOCK

    def weight_copies(e, s):
        return [pltpu.make_async_copy(src.at[e], dst.at[s], wsem.at[s])
                for src, dst in ((wg_hbm, wgf_ref), (wu_hbm, wuf_ref), (wd_hbm, wdf_ref))]

    def wait(s):
        pltpu.make_async_copy(_row_span(h_hbm, 0, g), _row_span(buf_ref, s * g, g), sem.at[s]).wait()

    @pl.when(i == 0)
    def _():
        for j in range(GATHER_AHEAD):
            def body(r, c, j=j):
                pltpu.make_async_copy(_one_row(h_hbm, rt_ref[j, 0, r]),
                                      _one_row(buf_ref, j * g + r), sem.at[j]).start(priority=GATHER_DMA_PRIORITY)
                return c
            lax.fori_loop(0, g, body, 0, unroll=8)

        for c in weight_copies(be_ref[0], 0):
            c.start(priority=WEIGHT_DMA_PRIORITY)

    used = i < nu_ref[0]
    first = jnp.logical_and(
        used, jnp.logical_or(i == 0, be_ref[i] != be_ref[jnp.maximum(i - 1, 0)]))
    for s in range(2):
        @pl.when(jnp.logical_and(first, wslot_ref[i] == s))
        def _(s=s):
            for c in weight_copies(be_ref[i], s):
                c.wait()
            wgb_ref[...] = wgf_ref[s].astype(BF16)
            wub_ref[...] = wuf_ref[s].astype(BF16)
            wdb_ref[...] = wdf_ref[s].astype(BF16)

            @pl.when(nexte_ref[i] >= 0)
            def _():
                for c in weight_copies(nexte_ref[i], 1 - s):
                    c.start(priority=WEIGHT_DMA_PRIORITY)

    def block(slot, more):
        wait(slot)
        x = jnp.concatenate([buf_ref[_col_block(slot * g, g, s)] for s in range(ROW_SUB)],
                            axis=1).astype(BF16)
        if more:
            ahead = (slot + GATHER_AHEAD) % GATHER_SLOTS
            for r in range(g):
                pltpu.make_async_copy(
                    _one_row(h_hbm, rtn_ref[0, 0, r]), _one_row(buf_ref, ahead * g + r),
                    sem.at[ahead]).start(priority=r % DMA_QUEUES)
        act = _silu(_dot(x, wgb_ref[...])) * _dot(x, wub_ref[...])
        packed = _pack_bf16_pairs(_dot(act.astype(BF16), wdb_ref[...]))
        for s in range(PACK_SUB):
            y_ref[_col_block(0, g, s, PACK_SUB)] = packed[:, s * LANES:(s + 1) * LANES]

    more = i + GATHER_AHEAD < nu_ref[0]
    for slot in range(GATHER_SLOTS):
        in_slot = jnp.logical_and(i % GATHER_SLOTS == slot, used)
        pl.when(jnp.logical_and(in_slot, more))(functools.partial(block, slot, True))
        pl.when(jnp.logical_and(in_slot, jnp.logical_not(more)))(
            functools.partial(block, slot, False))

    @pl.when(jnp.logical_not(used))
    def _():
        y_ref[...] = jnp.zeros_like(y_ref)


def _expert_call(block_e, n_used, w_slot, next_e, row_tok3, h1, wg, wu, wd):
    nb = block_e.shape[0]
    g = EXPERT_BLOCK
    grid_spec = pltpu.PrefetchScalarGridSpec(
        num_scalar_prefetch=4,
        grid=(nb,),
        in_specs=[
            pl.BlockSpec((GATHER_AHEAD, 1, g), lambda i, *_: (0, 0, 0), memory_space=pltpu.SMEM),
            pl.BlockSpec((1, 1, g), lambda i, *_: (jnp.minimum(i + GATHER_AHEAD, nb - 1), 0, 0),
                         memory_space=pltpu.SMEM),
            pl.BlockSpec(memory_space=pl.ANY),
            pl.BlockSpec(memory_space=pl.ANY),
            pl.BlockSpec(memory_space=pl.ANY),
            pl.BlockSpec(memory_space=pl.ANY),
        ],
        out_specs=pl.BlockSpec((g * PACK_SUB, LANES), lambda i, *_: (i, 0)),
        scratch_shapes=[
            pltpu.VMEM((GATHER_SLOTS * g * ROW_SUB, LANES), F32),
            pltpu.VMEM((2, D_MODEL, D_EXPERT), F32),
            pltpu.VMEM((2, D_MODEL, D_EXPERT), F32),
            pltpu.VMEM((2, D_EXPERT, D_MODEL), F32),
            pltpu.VMEM((D_MODEL, D_EXPERT), BF16),
            pltpu.VMEM((D_MODEL, D_EXPERT), BF16),
            pltpu.VMEM((D_EXPERT, D_MODEL), BF16),
            pltpu.SemaphoreType.DMA((GATHER_SLOTS,)),
            pltpu.SemaphoreType.DMA((2,)),
        ],
    )
    assert nb > GATHER_AHEAD
    return pl.pallas_call(
        _expert_kernel,
        name="experts",
        grid_spec=grid_spec,
        out_shape=jax.ShapeDtypeStruct((nb * g * PACK_SUB, LANES), jnp.uint32),
        compiler_params=pltpu.CompilerParams(
            dimension_semantics=("arbitrary",), vmem_limit_bytes=VMEM_LIMIT),
    )(block_e, n_used, w_slot, next_e, row_tok3, row_tok3, h1, wg, wu, wd)


def _combine_kernel(d_ref, dn_ref, y_hbm, w_ref, base_ref, g_ref, b_ref, o_ref, buf_ref, sem):
    i = pl.program_id(0)
    nt = pl.num_programs(0)
    tt = COMB_T

    def plane(s, k):
        return (s * TOP_K + k) * tt

    def wait(s):
        for k in range(TOP_K):
            pltpu.make_async_copy(_row_span(y_hbm, 0, tt, PACK_SUB),
                                  _row_span(buf_ref, plane(s, k), tt, PACK_SUB), sem.at[s]).wait()

    @pl.when(i == 0)
    def _():
        for j in range(GATHER_AHEAD):
            def body(t, c, j=j):
                for k in range(TOP_K):
                    pltpu.make_async_copy(
                        _one_row(y_hbm, d_ref[j, k, t], PACK_SUB),
                        _one_row(buf_ref, plane(j, k) + t, PACK_SUB),
                        sem.at[j]).start(priority=k % DMA_QUEUES)
                return c
            lax.fori_loop(0, tt, body, 0)

    def tile(slot):
        wait(slot)
        ahead = (slot + GATHER_AHEAD) % GATHER_SLOTS
        for t in range(tt):
            for k in range(TOP_K):
                pltpu.make_async_copy(_one_row(y_hbm, dn_ref[0, k, t], PACK_SUB),
                                      _one_row(buf_ref, plane(ahead, k) + t, PACK_SUB),
                                      sem.at[ahead]).start(priority=k % DMA_QUEUES)
        w = w_ref[...]
        lows, highs = [], []
        for s in range(PACK_SUB):
            lo_acc = hi_acc = None
            for k in range(TOP_K):
                lo, hi = _unpack_bf16_pairs(
                    buf_ref[_col_block(plane(slot, k), tt, s, PACK_SUB)])
                wk = w[:, k:k + 1]
                lo_acc = wk * lo if lo_acc is None else lo_acc + wk * lo
                hi_acc = wk * hi if hi_acc is None else hi_acc + wk * hi
            lows.append(lo_acc)
            highs.append(hi_acc)
        routed = jnp.concatenate(lows + highs, axis=1)
        o_ref[...] = _layer_norm(base_ref[...] + routed, g_ref[...], b_ref[...])

        @pl.when(i == nt - 1)
        def _():
            for j in range(1, GATHER_AHEAD + 1):
                wait((slot + j) % GATHER_SLOTS)

    for slot in range(GATHER_SLOTS):
        pl.when(i % GATHER_SLOTS == slot)(functools.partial(tile, slot))


def _combine_call(dest3, y, w_t, base, ln_g, ln_b):
    n = base.shape[0]
    tt = COMB_T
    nt = n // tt
    return pl.pallas_call(
        _combine_kernel,
        name="combine",
        grid=(nt,),
        in_specs=[
            pl.BlockSpec((GATHER_AHEAD, TOP_K, tt), lambda i: (0, 0, 0), memory_space=pltpu.SMEM),
            pl.BlockSpec((1, TOP_K, tt), lambda i: (jnp.minimum(i + GATHER_AHEAD, nt - 1), 0, 0),
                         memory_space=pltpu.SMEM),
            pl.BlockSpec(memory_space=pl.ANY),
            pl.BlockSpec((tt, TOP_K), lambda i: (i, 0)),
            pl.BlockSpec((tt, D_MODEL), lambda i: (i, 0)),
            pl.BlockSpec((1, D_MODEL), lambda i: (0, 0)),
            pl.BlockSpec((1, D_MODEL), lambda i: (0, 0)),
        ],
        out_specs=pl.BlockSpec((tt, D_MODEL), lambda i: (i, 0)),
        out_shape=jax.ShapeDtypeStruct((n, D_MODEL), F32),
        scratch_shapes=[
            pltpu.VMEM((GATHER_SLOTS * TOP_K * tt * PACK_SUB, LANES), jnp.uint32),
            pltpu.SemaphoreType.DMA((GATHER_SLOTS,)),
        ],
        compiler_params=pltpu.CompilerParams(
            dimension_semantics=("arbitrary",), vmem_limit_bytes=VMEM_LIMIT),
    )(dest3, dest3, y, w_t, base, ln_g, ln_b)


def _xpos_tables(seq):
    half = RET_QK // 2
    pos = jnp.arange(seq, dtype=F32)
    theta = 1.0 / (XPOS_BASE ** jnp.linspace(0.0, 1.0, half, dtype=F32))
    ang = pos[:, None] * theta[None, :]
    cos = jnp.cos(ang)
    sin = jnp.sin(ang)
    return jnp.concatenate([cos, cos], -1), jnp.concatenate([-sin, sin], -1)


def _decay_tables():
    c = RET_CHUNK
    log_gamma = jnp.log(1.0 - 2.0 ** (-5.0 - jnp.arange(RET_HEADS, dtype=F32)))
    idx = jnp.arange(c, dtype=F32)
    rel = idx[:, None] - idx[None, :]
    inner = jnp.where(rel >= 0, jnp.exp(jnp.maximum(rel, 0.0)[None] * log_gamma[:, None, None]), 0.0)
    q_dec = jnp.exp((idx + 1.0)[None, :] * log_gamma[:, None])
    k_dec = jnp.exp((c - 1.0 - idx)[None, :] * log_gamma[:, None])
    c_dec = jnp.exp(c * log_gamma)
    cdec = jnp.broadcast_to(c_dec[:, None, None], (RET_HEADS, 1, RET_V))
    return inner, q_dec[:, :, None], k_dec[:, :, None], cdec


def _t5_bucket(n):
    max_exact = REL_BUCKETS // 2
    nf = jnp.maximum(n, 1).astype(F32)
    large = max_exact + (jnp.log(nf / max_exact) / math.log(REL_MAX_DIST / max_exact)
                         * (REL_BUCKETS - max_exact)).astype(I32)
    large = jnp.minimum(large, REL_BUCKETS - 1)
    return jnp.where(n < max_exact, n, large)


def _bias_tables(rel_bias, t):
    c = REL_MAX_DIST
    nb = t // c
    far = rel_bias[REL_BUCKETS - 1].astype(F32)
    d = jnp.arange(c)[None, :] - jnp.arange(c)[:, None]
    def lookup(dist):
        onehot = jax.nn.one_hot(_t5_bucket(dist), REL_BUCKETS, dtype=F32)
        return jnp.einsum('ijk,kh->ijh', onehot, rel_bias.astype(F32),
                          precision=lax.Precision.HIGHEST)

    near0 = (lookup(jnp.maximum(d, 0)) - far) * LOG2E
    near0 = jnp.where((d >= 0)[:, :, None], near0, MASK_VALUE)
    near1 = (lookup(d + c) - far) * LOG2E
    tile0 = jnp.tile(near0, (nb, nb, 1))
    tile1 = jnp.tile(near1, (nb, nb, 1))
    kb = (jnp.arange(t) // c)[:, None, None]
    qb = (jnp.arange(t) // c)[None, :, None]
    diag = jnp.where(qb == kb, tile0,
                     jnp.where(qb == kb + 1, tile1, jnp.where(qb > kb, 0.0, MASK_VALUE)))
    left = jnp.where((qb == 0) & (kb == nb - 1), tile1, 0.0)
    return jnp.transpose(jnp.stack([diag, left], 0), (3, 0, 1, 2))


def kernel(x, p, ln_in_g, ln_in_b, rel_bias, w_in, ret_gn_g, lambda_q1, lambda_k1, lambda_q2,
           lambda_k2, diff_subln_g, w_ret_o, w_diff_o, w_o, ln1_g, ln1_b, router_w, router_bias,
           exp_w_gate, exp_w_up, exp_w_down, sh_w_gate, sh_w_up, sh_w_down, ple_w, ple_gate_w,
           ln2_g, ln2_b):
    batch, seq, d = x.shape
    n = batch * seq
    assert d == D_MODEL and seq % (ATTN_T * ATTN_KSUB) == 0 and seq % PROJ_TM == 0
    assert n % min(DEST_T, n) == 0 and n % COMB_T == 0 and n % POST_TM == 0
    x2 = x.reshape(n, d)
    row = lambda a: a.reshape(1, -1).astype(F32)

    cosf, sinf = _xpos_tables(seq)
    inner, qdec, kdec, cdec = _decay_tables()
    h, proj = _proj_call(x2, row(ln_in_g), row(ln_in_b), cosf, sinf, w_in[0].astype(BF16), seq)

    yr = _ret_call(proj, inner, qdec, kdec, cdec,
                   ret_gn_g[0].reshape(RET_HEADS, 1, RET_V).astype(F32), batch, seq)

    od = _attn_call(proj, _bias_tables(rel_bias, ATTN_T), row(lambda_q1[0]), row(lambda_k1[0]),
                    row(lambda_q2[0]), row(lambda_k2[0]),
                    diff_subln_g[0].reshape(-1, 1).astype(F32), batch, seq)

    h1, base, idx, w, rank, counts = _post_call(
        yr, od, proj, h, p[0].reshape(n, PLE_DIM),
        w_ret_o[0].astype(BF16), w_diff_o[0].astype(BF16), w_o[0].astype(BF16),
        row(ln1_g[0]), row(ln1_b[0]),
        sh_w_gate[0].astype(BF16), sh_w_up[0].astype(BF16), sh_w_down[0].astype(BF16),
        ple_w[0].astype(BF16), ple_gate_w[0].astype(BF16),
        router_w[0].T.astype(F32), router_bias[0].reshape(N_EXPERTS, 1).astype(F32))

    g = EXPERT_BLOCK
    counts = counts.reshape(N_EXPERTS)
    padded = (counts + g - 1) // g * g
    pend = jnp.cumsum(padded)
    pstart = pend - padded
    n_rows = -(-(n * TOP_K + N_EXPERTS * (g - 1)) // g) * g
    n_blocks = n_rows // g
    block_start = jnp.arange(n_blocks, dtype=I32) * g
    block_e = jnp.minimum(
        jnp.sum((pend[None, :] <= block_start[:, None]).astype(I32), axis=1),
        N_EXPERTS - 1).astype(I32)

    dest = _dest_call(idx, rank, pstart.astype(F32).reshape(N_EXPERTS, 1))
    tok = jnp.broadcast_to(jnp.arange(n, dtype=I32)[None, :], (TOP_K, n))
    row_tok = jnp.zeros((n_rows,), I32).at[dest.reshape(-1)].set(
        tok.reshape(-1), unique_indices=True, mode='drop')

    n_used = (pend[N_EXPERTS - 1:] // g).astype(I32)
    assert (n * TOP_K) // g >= GATHER_AHEAD
    present = padded > 0
    ids = jnp.arange(N_EXPERTS, dtype=I32)
    at_or_after = lax.cummin(jnp.where(present, ids, N_EXPERTS)[::-1])[::-1]
    after = jnp.concatenate([at_or_after[1:], jnp.full((1,), N_EXPERTS, I32)])
    after = jnp.where(after >= N_EXPERTS, -1, after)
    order = jnp.cumsum(present.astype(I32)) - 1
    w_slot = (jnp.take(order, block_e) % 2).astype(I32)
    next_e = jnp.take(after, block_e).astype(I32)
    y = _expert_call(block_e, n_used, w_slot, next_e, row_tok.reshape(n_blocks, 1, g), h1,
                     exp_w_gate[0], exp_w_up[0], exp_w_down[0])

    dest3 = dest.reshape(TOP_K, n // COMB_T, COMB_T).transpose(1, 0, 2)
    out = _combine_call(dest3, y, w.T, base, row(ln2_g[0]), row(ln2_b[0]))
    return out.reshape(batch, seq, d)
```

```python
import functools
import math

import numpy as np
import jax
import jax.numpy as jnp
from jax import lax
from jax.experimental import pallas as pl
from jax.experimental.pallas import tpu as pltpu

F32 = jnp.float32
BF16 = jnp.bfloat16
I32 = jnp.int32

D_MODEL = 1024
PLE_DIM = 256
RET_HEADS = 8
RET_QK = 128
RET_V = 256
RET_CHUNK = 128
XPOS_BASE = 10000.0
DIFF_HEADS = 8
DIFF_HD = 64
REL_BUCKETS = 32
REL_MAX_DIST = 128
N_EXPERTS = 256
TOP_K = 8
N_GROUPS = 8
GROUP_SIZE = N_EXPERTS // N_GROUPS
TOPK_GROUPS = 4
D_EXPERT = 256
ROUTED_SCALE = 2.5
EXPERT_BLOCK = 256
DEPTH = 1
DN_ALPHA = (2.0 * DEPTH) ** 0.25
EPS = 1e-5
LAM_INIT = 0.8 - 0.6 * math.exp(-0.3 * 0)

SEG_QR, SEG_KR, SEG_VR, SEG_GR, SEG_QD, SEG_KD, SEG_VD, SEG_GATE = (
    0, 1024, 2048, 4096, 6144, 7168, 8192, 9216)
IN_COLS = 11264

LANES = 128
ROW_SUB = D_MODEL // LANES
PACK_SUB = ROW_SUB // 2
MASK_VALUE = -1e30
LOG2E = math.log2(math.e)
VMEM_LIMIT = 56 * 1024 * 1024

RET_SUB = 4
PROJ_TM = 1024
PROJ_TN = 1024
ATTN_T = 512
ATTN_KSUB = 2
ATTN_VT_ROWS = 2 * DIFF_HD + 16
POST_TM = 512
DEST_T = 2048
COMB_T = 32
DMA_QUEUES = 2
GATHER_DMA_PRIORITY = 1
WEIGHT_DMA_PRIORITY = 1
GATHER_AHEAD = 3
GATHER_SLOTS = GATHER_AHEAD + 1


def _layer_norm(x, g, b):
    mu = jnp.mean(x, -1, keepdims=True)
    d = x - mu
    var = jnp.mean(d * d, -1, keepdims=True)
    return d * lax.rsqrt(var + EPS) * g + b


def _silu(x):
    return x * jax.nn.sigmoid(x)


def _dot(a, b):
    return jnp.dot(a, b, preferred_element_type=F32)


def _tree(op, items):
    items = list(items)
    while len(items) > 1:
        items = [op(items[i], items[i + 1]) if i + 1 < len(items) else items[i]
                 for i in range(0, len(items), 2)]
    return items[0]


def _col_block(first_row, rows, s, sub=ROW_SUB):
    return (pl.ds(first_row * sub + s, rows, stride=sub), slice(None))


def _one_row(ref, row, sub=ROW_SUB):
    if isinstance(row, int):
        return ref.at[pl.ds(row * sub, sub), :]
    return ref.at[pl.ds(pl.multiple_of(row * sub, sub), sub), :]


def _row_span(ref, first_row, rows, sub=ROW_SUB):
    return ref.at[pl.ds(first_row * sub, rows * sub), :]


def _pack_bf16_pairs(y):
    half = y.shape[1] // 2
    lo = lax.bitcast_convert_type(y[:, :half].astype(BF16).astype(F32), jnp.uint32)
    hi = lax.bitcast_convert_type(y[:, half:].astype(BF16).astype(F32), jnp.uint32)
    return (lo >> 16) | hi


def _unpack_bf16_pairs(u):
    lo = lax.bitcast_convert_type(u << 16, F32)
    hi = lax.bitcast_convert_type(u & jnp.uint32(0xFFFF0000), F32)
    return lo, hi


def _dot_nt(a, b):
    return lax.dot_general(a, b, (((1,), (1,)), ((), ())), preferred_element_type=F32)


def _proj_kernel(x_ref, g_ref, b_ref, cos_ref, sin_ref, w_ref, h_ref, o_ref, hb_ref):
    j = pl.program_id(1)

    @pl.when(j == 0)
    def _():
        h = _layer_norm(x_ref[...], g_ref[...], b_ref[...])
        h_ref[...] = h
        hb_ref[...] = h.astype(BF16)

    acc = _dot(hb_ref[...], w_ref[...])

    def rotate(scale):
        c = cos_ref[...]
        s = sin_ref[...]
        for hh in range(PROJ_TN // RET_QK):
            t = acc[:, hh * RET_QK:(hh + 1) * RET_QK]
            v = t * c + pltpu.roll(t, RET_QK // 2, 1) * s
            if scale != 1.0:
                v = v * scale
            o_ref[:, hh * RET_QK:(hh + 1) * RET_QK] = v.astype(BF16)

    @pl.when(j == SEG_QR // PROJ_TN)
    def _():
        rotate(1.0)

    @pl.when(j == SEG_KR // PROJ_TN)
    def _():
        rotate(RET_QK ** -0.5)

    is_gr = (j >= SEG_GR // PROJ_TN) & (j < SEG_QD // PROJ_TN)
    is_qd = j == SEG_QD // PROJ_TN
    is_gate = j >= SEG_GATE // PROJ_TN
    is_plain = ((j >= SEG_VR // PROJ_TN) & (j < SEG_GR // PROJ_TN)) | (
        (j >= SEG_KD // PROJ_TN) & (j < SEG_GATE // PROJ_TN))

    @pl.when(is_plain)
    def _():
        o_ref[...] = acc.astype(BF16)

    @pl.when(is_gr)
    def _():
        o_ref[...] = _silu(acc).astype(BF16)

    @pl.when(is_qd)
    def _():
        o_ref[...] = (acc * (DIFF_HD ** -0.5 * LOG2E)).astype(BF16)

    @pl.when(is_gate)
    def _():
        o_ref[...] = jax.nn.sigmoid(acc).astype(BF16)


def _proj_call(x2, ln_g, ln_b, cosf, sinf, w_bf, seq):
    n = x2.shape[0]
    tm, tn = PROJ_TM, PROJ_TN
    s_tiles = seq // tm
    return pl.pallas_call(
        _proj_kernel,
        name="proj",
        grid=(n // tm, IN_COLS // tn),
        in_specs=[
            pl.BlockSpec((tm, D_MODEL), lambda i, j: (i, 0)),
            pl.BlockSpec((1, D_MODEL), lambda i, j: (0, 0)),
            pl.BlockSpec((1, D_MODEL), lambda i, j: (0, 0)),
            pl.BlockSpec((tm, RET_QK), lambda i, j: (i % s_tiles, 0)),
            pl.BlockSpec((tm, RET_QK), lambda i, j: (i % s_tiles, 0)),
            pl.BlockSpec((D_MODEL, tn), lambda i, j: (0, j)),
        ],
        out_specs=[
            pl.BlockSpec((tm, D_MODEL), lambda i, j: (i, 0)),
            pl.BlockSpec((tm, tn), lambda i, j: (i, j)),
        ],
        out_shape=[
            jax.ShapeDtypeStruct((n, D_MODEL), F32),
            jax.ShapeDtypeStruct((n, IN_COLS), BF16),
        ],
        scratch_shapes=[pltpu.VMEM((tm, D_MODEL), BF16)],
        compiler_params=pltpu.CompilerParams(
            dimension_semantics=("arbitrary", "arbitrary"), vmem_limit_bytes=VMEM_LIMIT),
    )(x2, ln_g, ln_b, cosf, sinf, w_bf)


def _ret_kernel(q_ref, k_ref, v_ref, g_ref, inner_ref, qdec_ref, kdec_ref, cdec_ref, gn_ref,
                o_ref, state_ref):
    c = pl.program_id(2)

    @pl.when(c == 0)
    def _():
        state_ref[...] = jnp.zeros_like(state_ref)

    st = state_ref[...]
    for j in range(RET_SUB):
        rows = slice(j * RET_CHUNK, (j + 1) * RET_CHUNK)
        q = q_ref[rows, :]
        k = k_ref[rows, :]
        v = v_ref[rows, :]
        scores = _dot_nt(q, k) * inner_ref[0]
        o = _dot(scores.astype(BF16), v) + qdec_ref[0] * _dot(q, st.astype(BF16))
        kd = (k.astype(F32) * kdec_ref[0]).astype(BF16)
        kv = lax.dot_general(kd, v, (((0,), (0,)), ((), ())), preferred_element_type=F32)
        st = cdec_ref[0] * st + kv

        mu = jnp.mean(o, -1, keepdims=True)
        d = o - mu
        var = jnp.mean(d * d, -1, keepdims=True)
        on = d * lax.rsqrt(var + EPS) * gn_ref[0]
        o_ref[rows, :] = (g_ref[rows, :].astype(F32) * on).astype(BF16)
    state_ref[...] = st


def _ret_call(proj, inner, qdec, kdec, cdec, gn, batch, seq):
    n = proj.shape[0]
    c = RET_CHUNK
    rows = RET_SUB * c
    nc = seq // rows
    qb = SEG_QR // RET_QK
    kb = SEG_KR // RET_QK
    vb = SEG_VR // RET_V
    gb = SEG_GR // RET_V
    return pl.pallas_call(
        _ret_kernel,
        name="retention",
        grid=(batch, RET_HEADS, nc),
        in_specs=[
            pl.BlockSpec((rows, RET_QK), lambda b, h, i: (b * nc + i, qb + h)),
            pl.BlockSpec((rows, RET_QK), lambda b, h, i: (b * nc + i, kb + h)),
            pl.BlockSpec((rows, RET_V), lambda b, h, i: (b * nc + i, vb + h)),
            pl.BlockSpec((rows, RET_V), lambda b, h, i: (b * nc + i, gb + h)),
            pl.BlockSpec((1, c, c), lambda b, h, i: (h, 0, 0)),
            pl.BlockSpec((1, c, 1), lambda b, h, i: (h, 0, 0)),
            pl.BlockSpec((1, c, 1), lambda b, h, i: (h, 0, 0)),
            pl.BlockSpec((1, 1, RET_V), lambda b, h, i: (h, 0, 0)),
            pl.BlockSpec((1, 1, RET_V), lambda b, h, i: (h, 0, 0)),
        ],
        out_specs=pl.BlockSpec((rows, RET_V), lambda b, h, i: (b * nc + i, h)),
        out_shape=jax.ShapeDtypeStruct((n, RET_HEADS * RET_V), BF16),
        scratch_shapes=[pltpu.VMEM((RET_QK, RET_V), F32)],
        compiler_params=pltpu.CompilerParams(
            dimension_semantics=("arbitrary", "arbitrary", "arbitrary"),
            vmem_limit_bytes=VMEM_LIMIT),
    )(proj, proj, proj, proj, inner, qdec, kdec, cdec, gn)


def _attn_kernel(qi_ref, ki_ref, q_ref, k_ref, vt_ref, tab_ref, lq1_ref, lk1_ref, lq2_ref,
                 lk2_ref, sg_ref, o_ref, qst_ref, m_ref, acc_ref, s_ref, p_ref, mx_ref):
    t = ATTN_T
    hd2 = 2 * DIFF_HD
    p = pl.program_id(2)
    qi = qi_ref[p]
    ki = ki_ref[p]

    @pl.when(ki == 0)
    def _():
        qt = q_ref[...].astype(F32).T
        sub = lax.broadcasted_iota(I32, qt.shape, 0)
        qst_ref[:, 0:t] = jnp.where(sub < DIFF_HD, qt, 0.0).astype(BF16)
        qst_ref[:, t:2 * t] = jnp.where(sub >= DIFF_HD, qt, 0.0).astype(BF16)
        m_ref[...] = jnp.full_like(m_ref, MASK_VALUE)
        acc_ref[...] = jnp.zeros_like(acc_ref)

    def steps(*biases):
        for sub, bias in enumerate(biases):
            kk = k_ref[sub * t:(sub + 1) * t, :]
            for half in range(2):
                cols = slice(half * t, (half + 1) * t)
                s = _dot(kk, qst_ref[:, cols])
                if bias is not None:
                    s = s + tab_ref[0, bias]
                s_ref[sub, half] = s
                mx_ref[sub, half] = _tree(jnp.maximum, [s[r:r + 8, :] for r in range(0, t, 8)])
        for sub in range(len(biases)):
            vt = vt_ref[:, sub * t:(sub + 1) * t]
            for half in range(2):
                cols = slice(half * t, (half + 1) * t)
                m_prev = m_ref[:, cols]
                m_new = jnp.maximum(m_prev, jnp.max(mx_ref[sub, half], 0, keepdims=True))
                alpha = jnp.exp2(m_prev - m_new)
                for r in range(0, t, 64):
                    p_ref[sub, half, r:r + 64, :] = jnp.exp2(
                        s_ref[sub, half, r:r + 64, :] - m_new).astype(BF16)
                acc_ref[:, cols] = alpha * acc_ref[:, cols] + _dot(vt, p_ref[sub, half])
                m_ref[:, cols] = m_new

    def finalize():
        lam = (jnp.exp(jnp.sum(lq1_ref[...] * lk1_ref[...], -1, keepdims=True))
               - jnp.exp(jnp.sum(lq2_ref[...] * lk2_ref[...], -1, keepdims=True)) + LAM_INIT)
        o1 = acc_ref[0:hd2, 0:t] / acc_ref[hd2:hd2 + 1, 0:t]
        o2 = acc_ref[0:hd2, t:2 * t] / acc_ref[hd2:hd2 + 1, t:2 * t]
        o = o1 - lam * o2
        o = o * lax.rsqrt(jnp.mean(o * o, 0, keepdims=True) + EPS) * (sg_ref[...] * (1.0 - LAM_INIT))
        o_ref[...] = o.T.astype(BF16)

    assert ATTN_KSUB == 2
    last = ki * ATTN_KSUB + 1
    pl.when(last < qi - 1)(functools.partial(steps, None, None))
    pl.when(last == qi - 1)(functools.partial(steps, None, 1))

    @pl.when(last == qi)
    def _():
        steps(1, 0)
        finalize()

    @pl.when(last == qi + 1)
    def _():
        steps(0)
        finalize()


def _attn_call(proj, tab, lq1, lk1, lq2, lk2, sg, batch, seq):
    n = proj.shape[0]
    t = ATTN_T
    nq = seq // t
    tk = ATTN_KSUB * t
    nk = seq // tk
    pairs = [(a, b) for a in range(nq) for b in range(a // ATTN_KSUB + 1)]
    qi_tab = jnp.asarray(np.array([a for a, _ in pairs], np.int32))
    ki_tab = jnp.asarray(np.array([b for _, b in pairs], np.int32))
    hd2 = 2 * DIFF_HD
    qb, kb = SEG_QD // hd2, SEG_KD // hd2
    v = proj[:, SEG_VD:SEG_VD + DIFF_HEADS * hd2].reshape(batch, seq, DIFF_HEADS, hd2)
    vt = jnp.transpose(v, (0, 2, 3, 1))
    extra = jnp.zeros((batch, DIFF_HEADS, ATTN_VT_ROWS - hd2, seq), BF16).at[:, :, 0, :].set(1.0)
    vt = jnp.concatenate([vt, extra], axis=2).reshape(batch * DIFF_HEADS * ATTN_VT_ROWS, seq)
    vec = pl.BlockSpec((1, DIFF_HD), lambda b, h, p, qi, ki: (0, 0))
    grid_spec = pltpu.PrefetchScalarGridSpec(
        num_scalar_prefetch=2,
        grid=(batch, DIFF_HEADS, len(pairs)),
        in_specs=[
            pl.BlockSpec((t, hd2), lambda b, h, p, qi, ki: (b * nq + qi[p], qb + h)),
            pl.BlockSpec((tk, hd2), lambda b, h, p, qi, ki: (b * nk + ki[p], kb + h)),
            pl.BlockSpec((ATTN_VT_ROWS, tk), lambda b, h, p, qi, ki: (b * DIFF_HEADS + h, ki[p])),
            pl.BlockSpec((1, 2, t, t), lambda b, h, p, qi, ki: (h, 0, 0, 0)),
            vec, vec, vec, vec,
            pl.BlockSpec((hd2, 1), lambda b, h, p, qi, ki: (0, 0)),
        ],
        out_specs=pl.BlockSpec((t, hd2), lambda b, h, p, qi, ki: (b * nq + qi[p], h)),
        scratch_shapes=[
            pltpu.VMEM((hd2, 2 * t), BF16),
            pltpu.VMEM((1, 2 * t), F32),
            pltpu.VMEM((ATTN_VT_ROWS, 2 * t), F32),
            pltpu.VMEM((ATTN_KSUB, 2, t, t), F32),
            pltpu.VMEM((ATTN_KSUB, 2, t, t), BF16),
            pltpu.VMEM((ATTN_KSUB, 2, 8, t), F32),
        ],
    )
    return pl.pallas_call(
        _attn_kernel,
        name="diffattn",
        grid_spec=grid_spec,
        out_shape=jax.ShapeDtypeStruct((n, DIFF_HEADS * hd2), BF16),
        compiler_params=pltpu.CompilerParams(
            dimension_semantics=("arbitrary", "arbitrary", "arbitrary"),
            vmem_limit_bytes=VMEM_LIMIT),
    )(qi_tab, ki_tab, proj, proj, vt, tab, lq1, lk1, lq2, lk2, sg)


def _route(scores, bias, carry):
    e, tm = scores.shape
    neg = -jnp.inf
    choice = scores + bias
    row_g = lax.broadcasted_iota(I32, (GROUP_SIZE, tm), 0)
    group_scores = []
    for g in range(N_GROUPS):
        blk = choice[g * GROUP_SIZE:(g + 1) * GROUP_SIZE]
        m1 = jnp.max(blk, 0, keepdims=True)
        i1 = jnp.min(jnp.where(blk == m1, row_g, GROUP_SIZE), 0, keepdims=True)
        m2 = jnp.max(jnp.where(row_g == i1, neg, blk), 0, keepdims=True)
        group_scores.append(m1 + m2)
    gsc = jnp.concatenate(group_scores, 0)
    row8 = lax.broadcasted_iota(I32, (N_GROUPS, tm), 0)
    okf = jnp.zeros((N_GROUPS, tm), F32)
    for _ in range(TOPK_GROUPS):
        m = jnp.max(gsc, 0, keepdims=True)
        i = jnp.min(jnp.where(gsc == m, row8, N_GROUPS), 0, keepdims=True)
        hit = row8 == i
        okf = jnp.where(hit, 1.0, okf)
        gsc = jnp.where(hit, neg, gsc)
    masked = jnp.concatenate(
        [jnp.where(jnp.broadcast_to(okf[g:g + 1], (GROUP_SIZE, tm)) > 0.0,
                   choice[g * GROUP_SIZE:(g + 1) * GROUP_SIZE], neg)
         for g in range(N_GROUPS)], 0)
    row = lax.broadcasted_iota(I32, (e, tm), 0)
    idxs, ws = [], []
    multi = jnp.zeros((e, tm), F32)
    for _ in range(TOP_K):
        m = jnp.max(masked, 0, keepdims=True)
        i = jnp.min(jnp.where(masked == m, row, e), 0, keepdims=True)
        hit = row == i
        ws.append(jnp.sum(jnp.where(hit, scores, 0.0), 0, keepdims=True))
        idxs.append(i)
        multi = jnp.where(hit, 1.0, multi)
        masked = jnp.where(hit, neg, masked)
    a = lax.broadcasted_iota(I32, (tm, tm), 0)
    b = lax.broadcasted_iota(I32, (tm, tm), 1)
    upper = jnp.where(a < b, 1.0, 0.0).astype(BF16)
    before = _dot(multi.astype(BF16), upper) + carry
    ranks = [jnp.sum(jnp.where(row == i, before, 0.0), 0, keepdims=True).astype(I32) for i in idxs]
    new_carry = carry + jnp.sum(multi, 1, keepdims=True)
    wsum = ws[0]
    for w in ws[1:]:
        wsum = wsum + w
    ws = [w / wsum * ROUTED_SCALE for w in ws]
    return idxs, ws, ranks, new_carry


def _post_kernel(yr_ref, od_ref, g0_ref, g1_ref, h_ref, p_ref, wr_ref, wd_ref, wo_ref, l1g_ref,
                 l1b_ref, sg_ref, su_ref, sd_ref, pw_ref, pg_ref, rwt_ref, rb_ref,
                 h1_ref, base_ref, idx_ref, w_ref, rank_ref, cnt_ref, carry_ref):
    i = pl.program_id(0)

    @pl.when(i == 0)
    def _():
        carry_ref[...] = jnp.zeros_like(carry_ref)

    y_r = _dot(yr_ref[...], wr_ref[...])
    y_d = _dot(od_ref[...], wd_ref[...])
    merged = g0_ref[...].astype(F32) * y_r + g1_ref[...].astype(F32) * y_d
    mix = _dot(merged.astype(BF16), wo_ref[...])
    h1 = _layer_norm(DN_ALPHA * h_ref[...] + mix, l1g_ref[...], l1b_ref[...])
    packed = _pack_bf16_pairs(h1)
    for s in range(PACK_SUB):
        h1_ref[_col_block(0, h1.shape[0], s, PACK_SUB)] = packed[:, s * LANES:(s + 1) * LANES]
    hb = h1.astype(BF16)

    shared = _dot((_silu(_dot(hb, sg_ref[...])) * _dot(hb, su_ref[...])).astype(BF16), sd_ref[...])
    ple = _dot(p_ref[...].astype(BF16), pw_ref[...]) * jax.nn.sigmoid(_dot(hb, pg_ref[...]))
    base_ref[...] = DN_ALPHA * h1 + shared + ple

    h_lo = (h1 - hb.astype(F32)).astype(BF16)
    rw = rwt_ref[...]
    rw_hi = rw.astype(BF16)
    rw_lo = (rw - rw_hi.astype(F32)).astype(BF16)
    logits = _dot_nt(rw_hi, hb) + (_dot_nt(rw_hi, h_lo) + _dot_nt(rw_lo, hb))
    scores = jax.nn.sigmoid(logits)
    idxs, ws, ranks, new_carry = _route(scores, rb_ref[...], carry_ref[...])
    idx_ref[...] = jnp.concatenate(idxs, 0)
    w_ref[...] = jnp.concatenate(ws, 0)
    rank_ref[...] = jnp.concatenate(ranks, 0)
    carry_ref[...] = new_carry
    cnt_ref[...] = new_carry.astype(I32)


def _post_call(yr, od, proj, h, p2, wr, wd, wo, l1g, l1b, sg, su, sd, pw, pg, rwt, rb):
    n = h.shape[0]
    tm = POST_TM
    g0b = SEG_GATE // D_MODEL

    def full(a):
        return pl.BlockSpec(a.shape, lambda i: (0,) * a.ndim)

    row = lambda w: pl.BlockSpec((tm, w), lambda i: (i, 0))
    tok = lambda: pl.BlockSpec((TOP_K, tm), lambda i: (0, i))
    return pl.pallas_call(
        _post_kernel,
        name="post",
        grid=(n // tm,),
        in_specs=[
            row(RET_HEADS * RET_V), row(D_MODEL),
            pl.BlockSpec((tm, D_MODEL), lambda i: (i, g0b)),
            pl.BlockSpec((tm, D_MODEL), lambda i: (i, g0b + 1)),
            row(D_MODEL), row(PLE_DIM),
            full(wr), full(wd), full(wo), full(l1g), full(l1b), full(sg), full(su), full(sd),
            full(pw), full(pg), full(rwt), full(rb),
        ],
        out_specs=[pl.BlockSpec((tm * PACK_SUB, LANES), lambda i: (i, 0)), row(D_MODEL),
                   tok(), tok(), tok(), pl.BlockSpec((N_EXPERTS, 1), lambda i: (0, 0))],
        out_shape=[
            jax.ShapeDtypeStruct((n * PACK_SUB, LANES), jnp.uint32),
            jax.ShapeDtypeStruct((n, D_MODEL), F32),
            jax.ShapeDtypeStruct((TOP_K, n), I32),
            jax.ShapeDtypeStruct((TOP_K, n), F32),
            jax.ShapeDtypeStruct((TOP_K, n), I32),
            jax.ShapeDtypeStruct((N_EXPERTS, 1), I32),
        ],
        scratch_shapes=[pltpu.VMEM((N_EXPERTS, 1), F32)],
        compiler_params=pltpu.CompilerParams(
            dimension_semantics=("arbitrary",), vmem_limit_bytes=VMEM_LIMIT),
    )(yr, od, proj, proj, h, p2, wr, wd, wo, l1g, l1b, sg, su, sd, pw, pg, rwt, rb)


def _dest_kernel(idx_ref, rank_ref, ps_ref, o_ref):
    idx = idx_ref[...]
    ps = ps_ref[...]
    td = idx.shape[1]
    row = lax.broadcasted_iota(I32, (N_EXPERTS, td), 0)
    outs = []
    for k in range(TOP_K):
        sel = jnp.sum(jnp.where(row == idx[k:k + 1], ps, 0.0), 0, keepdims=True)
        outs.append(sel.astype(I32))
    o_ref[...] = jnp.concatenate(outs, 0) + rank_ref[...]


def _dest_call(idx, rank, pstart_f):
    n = idx.shape[1]
    td = min(DEST_T, n)
    spec = pl.BlockSpec((TOP_K, td), lambda i: (0, i))
    return pl.pallas_call(
        _dest_kernel,
        name="dest",
        grid=(n // td,),
        in_specs=[spec, spec, pl.BlockSpec((N_EXPERTS, 1), lambda i: (0, 0))],
        out_specs=spec,
        out_shape=jax.ShapeDtypeStruct((TOP_K, n), I32),
        compiler_params=pltpu.CompilerParams(dimension_semantics=("arbitrary",)),
    )(idx, rank, pstart_f)


def _expert_kernel(be_ref, nu_ref, wslot_ref, nexte_ref, rt_ref, rtn_ref, h_hbm, wg_hbm, wu_hbm,
                   wd_hbm, y_ref, buf_ref, wgf_ref, wuf_ref, wdf_ref, wgb_ref, wub_ref, wdb_ref,
                   sem, wsem):
    i = pl.program_id(0)
    g = EXPERT_BLOCK

    def weight_copies(e, s):
        return [pltpu.make_async_copy(src.at[e], dst.at[s], wsem.at[s])
                for src, dst in ((wg_hbm, wgf_ref), (wu_hbm, wuf_ref), (wd_hbm, wdf_ref))]

    def wait(s):
        pltpu.make_async_copy(_row_span(h_hbm, 0, g, PACK_SUB),
                              _row_span(buf_ref, s * g, g, PACK_SUB), sem.at[s]).wait()

    @pl.when(i == 0)
    def _():
        for j in range(GATHER_AHEAD):
            def body(r, c, j=j):
                pltpu.make_async_copy(_one_row(h_hbm, rt_ref[j, 0, r], PACK_SUB),
                                      _one_row(buf_ref, j * g + r, PACK_SUB),
                                      sem.at[j]).start(priority=GATHER_DMA_PRIORITY)
                return c
            lax.fori_loop(0, g, body, 0, unroll=8)

        for c in weight_copies(be_ref[0], 0):
            c.start(priority=WEIGHT_DMA_PRIORITY)

    used = i < nu_ref[0]
    first = jnp.logical_and(
        used, jnp.logical_or(i == 0, be_ref[i] != be_ref[jnp.maximum(i - 1, 0)]))
    for s in range(2):
        @pl.when(jnp.logical_and(first, wslot_ref[i] == s))
        def _(s=s):
            for c in weight_copies(be_ref[i], s):
                c.wait()
            wgb_ref[...] = wgf_ref[s].astype(BF16)
            wub_ref[...] = wuf_ref[s].astype(BF16)
            wdb_ref[...] = wdf_ref[s].astype(BF16)

            @pl.when(nexte_ref[i] >= 0)
            def _():
                for c in weight_copies(nexte_ref[i], 1 - s):
                    c.start(priority=WEIGHT_DMA_PRIORITY)

    def block(slot, more):
        wait(slot)
        halves = [_unpack_bf16_pairs(buf_ref[_col_block(slot * g, g, s, PACK_SUB)])
                  for s in range(PACK_SUB)]
        x = jnp.concatenate([lo for lo, _ in halves] + [hi for _, hi in halves],
                            axis=1).astype(BF16)
        if more:
            ahead = (slot + GATHER_AHEAD) % GATHER_SLOTS
            for r in range(g):
                pltpu.make_async_copy(
                    _one_row(h_hbm, rtn_ref[0, 0, r], PACK_SUB),
                    _one_row(buf_ref, ahead * g + r, PACK_SUB),
                    sem.at[ahead]).start(priority=r % DMA_QUEUES)
        act = _silu(_dot(x, wgb_ref[...])) * _dot(x, wub_ref[...])
        packed = _pack_bf16_pairs(_dot(act.astype(BF16), wdb_ref[...]))
        for s in range(PACK_SUB):
            y_ref[_col_block(0, g, s, PACK_SUB)] = packed[:, s * LANES:(s + 1) * LANES]

    more = i + GATHER_AHEAD < nu_ref[0]
    for slot in range(GATHER_SLOTS):
        in_slot = jnp.logical_and(i % GATHER_SLOTS == slot, used)
        pl.when(jnp.logical_and(in_slot, more))(functools.partial(block, slot, True))
        pl.when(jnp.logical_and(in_slot, jnp.logical_not(more)))(
            functools.partial(block, slot, False))

    @pl.when(jnp.logical_not(used))
    def _():
        y_ref[...] = jnp.zeros_like(y_ref)


def _expert_call(block_e, n_used, w_slot, next_e, row_tok3, h1, wg, wu, wd):
    nb = block_e.shape[0]
    g = EXPERT_BLOCK
    grid_spec = pltpu.PrefetchScalarGridSpec(
        num_scalar_prefetch=4,
        grid=(nb,),
        in_specs=[
            pl.BlockSpec((GATHER_AHEAD, 1, g), lambda i, *_: (0, 0, 0), memory_space=pltpu.SMEM),
            pl.BlockSpec((1, 1, g), lambda i, *_: (jnp.minimum(i + GATHER_AHEAD, nb - 1), 0, 0),
                         memory_space=pltpu.SMEM),
            pl.BlockSpec(memory_space=pl.ANY),
            pl.BlockSpec(memory_space=pl.ANY),
            pl.BlockSpec(memory_space=pl.ANY),
            pl.BlockSpec(memory_space=pl.ANY),
        ],
        out_specs=pl.BlockSpec((g * PACK_SUB, LANES), lambda i, *_: (i, 0)),
        scratch_shapes=[
            pltpu.VMEM((GATHER_SLOTS * g * PACK_SUB, LANES), jnp.uint32),
            pltpu.VMEM((2, D_MODEL, D_EXPERT), F32),
            pltpu.VMEM((2, D_MODEL, D_EXPERT), F32),
            pltpu.VMEM((2, D_EXPERT, D_MODEL), F32),
            pltpu.VMEM((D_MODEL, D_EXPERT), BF16),
            pltpu.VMEM((D_MODEL, D_EXPERT), BF16),
            pltpu.VMEM((D_EXPERT, D_MODEL), BF16),
            pltpu.SemaphoreType.DMA((GATHER_SLOTS,)),
            pltpu.SemaphoreType.DMA((2,)),
        ],
    )
    assert nb > GATHER_AHEAD
    return pl.pallas_call(
        _expert_kernel,
        name="experts",
        grid_spec=grid_spec,
        out_shape=jax.ShapeDtypeStruct((nb * g * PACK_SUB, LANES), jnp.uint32),
        compiler_params=pltpu.CompilerParams(
            dimension_semantics=("arbitrary",), vmem_limit_bytes=VMEM_LIMIT),
    )(block_e, n_used, w_slot, next_e, row_tok3, row_tok3, h1, wg, wu, wd)


def _combine_kernel(d_ref, dn_ref, y_hbm, w_ref, base_ref, g_ref, b_ref, o_ref, buf_ref, sem):
    i = pl.program_id(0)
    nt = pl.num_programs(0)
    tt = COMB_T

    def plane(s, k):
        return (s * TOP_K + k) * tt

    def wait(s):
        for k in range(TOP_K):
            pltpu.make_async_copy(_row_span(y_hbm, 0, tt, PACK_SUB),
                                  _row_span(buf_ref, plane(s, k), tt, PACK_SUB), sem.at[s]).wait()

    @pl.when(i == 0)
    def _():
        for j in range(GATHER_AHEAD):
            def body(t, c, j=j):
                for k in range(TOP_K):
                    pltpu.make_async_copy(
                        _one_row(y_hbm, d_ref[j, k, t], PACK_SUB),
                        _one_row(buf_ref, plane(j, k) + t, PACK_SUB),
                        sem.at[j]).start(priority=k % DMA_QUEUES)
                return c
            lax.fori_loop(0, tt, body, 0)

    def tile(slot):
        wait(slot)
        ahead = (slot + GATHER_AHEAD) % GATHER_SLOTS
        for t in range(tt):
            for k in range(TOP_K):
                pltpu.make_async_copy(_one_row(y_hbm, dn_ref[0, k, t], PACK_SUB),
                                      _one_row(buf_ref, plane(ahead, k) + t, PACK_SUB),
                                      sem.at[ahead]).start(priority=k % DMA_QUEUES)
        w = w_ref[...]
        lows, highs = [], []
        for s in range(PACK_SUB):
            lo_acc = hi_acc = None
            for k in range(TOP_K):
                lo, hi = _unpack_bf16_pairs(
                    buf_ref[_col_block(plane(slot, k), tt, s, PACK_SUB)])
                wk = w[:, k:k + 1]
                lo_acc = wk * lo if lo_acc is None else lo_acc + wk * lo
                hi_acc = wk * hi if hi_acc is None else hi_acc + wk * hi
            lows.append(lo_acc)
            highs.append(hi_acc)
        routed = jnp.concatenate(lows + highs, axis=1)
        o_ref[...] = _layer_norm(base_ref[...] + routed, g_ref[...], b_ref[...])

        @pl.when(i == nt - 1)
        def _():
            for j in range(1, GATHER_AHEAD + 1):
                wait((slot + j) % GATHER_SLOTS)

    for slot in range(GATHER_SLOTS):
        pl.when(i % GATHER_SLOTS == slot)(functools.partial(tile, slot))


def _combine_call(dest3, y, w_t, base, ln_g, ln_b):
    n = base.shape[0]
    tt = COMB_T
    nt = n // tt
    return pl.pallas_call(
        _combine_kernel,
        name="combine",
        grid=(nt,),
        in_specs=[
            pl.BlockSpec((GATHER_AHEAD, TOP_K, tt), lambda i: (0, 0, 0), memory_space=pltpu.SMEM),
            pl.BlockSpec((1, TOP_K, tt), lambda i: (jnp.minimum(i + GATHER_AHEAD, nt - 1), 0, 0),
                         memory_space=pltpu.SMEM),
            pl.BlockSpec(memory_space=pl.ANY),
            pl.BlockSpec((tt, TOP_K), lambda i: (i, 0)),
            pl.BlockSpec((tt, D_MODEL), lambda i: (i, 0)),
            pl.BlockSpec((1, D_MODEL), lambda i: (0, 0)),
            pl.BlockSpec((1, D_MODEL), lambda i: (0, 0)),
        ],
        out_specs=pl.BlockSpec((tt, D_MODEL), lambda i: (i, 0)),
        out_shape=jax.ShapeDtypeStruct((n, D_MODEL), F32),
        scratch_shapes=[
            pltpu.VMEM((GATHER_SLOTS * TOP_K * tt * PACK_SUB, LANES), jnp.uint32),
            pltpu.SemaphoreType.DMA((GATHER_SLOTS,)),
        ],
        compiler_params=pltpu.CompilerParams(
            dimension_semantics=("arbitrary",), vmem_limit_bytes=VMEM_LIMIT),
    )(dest3, dest3, y, w_t, base, ln_g, ln_b)


def _xpos_tables(seq):
    half = RET_QK // 2
    pos = jnp.arange(seq, dtype=F32)
    theta = 1.0 / (XPOS_BASE ** jnp.linspace(0.0, 1.0, half, dtype=F32))
    ang = pos[:, None] * theta[None, :]
    cos = jnp.cos(ang)
    sin = jnp.sin(ang)
    return jnp.concatenate([cos, cos], -1), jnp.concatenate([-sin, sin], -1)


def _decay_tables():
    c = RET_CHUNK
    log_gamma = jnp.log(1.0 - 2.0 ** (-5.0 - jnp.arange(RET_HEADS, dtype=F32)))
    idx = jnp.arange(c, dtype=F32)
    rel = idx[:, None] - idx[None, :]
    inner = jnp.where(rel >= 0, jnp.exp(jnp.maximum(rel, 0.0)[None] * log_gamma[:, None, None]), 0.0)
    q_dec = jnp.exp((idx + 1.0)[None, :] * log_gamma[:, None])
    k_dec = jnp.exp((c - 1.0 - idx)[None, :] * log_gamma[:, None])
    c_dec = jnp.exp(c * log_gamma)
    cdec = jnp.broadcast_to(c_dec[:, None, None], (RET_HEADS, 1, RET_V))
    return inner, q_dec[:, :, None], k_dec[:, :, None], cdec


def _t5_bucket(n):
    max_exact = REL_BUCKETS // 2
    nf = jnp.maximum(n, 1).astype(F32)
    large = max_exact + (jnp.log(nf / max_exact) / math.log(REL_MAX_DIST / max_exact)
                         * (REL_BUCKETS - max_exact)).astype(I32)
    large = jnp.minimum(large, REL_BUCKETS - 1)
    return jnp.where(n < max_exact, n, large)


def _bias_tables(rel_bias, t):
    c = REL_MAX_DIST
    nb = t // c
    far = rel_bias[REL_BUCKETS - 1].astype(F32)
    d = jnp.arange(c)[None, :] - jnp.arange(c)[:, None]
    def lookup(dist):
        onehot = jax.nn.one_hot(_t5_bucket(dist), REL_BUCKETS, dtype=F32)
        return jnp.einsum('ijk,kh->ijh', onehot, rel_bias.astype(F32),
                          precision=lax.Precision.HIGHEST)

    near0 = (lookup(jnp.maximum(d, 0)) - far) * LOG2E
    near0 = jnp.where((d >= 0)[:, :, None], near0, MASK_VALUE)
    near1 = (lookup(d + c) - far) * LOG2E
    tile0 = jnp.tile(near0, (nb, nb, 1))
    tile1 = jnp.tile(near1, (nb, nb, 1))
    kb = (jnp.arange(t) // c)[:, None, None]
    qb = (jnp.arange(t) // c)[None, :, None]
    diag = jnp.where(qb == kb, tile0,
                     jnp.where(qb == kb + 1, tile1, jnp.where(qb > kb, 0.0, MASK_VALUE)))
    left = jnp.where((qb == 0) & (kb == nb - 1), tile1, 0.0)
    return jnp.transpose(jnp.stack([diag, left], 0), (3, 0, 1, 2))


def kernel(x, p, ln_in_g, ln_in_b, rel_bias, w_in, ret_gn_g, lambda_q1, lambda_k1, lambda_q2,
           lambda_k2, diff_subln_g, w_ret_o, w_diff_o, w_o, ln1_g, ln1_b, router_w, router_bias,
           exp_w_gate, exp_w_up, exp_w_down, sh_w_gate, sh_w_up, sh_w_down, ple_w, ple_gate_w,
           ln2_g, ln2_b):
    batch, seq, d = x.shape
    n = batch * seq
    assert d == D_MODEL and seq % (ATTN_T * ATTN_KSUB) == 0 and seq % PROJ_TM == 0
    assert n % min(DEST_T, n) == 0 and n % COMB_T == 0 and n % POST_TM == 0
    x2 = x.reshape(n, d)
    row = lambda a: a.reshape(1, -1).astype(F32)

    cosf, sinf = _xpos_tables(seq)
    inner, qdec, kdec, cdec = _decay_tables()
    h, proj = _proj_call(x2, row(ln_in_g), row(ln_in_b), cosf, sinf, w_in[0].astype(BF16), seq)

    yr = _ret_call(proj, inner, qdec, kdec, cdec,
                   ret_gn_g[0].reshape(RET_HEADS, 1, RET_V).astype(F32), batch, seq)

    od = _attn_call(proj, _bias_tables(rel_bias, ATTN_T), row(lambda_q1[0]), row(lambda_k1[0]),
                    row(lambda_q2[0]), row(lambda_k2[0]),
                    diff_subln_g[0].reshape(-1, 1).astype(F32), batch, seq)

    h1, base, idx, w, rank, counts = _post_call(
        yr, od, proj, h, p[0].reshape(n, PLE_DIM),
        w_ret_o[0].astype(BF16), w_diff_o[0].astype(BF16), w_o[0].astype(BF16),
        row(ln1_g[0]), row(ln1_b[0]),
        sh_w_gate[0].astype(BF16), sh_w_up[0].astype(BF16), sh_w_down[0].astype(BF16),
        ple_w[0].astype(BF16), ple_gate_w[0].astype(BF16),
        router_w[0].T.astype(F32), router_bias[0].reshape(N_EXPERTS, 1).astype(F32))

    g = EXPERT_BLOCK
    counts = counts.reshape(N_EXPERTS)
    padded = (counts + g - 1) // g * g
    pend = jnp.cumsum(padded)
    pstart = pend - padded
    n_rows = -(-(n * TOP_K + N_EXPERTS * (g - 1)) // g) * g
    n_blocks = n_rows // g
    block_start = jnp.arange(n_blocks, dtype=I32) * g
    block_e = jnp.minimum(
        jnp.sum((pend[None, :] <= block_start[:, None]).astype(I32), axis=1),
        N_EXPERTS - 1).astype(I32)

    dest = _dest_call(idx, rank, pstart.astype(F32).reshape(N_EXPERTS, 1))
    tok = jnp.broadcast_to(jnp.arange(n, dtype=I32)[None, :], (TOP_K, n))
    row_tok = jnp.zeros((n_rows,), I32).at[dest.reshape(-1)].set(
        tok.reshape(-1), unique_indices=True, mode='drop')

    n_used = (pend[N_EXPERTS - 1:] // g).astype(I32)
    assert (n * TOP_K) // g >= GATHER_AHEAD
    present = padded > 0
    ids = jnp.arange(N_EXPERTS, dtype=I32)
    at_or_after = lax.cummin(jnp.where(present, ids, N_EXPERTS)[::-1])[::-1]
    after = jnp.concatenate([at_or_after[1:], jnp.full((1,), N_EXPERTS, I32)])
    after = jnp.where(after >= N_EXPERTS, -1, after)
    order = jnp.cumsum(present.astype(I32)) - 1
    w_slot = (jnp.take(order, block_e) % 2).astype(I32)
    next_e = jnp.take(after, block_e).astype(I32)
    y = _expert_call(block_e, n_used, w_slot, next_e, row_tok.reshape(n_blocks, 1, g), h1,
                     exp_w_gate[0], exp_w_up[0], exp_w_down[0])

    dest3 = dest.reshape(TOP_K, n // COMB_T, COMB_T).transpose(1, 0, 2)
    out = _combine_call(dest3, y, w.T, base, row(ln2_g[0]), row(ln2_b[0]))
    return out.reshape(batch, seq, d)
```

```python
import functools
import math

import numpy as np
import jax
import jax.numpy as jnp
from jax import lax
from jax.experimental import pallas as pl
from jax.experimental.pallas import tpu as pltpu

F32 = jnp.float32
BF16 = jnp.bfloat16
I32 = jnp.int32

D_MODEL = 1024
PLE_DIM = 256
RET_HEADS = 8
RET_QK = 128
RET_V = 256
RET_CHUNK = 128
XPOS_BASE = 10000.0
DIFF_HEADS = 8
DIFF_HD = 64
REL_BUCKETS = 32
REL_MAX_DIST = 128
N_EXPERTS = 256
TOP_K = 8
N_GROUPS = 8
GROUP_SIZE = N_EXPERTS // N_GROUPS
TOPK_GROUPS = 4
D_EXPERT = 256
ROUTED_SCALE = 2.5
EXPERT_BLOCK = 128
DEPTH = 1
DN_ALPHA = (2.0 * DEPTH) ** 0.25
EPS = 1e-5
LAM_INIT = 0.8 - 0.6 * math.exp(-0.3 * 0)

SEG_QR, SEG_KR, SEG_VR, SEG_GR, SEG_QD, SEG_KD, SEG_VD, SEG_GATE = (
    0, 1024, 2048, 4096, 6144, 7168, 8192, 9216)
IN_COLS = 11264

LANES = 128
ROW_SUB = D_MODEL // LANES
PACK_SUB = ROW_SUB // 2
MASK_VALUE = -1e30
LOG2E = math.log2(math.e)
VMEM_LIMIT = 56 * 1024 * 1024

RET_SUB = 4
PROJ_TM = 1024
PROJ_TN = 1024
ATTN_T = 512
ATTN_KSUB = 2
ATTN_VT_ROWS = 2 * DIFF_HD + 16
POST_TM = 512
DEST_T = 2048
COMB_T = 32
DMA_QUEUES = 2
GATHER_DMA_PRIORITY = 1
WEIGHT_DMA_PRIORITY = 1
GATHER_AHEAD = 3
GATHER_SLOTS = GATHER_AHEAD + 1


def _layer_norm(x, g, b):
    mu = jnp.mean(x, -1, keepdims=True)
    d = x - mu
    var = jnp.mean(d * d, -1, keepdims=True)
    return d * lax.rsqrt(var + EPS) * g + b


def _silu(x):
    return x * jax.nn.sigmoid(x)


def _dot(a, b):
    return jnp.dot(a, b, preferred_element_type=F32)


def _tree(op, items):
    items = list(items)
    while len(items) > 1:
        items = [op(items[i], items[i + 1]) if i + 1 < len(items) else items[i]
                 for i in range(0, len(items), 2)]
    return items[0]


def _col_block(first_row, rows, s, sub=ROW_SUB):
    return (pl.ds(first_row * sub + s, rows, stride=sub), slice(None))


def _one_row(ref, row, sub=ROW_SUB):
    if isinstance(row, int):
        return ref.at[pl.ds(row * sub, sub), :]
    return ref.at[pl.ds(pl.multiple_of(row * sub, sub), sub), :]


def _row_span(ref, first_row, rows, sub=ROW_SUB):
    return ref.at[pl.ds(first_row * sub, rows * sub), :]


def _pack_bf16_pairs(y):
    half = y.shape[1] // 2
    lo = lax.bitcast_convert_type(y[:, :half].astype(BF16).astype(F32), jnp.uint32)
    hi = lax.bitcast_convert_type(y[:, half:].astype(BF16).astype(F32), jnp.uint32)
    return (lo >> 16) | hi


def _unpack_bf16_pairs(u):
    lo = lax.bitcast_convert_type(u << 16, F32)
    hi = lax.bitcast_convert_type(u & jnp.uint32(0xFFFF0000), F32)
    return lo, hi


def _dot_nt(a, b):
    return lax.dot_general(a, b, (((1,), (1,)), ((), ())), preferred_element_type=F32)


def _proj_kernel(x_ref, g_ref, b_ref, cos_ref, sin_ref, w_ref, h_ref, o_ref, hb_ref):
    j = pl.program_id(1)

    @pl.when(j == 0)
    def _():
        h = _layer_norm(x_ref[...], g_ref[...], b_ref[...])
        h_ref[...] = h
        hb_ref[...] = h.astype(BF16)

    acc = _dot(hb_ref[...], w_ref[...])

    def rotate(scale):
        c = cos_ref[...]
        s = sin_ref[...]
        for hh in range(PROJ_TN // RET_QK):
            t = acc[:, hh * RET_QK:(hh + 1) * RET_QK]
            v = t * c + pltpu.roll(t, RET_QK // 2, 1) * s
            if scale != 1.0:
                v = v * scale
            o_ref[:, hh * RET_QK:(hh + 1) * RET_QK] = v.astype(BF16)

    @pl.when(j == SEG_QR // PROJ_TN)
    def _():
        rotate(1.0)

    @pl.when(j == SEG_KR // PROJ_TN)
    def _():
        rotate(RET_QK ** -0.5)

    is_gr = (j >= SEG_GR // PROJ_TN) & (j < SEG_QD // PROJ_TN)
    is_qd = j == SEG_QD // PROJ_TN
    is_gate = j >= SEG_GATE // PROJ_TN
    is_plain = ((j >= SEG_VR // PROJ_TN) & (j < SEG_GR // PROJ_TN)) | (
        (j >= SEG_KD // PROJ_TN) & (j < SEG_GATE // PROJ_TN))

    @pl.when(is_plain)
    def _():
        o_ref[...] = acc.astype(BF16)

    @pl.when(is_gr)
    def _():
        o_ref[...] = _silu(acc).astype(BF16)

    @pl.when(is_qd)
    def _():
        o_ref[...] = (acc * (DIFF_HD ** -0.5 * LOG2E)).astype(BF16)

    @pl.when(is_gate)
    def _():
        o_ref[...] = jax.nn.sigmoid(acc).astype(BF16)


def _proj_call(x2, ln_g, ln_b, cosf, sinf, w_bf, seq):
    n = x2.shape[0]
    tm, tn = PROJ_TM, PROJ_TN
    s_tiles = seq // tm
    return pl.pallas_call(
        _proj_kernel,
        name="proj",
        grid=(n // tm, IN_COLS // tn),
        in_specs=[
            pl.BlockSpec((tm, D_MODEL), lambda i, j: (i, 0)),
            pl.BlockSpec((1, D_MODEL), lambda i, j: (0, 0)),
            pl.BlockSpec((1, D_MODEL), lambda i, j: (0, 0)),
            pl.BlockSpec((tm, RET_QK), lambda i, j: (i % s_tiles, 0)),
            pl.BlockSpec((tm, RET_QK), lambda i, j: (i % s_tiles, 0)),
            pl.BlockSpec((D_MODEL, tn), lambda i, j: (0, j)),
        ],
        out_specs=[
            pl.BlockSpec((tm, D_MODEL), lambda i, j: (i, 0)),
            pl.BlockSpec((tm, tn), lambda i, j: (i, j)),
        ],
        out_shape=[
            jax.ShapeDtypeStruct((n, D_MODEL), F32),
            jax.ShapeDtypeStruct((n, IN_COLS), BF16),
        ],
        scratch_shapes=[pltpu.VMEM((tm, D_MODEL), BF16)],
        compiler_params=pltpu.CompilerParams(
            dimension_semantics=("arbitrary", "arbitrary"), vmem_limit_bytes=VMEM_LIMIT),
    )(x2, ln_g, ln_b, cosf, sinf, w_bf)


def _ret_kernel(q_ref, k_ref, v_ref, g_ref, inner_ref, qdec_ref, kdec_ref, cdec_ref, gn_ref,
                o_ref, state_ref):
    c = pl.program_id(2)

    @pl.when(c == 0)
    def _():
        state_ref[...] = jnp.zeros_like(state_ref)

    st = state_ref[...]
    for j in range(RET_SUB):
        rows = slice(j * RET_CHUNK, (j + 1) * RET_CHUNK)
        q = q_ref[rows, :]
        k = k_ref[rows, :]
        v = v_ref[rows, :]
        scores = _dot_nt(q, k) * inner_ref[0]
        o = _dot(scores.astype(BF16), v) + qdec_ref[0] * _dot(q, st.astype(BF16))
        kd = (k.astype(F32) * kdec_ref[0]).astype(BF16)
        kv = lax.dot_general(kd, v, (((0,), (0,)), ((), ())), preferred_element_type=F32)
        st = cdec_ref[0] * st + kv

        mu = jnp.mean(o, -1, keepdims=True)
        d = o - mu
        var = jnp.mean(d * d, -1, keepdims=True)
        on = d * lax.rsqrt(var + EPS) * gn_ref[0]
        o_ref[rows, :] = (g_ref[rows, :].astype(F32) * on).astype(BF16)
    state_ref[...] = st


def _ret_call(proj, inner, qdec, kdec, cdec, gn, batch, seq):
    n = proj.shape[0]
    c = RET_CHUNK
    rows = RET_SUB * c
    nc = seq // rows
    qb = SEG_QR // RET_QK
    kb = SEG_KR // RET_QK
    vb = SEG_VR // RET_V
    gb = SEG_GR // RET_V
    return pl.pallas_call(
        _ret_kernel,
        name="retention",
        grid=(batch, RET_HEADS, nc),
        in_specs=[
            pl.BlockSpec((rows, RET_QK), lambda b, h, i: (b * nc + i, qb + h)),
            pl.BlockSpec((rows, RET_QK), lambda b, h, i: (b * nc + i, kb + h)),
            pl.BlockSpec((rows, RET_V), lambda b, h, i: (b * nc + i, vb + h)),
            pl.BlockSpec((rows, RET_V), lambda b, h, i: (b * nc + i, gb + h)),
            pl.BlockSpec((1, c, c), lambda b, h, i: (h, 0, 0)),
            pl.BlockSpec((1, c, 1), lambda b, h, i: (h, 0, 0)),
            pl.BlockSpec((1, c, 1), lambda b, h, i: (h, 0, 0)),
            pl.BlockSpec((1, 1, RET_V), lambda b, h, i: (h, 0, 0)),
            pl.BlockSpec((1, 1, RET_V), lambda b, h, i: (h, 0, 0)),
        ],
        out_specs=pl.BlockSpec((rows, RET_V), lambda b, h, i: (b * nc + i, h)),
        out_shape=jax.ShapeDtypeStruct((n, RET_HEADS * RET_V), BF16),
        scratch_shapes=[pltpu.VMEM((RET_QK, RET_V), F32)],
        compiler_params=pltpu.CompilerParams(
            dimension_semantics=("arbitrary", "arbitrary", "arbitrary"),
            vmem_limit_bytes=VMEM_LIMIT),
    )(proj, proj, proj, proj, inner, qdec, kdec, cdec, gn)


def _attn_kernel(qi_ref, ki_ref, q_ref, k_ref, vt_ref, tab_ref, lq1_ref, lk1_ref, lq2_ref,
                 lk2_ref, sg_ref, o_ref, qst_ref, m_ref, acc_ref, s_ref, p_ref, mx_ref):
    t = ATTN_T
    hd2 = 2 * DIFF_HD
    p = pl.program_id(2)
    qi = qi_ref[p]
    ki = ki_ref[p]

    @pl.when(ki == 0)
    def _():
        qt = q_ref[...].astype(F32).T
        sub = lax.broadcasted_iota(I32, qt.shape, 0)
        qst_ref[:, 0:t] = jnp.where(sub < DIFF_HD, qt, 0.0).astype(BF16)
        qst_ref[:, t:2 * t] = jnp.where(sub >= DIFF_HD, qt, 0.0).astype(BF16)
        m_ref[...] = jnp.full_like(m_ref, MASK_VALUE)
        acc_ref[...] = jnp.zeros_like(acc_ref)

    def steps(*biases):
        for sub, bias in enumerate(biases):
            kk = k_ref[sub * t:(sub + 1) * t, :]
            for half in range(2):
                cols = slice(half * t, (half + 1) * t)
                s = _dot(kk, qst_ref[:, cols])
                if bias is not None:
                    s = s + tab_ref[0, bias]
                s_ref[sub, half] = s
                mx_ref[sub, half] = _tree(jnp.maximum, [s[r:r + 8, :] for r in range(0, t, 8)])
        for sub in range(len(biases)):
            vt = vt_ref[:, sub * t:(sub + 1) * t]
            for half in range(2):
                cols = slice(half * t, (half + 1) * t)
                m_prev = m_ref[:, cols]
                m_new = jnp.maximum(m_prev, jnp.max(mx_ref[sub, half], 0, keepdims=True))
                alpha = jnp.exp2(m_prev - m_new)
                for r in range(0, t, 64):
                    p_ref[sub, half, r:r + 64, :] = jnp.exp2(
                        s_ref[sub, half, r:r + 64, :] - m_new).astype(BF16)
                acc_ref[:, cols] = alpha * acc_ref[:, cols] + _dot(vt, p_ref[sub, half])
                m_ref[:, cols] = m_new

    def finalize():
        lam = (jnp.exp(jnp.sum(lq1_ref[...] * lk1_ref[...], -1, keepdims=True))
               - jnp.exp(jnp.sum(lq2_ref[...] * lk2_ref[...], -1, keepdims=True)) + LAM_INIT)
        o1 = acc_ref[0:hd2, 0:t] / acc_ref[hd2:hd2 + 1, 0:t]
        o2 = acc_ref[0:hd2, t:2 * t] / acc_ref[hd2:hd2 + 1, t:2 * t]
        o = o1 - lam * o2
        o = o * lax.rsqrt(jnp.mean(o * o, 0, keepdims=True) + EPS) * (sg_ref[...] * (1.0 - LAM_INIT))
        o_ref[...] = o.T.astype(BF16)

    assert ATTN_KSUB == 2
    last = ki * ATTN_KSUB + 1
    pl.when(last < qi - 1)(functools.partial(steps, None, None))
    pl.when(last == qi - 1)(functools.partial(steps, None, 1))

    @pl.when(last == qi)
    def _():
        steps(1, 0)
        finalize()

    @pl.when(last == qi + 1)
    def _():
        steps(0)
        finalize()


def _attn_call(proj, tab, lq1, lk1, lq2, lk2, sg, batch, seq):
    n = proj.shape[0]
    t = ATTN_T
    nq = seq // t
    tk = ATTN_KSUB * t
    nk = seq // tk
    pairs = [(a, b) for a in range(nq) for b in range(a // ATTN_KSUB + 1)]
    qi_tab = jnp.asarray(np.array([a for a, _ in pairs], np.int32))
    ki_tab = jnp.asarray(np.array([b for _, b in pairs], np.int32))
    hd2 = 2 * DIFF_HD
    qb, kb = SEG_QD // hd2, SEG_KD // hd2
    v = proj[:, SEG_VD:SEG_VD + DIFF_HEADS * hd2].reshape(batch, seq, DIFF_HEADS, hd2)
    vt = jnp.transpose(v, (0, 2, 3, 1))
    extra = jnp.zeros((batch, DIFF_HEADS, ATTN_VT_ROWS - hd2, seq), BF16).at[:, :, 0, :].set(1.0)
    vt = jnp.concatenate([vt, extra], axis=2).reshape(batch * DIFF_HEADS * ATTN_VT_ROWS, seq)
    vec = pl.BlockSpec((1, DIFF_HD), lambda b, h, p, qi, ki: (0, 0))
    grid_spec = pltpu.PrefetchScalarGridSpec(
        num_scalar_prefetch=2,
        grid=(batch, DIFF_HEADS, len(pairs)),
        in_specs=[
            pl.BlockSpec((t, hd2), lambda b, h, p, qi, ki: (b * nq + qi[p], qb + h)),
            pl.BlockSpec((tk, hd2), lambda b, h, p, qi, ki: (b * nk + ki[p], kb + h)),
            pl.BlockSpec((ATTN_VT_ROWS, tk), lambda b, h, p, qi, ki: (b * DIFF_HEADS + h, ki[p])),
            pl.BlockSpec((1, 2, t, t), lambda b, h, p, qi, ki: (h, 0, 0, 0)),
            vec, vec, vec, vec,
            pl.BlockSpec((hd2, 1), lambda b, h, p, qi, ki: (0, 0)),
        ],
        out_specs=pl.BlockSpec((t, hd2), lambda b, h, p, qi, ki: (b * nq + qi[p], h)),
        scratch_shapes=[
            pltpu.VMEM((hd2, 2 * t), BF16),
            pltpu.VMEM((1, 2 * t), F32),
            pltpu.VMEM((ATTN_VT_ROWS, 2 * t), F32),
            pltpu.VMEM((ATTN_KSUB, 2, t, t), F32),
            pltpu.VMEM((ATTN_KSUB, 2, t, t), BF16),
            pltpu.VMEM((ATTN_KSUB, 2, 8, t), F32),
        ],
    )
    return pl.pallas_call(
        _attn_kernel,
        name="diffattn",
        grid_spec=grid_spec,
        out_shape=jax.ShapeDtypeStruct((n, DIFF_HEADS * hd2), BF16),
        compiler_params=pltpu.CompilerParams(
            dimension_semantics=("arbitrary", "arbitrary", "arbitrary"),
            vmem_limit_bytes=VMEM_LIMIT),
    )(qi_tab, ki_tab, proj, proj, vt, tab, lq1, lk1, lq2, lk2, sg)


def _route(scores, bias, carry):
    e, tm = scores.shape
    neg = -jnp.inf
    choice = scores + bias
    row_g = lax.broadcasted_iota(I32, (GROUP_SIZE, tm), 0)
    group_scores = []
    for g in range(N_GROUPS):
        blk = choice[g * GROUP_SIZE:(g + 1) * GROUP_SIZE]
        m1 = jnp.max(blk, 0, keepdims=True)
        i1 = jnp.min(jnp.where(blk == m1, row_g, GROUP_SIZE), 0, keepdims=True)
        m2 = jnp.max(jnp.where(row_g == i1, neg, blk), 0, keepdims=True)
        group_scores.append(m1 + m2)
    gsc = jnp.concatenate(group_scores, 0)
    row8 = lax.broadcasted_iota(I32, (N_GROUPS, tm), 0)
    okf = jnp.zeros((N_GROUPS, tm), F32)
    for _ in range(TOPK_GROUPS):
        m = jnp.max(gsc, 0, keepdims=True)
        i = jnp.min(jnp.where(gsc == m, row8, N_GROUPS), 0, keepdims=True)
        hit = row8 == i
        okf = jnp.where(hit, 1.0, okf)
        gsc = jnp.where(hit, neg, gsc)
    masked = jnp.concatenate(
        [jnp.where(jnp.broadcast_to(okf[g:g + 1], (GROUP_SIZE, tm)) > 0.0,
                   choice[g * GROUP_SIZE:(g + 1) * GROUP_SIZE], neg)
         for g in range(N_GROUPS)], 0)
    row = lax.broadcasted_iota(I32, (e, tm), 0)
    idxs, ws = [], []
    multi = jnp.zeros((e, tm), F32)
    for _ in range(TOP_K):
        m = jnp.max(masked, 0, keepdims=True)
        i = jnp.min(jnp.where(masked == m, row, e), 0, keepdims=True)
        hit = row == i
        ws.append(jnp.sum(jnp.where(hit, scores, 0.0), 0, keepdims=True))
        idxs.append(i)
        multi = jnp.where(hit, 1.0, multi)
        masked = jnp.where(hit, neg, masked)
    a = lax.broadcasted_iota(I32, (tm, tm), 0)
    b = lax.broadcasted_iota(I32, (tm, tm), 1)
    upper = jnp.where(a < b, 1.0, 0.0).astype(BF16)
    before = _dot(multi.astype(BF16), upper) + carry
    ranks = [jnp.sum(jnp.where(row == i, before, 0.0), 0, keepdims=True).astype(I32) for i in idxs]
    new_carry = carry + jnp.sum(multi, 1, keepdims=True)
    wsum = ws[0]
    for w in ws[1:]:
        wsum = wsum + w
    ws = [w / wsum * ROUTED_SCALE for w in ws]
    return idxs, ws, ranks, new_carry


def _post_kernel(yr_ref, od_ref, g0_ref, g1_ref, h_ref, p_ref, wr_ref, wd_ref, wo_ref, l1g_ref,
                 l1b_ref, sg_ref, su_ref, sd_ref, pw_ref, pg_ref, rwt_ref, rb_ref,
                 h1_ref, base_ref, idx_ref, w_ref, rank_ref, cnt_ref, carry_ref):
    i = pl.program_id(0)

    @pl.when(i == 0)
    def _():
        carry_ref[...] = jnp.zeros_like(carry_ref)

    y_r = _dot(yr_ref[...], wr_ref[...])
    y_d = _dot(od_ref[...], wd_ref[...])
    merged = g0_ref[...].astype(F32) * y_r + g1_ref[...].astype(F32) * y_d
    mix = _dot(merged.astype(BF16), wo_ref[...])
    h1 = _layer_norm(DN_ALPHA * h_ref[...] + mix, l1g_ref[...], l1b_ref[...])
    packed = _pack_bf16_pairs(h1)
    for s in range(PACK_SUB):
        h1_ref[_col_block(0, h1.shape[0], s, PACK_SUB)] = packed[:, s * LANES:(s + 1) * LANES]
    hb = h1.astype(BF16)

    shared = _dot((_silu(_dot(hb, sg_ref[...])) * _dot(hb, su_ref[...])).astype(BF16), sd_ref[...])
    ple = _dot(p_ref[...].astype(BF16), pw_ref[...]) * jax.nn.sigmoid(_dot(hb, pg_ref[...]))
    base_ref[...] = DN_ALPHA * h1 + shared + ple

    h_lo = (h1 - hb.astype(F32)).astype(BF16)
    rw = rwt_ref[...]
    rw_hi = rw.astype(BF16)
    rw_lo = (rw - rw_hi.astype(F32)).astype(BF16)
    logits = _dot_nt(rw_hi, hb) + (_dot_nt(rw_hi, h_lo) + _dot_nt(rw_lo, hb))
    scores = jax.nn.sigmoid(logits)
    idxs, ws, ranks, new_carry = _route(scores, rb_ref[...], carry_ref[...])
    idx_ref[...] = jnp.concatenate(idxs, 0)
    w_ref[...] = jnp.concatenate(ws, 0)
    rank_ref[...] = jnp.concatenate(ranks, 0)
    carry_ref[...] = new_carry
    cnt_ref[...] = new_carry.astype(I32)


def _post_call(yr, od, proj, h, p2, wr, wd, wo, l1g, l1b, sg, su, sd, pw, pg, rwt, rb):
    n = h.shape[0]
    tm = POST_TM
    g0b = SEG_GATE // D_MODEL

    def full(a):
        return pl.BlockSpec(a.shape, lambda i: (0,) * a.ndim)

    row = lambda w: pl.BlockSpec((tm, w), lambda i: (i, 0))
    tok = lambda: pl.BlockSpec((TOP_K, tm), lambda i: (0, i))
    return pl.pallas_call(
        _post_kernel,
        name="post",
        grid=(n // tm,),
        in_specs=[
            row(RET_HEADS * RET_V), row(D_MODEL),
            pl.BlockSpec((tm, D_MODEL), lambda i: (i, g0b)),
            pl.BlockSpec((tm, D_MODEL), lambda i: (i, g0b + 1)),
            row(D_MODEL), row(PLE_DIM),
            full(wr), full(wd), full(wo), full(l1g), full(l1b), full(sg), full(su), full(sd),
            full(pw), full(pg), full(rwt), full(rb),
        ],
        out_specs=[pl.BlockSpec((tm * PACK_SUB, LANES), lambda i: (i, 0)), row(D_MODEL),
                   tok(), tok(), tok(), pl.BlockSpec((N_EXPERTS, 1), lambda i: (0, 0))],
        out_shape=[
            jax.ShapeDtypeStruct((n * PACK_SUB, LANES), jnp.uint32),
            jax.ShapeDtypeStruct((n, D_MODEL), F32),
            jax.ShapeDtypeStruct((TOP_K, n), I32),
            jax.ShapeDtypeStruct((TOP_K, n), F32),
            jax.ShapeDtypeStruct((TOP_K, n), I32),
            jax.ShapeDtypeStruct((N_EXPERTS, 1), I32),
        ],
        scratch_shapes=[pltpu.VMEM((N_EXPERTS, 1), F32)],
        compiler_params=pltpu.CompilerParams(
            dimension_semantics=("arbitrary",), vmem_limit_bytes=VMEM_LIMIT),
    )(yr, od, proj, proj, h, p2, wr, wd, wo, l1g, l1b, sg, su, sd, pw, pg, rwt, rb)


def _dest_kernel(idx_ref, rank_ref, ps_ref, o_ref):
    idx = idx_ref[...]
    ps = ps_ref[...]
    td = idx.shape[1]
    row = lax.broadcasted_iota(I32, (N_EXPERTS, td), 0)
    outs = []
    for k in range(TOP_K):
        sel = jnp.sum(jnp.where(row == idx[k:k + 1], ps, 0.0), 0, keepdims=True)
        outs.append(sel.astype(I32))
    o_ref[...] = jnp.concatenate(outs, 0) + rank_ref[...]


def _dest_call(idx, rank, pstart_f):
    n = idx.shape[1]
    td = min(DEST_T, n)
    spec = pl.BlockSpec((TOP_K, td), lambda i: (0, i))
    return pl.pallas_call(
        _dest_kernel,
        name="dest",
        grid=(n // td,),
        in_specs=[spec, spec, pl.BlockSpec((N_EXPERTS, 1), lambda i: (0, 0))],
        out_specs=spec,
        out_shape=jax.ShapeDtypeStruct((TOP_K, n), I32),
        compiler_params=pltpu.CompilerParams(dimension_semantics=("arbitrary",)),
    )(idx, rank, pstart_f)


def _expert_kernel(be_ref, nu_ref, wslot_ref, nexte_ref, rt_ref, rtn_ref, h_hbm, wg_hbm, wu_hbm,
                   wd_hbm, y_ref, buf_ref, wgf_ref, wuf_ref, wdf_ref, wgb_ref, wub_ref, wdb_ref,
                   sem, wsem):
    i = pl.program_id(0)
    g = EXPERT_BLOCK

    def weight_copies(e, s):
        return [pltpu.make_async_copy(src.at[e], dst.at[s], wsem.at[s])
                for src, dst in ((wg_hbm, wgf_ref), (wu_hbm, wuf_ref), (wd_hbm, wdf_ref))]

    def wait(s):
        pltpu.make_async_copy(_row_span(h_hbm, 0, g, PACK_SUB),
                              _row_span(buf_ref, s * g, g, PACK_SUB), sem.at[s]).wait()

    @pl.when(i == 0)
    def _():
        for j in range(GATHER_AHEAD):
            def body(r, c, j=j):
                pltpu.make_async_copy(_one_row(h_hbm, rt_ref[j, 0, r], PACK_SUB),
                                      _one_row(buf_ref, j * g + r, PACK_SUB),
                                      sem.at[j]).start(priority=GATHER_DMA_PRIORITY)
                return c
            lax.fori_loop(0, g, body, 0, unroll=8)

        for c in weight_copies(be_ref[0], 0):
            c.start(priority=WEIGHT_DMA_PRIORITY)

    used = i < nu_ref[0]
    first = jnp.logical_and(
        used, jnp.logical_or(i == 0, be_ref[i] != be_ref[jnp.maximum(i - 1, 0)]))
    for s in range(2):
        @pl.when(jnp.logical_and(first, wslot_ref[i] == s))
        def _(s=s):
            for c in weight_copies(be_ref[i], s):
                c.wait()
            wgb_ref[...] = wgf_ref[s].astype(BF16)
            wub_ref[...] = wuf_ref[s].astype(BF16)
            wdb_ref[...] = wdf_ref[s].astype(BF16)

            @pl.when(nexte_ref[i] >= 0)
            def _():
                for c in weight_copies(nexte_ref[i], 1 - s):
                    c.start(priority=WEIGHT_DMA_PRIORITY)

    def block(slot, more):
        wait(slot)
        halves = [_unpack_bf16_pairs(buf_ref[_col_block(slot * g, g, s, PACK_SUB)])
                  for s in range(PACK_SUB)]
        x = jnp.concatenate([lo for lo, _ in halves] + [hi for _, hi in halves],
                            axis=1).astype(BF16)
        if more:
            ahead = (slot + GATHER_AHEAD) % GATHER_SLOTS
            for r in range(g):
                pltpu.make_async_copy(
                    _one_row(h_hbm, rtn_ref[0, 0, r], PACK_SUB),
                    _one_row(buf_ref, ahead * g + r, PACK_SUB),
                    sem.at[ahead]).start(priority=r % DMA_QUEUES)
        act = _silu(_dot(x, wgb_ref[...])) * _dot(x, wub_ref[...])
        packed = _pack_bf16_pairs(_dot(act.astype(BF16), wdb_ref[...]))
        for s in range(PACK_SUB):
            y_ref[_col_block(0, g, s, PACK_SUB)] = packed[:, s * LANES:(s + 1) * LANES]

    more = i + GATHER_AHEAD < nu_ref[0]
    for slot in range(GATHER_SLOTS):
        in_slot = jnp.logical_and(i % GATHER_SLOTS == slot, used)
        pl.when(jnp.logical_and(in_slot, more))(functools.partial(block, slot, True))
        pl.when(jnp.logical_and(in_slot, jnp.logical_not(more)))(
            functools.partial(block, slot, False))

    @pl.when(jnp.logical_not(used))
    def _():
        y_ref[...] = jnp.zeros_like(y_ref)


def _expert_call(block_e, n_used, w_slot, next_e, row_tok3, h1, wg, wu, wd):
    nb = block_e.shape[0]
    g = EXPERT_BLOCK
    grid_spec = pltpu.PrefetchScalarGridSpec(
        num_scalar_prefetch=4,
        grid=(nb,),
        in_specs=[
            pl.BlockSpec((GATHER_AHEAD, 1, g), lambda i, *_: (0, 0, 0), memory_space=pltpu.SMEM),
            pl.BlockSpec((1, 1, g), lambda i, *_: (jnp.minimum(i + GATHER_AHEAD, nb - 1), 0, 0),
                         memory_space=pltpu.SMEM),
            pl.BlockSpec(memory_space=pl.ANY),
            pl.BlockSpec(memory_space=pl.ANY),
            pl.BlockSpec(memory_space=pl.ANY),
            pl.BlockSpec(memory_space=pl.ANY),
        ],
        out_specs=pl.BlockSpec((g * PACK_SUB, LANES), lambda i, *_: (i, 0)),
        scratch_shapes=[
            pltpu.VMEM((GATHER_SLOTS * g * PACK_SUB, LANES), jnp.uint32),
            pltpu.VMEM((2, D_MODEL, D_EXPERT), F32),
            pltpu.VMEM((2, D_MODEL, D_EXPERT), F32),
            pltpu.VMEM((2, D_EXPERT, D_MODEL), F32),
            pltpu.VMEM((D_MODEL, D_EXPERT), BF16),
            pltpu.VMEM((D_MODEL, D_EXPERT), BF16),
            pltpu.VMEM((D_EXPERT, D_MODEL), BF16),
            pltpu.SemaphoreType.DMA((GATHER_SLOTS,)),
            pltpu.SemaphoreType.DMA((2,)),
        ],
    )
    assert nb > GATHER_AHEAD
    return pl.pallas_call(
        _expert_kernel,
        name="experts",
        grid_spec=grid_spec,
        out_shape=jax.ShapeDtypeStruct((nb * g * PACK_SUB, LANES), jnp.uint32),
        compiler_params=pltpu.CompilerParams(
            dimension_semantics=("arbitrary",), vmem_limit_bytes=VMEM_LIMIT),
    )(block_e, n_used, w_slot, next_e, row_tok3, row_tok3, h1, wg, wu, wd)


def _combine_kernel(d_ref, dn_ref, y_hbm, w_ref, base_ref, g_ref, b_ref, o_ref, buf_ref, sem):
    i = pl.program_id(0)
    nt = pl.num_programs(0)
    tt = COMB_T

    def plane(s, k):
        return (s * TOP_K + k) * tt

    def wait(s):
        for k in range(TOP_K):
            pltpu.make_async_copy(_row_span(y_hbm, 0, tt, PACK_SUB),
                                  _row_span(buf_ref, plane(s, k), tt, PACK_SUB), sem.at[s]).wait()

    @pl.when(i == 0)
    def _():
        for j in range(GATHER_AHEAD):
            def body(t, c, j=j):
                for k in range(TOP_K):
                    pltpu.make_async_copy(
                        _one_row(y_hbm, d_ref[j, k, t], PACK_SUB),
                        _one_row(buf_ref, plane(j, k) + t, PACK_SUB),
                        sem.at[j]).start(priority=k % DMA_QUEUES)
                return c
            lax.fori_loop(0, tt, body, 0)

    def tile(slot):
        wait(slot)
        ahead = (slot + GATHER_AHEAD) % GATHER_SLOTS
        for t in range(tt):
            for k in range(TOP_K):
                pltpu.make_async_copy(_one_row(y_hbm, dn_ref[0, k, t], PACK_SUB),
                                      _one_row(buf_ref, plane(ahead, k) + t, PACK_SUB),
                                      sem.at[ahead]).start(priority=k % DMA_QUEUES)
        w = w_ref[...]
        lows, highs = [], []
        for s in range(PACK_SUB):
            lo_acc = hi_acc = None
            for k in range(TOP_K):
                lo, hi = _unpack_bf16_pairs(
                    buf_ref[_col_block(plane(slot, k), tt, s, PACK_SUB)])
                wk = w[:, k:k + 1]
                lo_acc = wk * lo if lo_acc is None else lo_acc + wk * lo
                hi_acc = wk * hi if hi_acc is None else hi_acc + wk * hi
            lows.append(lo_acc)
            highs.append(hi_acc)
        routed = jnp.concatenate(lows + highs, axis=1)
        o_ref[...] = _layer_norm(base_ref[...] + routed, g_ref[...], b_ref[...])

        @pl.when(i == nt - 1)
        def _():
            for j in range(1, GATHER_AHEAD + 1):
                wait((slot + j) % GATHER_SLOTS)

    for slot in range(GATHER_SLOTS):
        pl.when(i % GATHER_SLOTS == slot)(functools.partial(tile, slot))


def _combine_call(dest3, y, w_t, base, ln_g, ln_b):
    n = base.shape[0]
    tt = COMB_T
    nt = n // tt
    return pl.pallas_call(
        _combine_kernel,
        name="combine",
        grid=(nt,),
        in_specs=[
            pl.BlockSpec((GATHER_AHEAD, TOP_K, tt), lambda i: (0, 0, 0), memory_space=pltpu.SMEM),
            pl.BlockSpec((1, TOP_K, tt), lambda i: (jnp.minimum(i + GATHER_AHEAD, nt - 1), 0, 0),
                         memory_space=pltpu.SMEM),
            pl.BlockSpec(memory_space=pl.ANY),
            pl.BlockSpec((tt, TOP_K), lambda i: (i, 0)),
            pl.BlockSpec((tt, D_MODEL), lambda i: (i, 0)),
            pl.BlockSpec((1, D_MODEL), lambda i: (0, 0)),
            pl.BlockSpec((1, D_MODEL), lambda i: (0, 0)),
        ],
        out_specs=pl.BlockSpec((tt, D_MODEL), lambda i: (i, 0)),
        out_shape=jax.ShapeDtypeStruct((n, D_MODEL), F32),
        scratch_shapes=[
            pltpu.VMEM((GATHER_SLOTS * TOP_K * tt * PACK_SUB, LANES), jnp.uint32),
            pltpu.SemaphoreType.DMA((GATHER_SLOTS,)),
        ],
        compiler_params=pltpu.CompilerParams(
            dimension_semantics=("arbitrary",), vmem_limit_bytes=VMEM_LIMIT),
    )(dest3, dest3, y, w_t, base, ln_g, ln_b)


def _xpos_tables(seq):
    half = RET_QK // 2
    pos = jnp.arange(seq, dtype=F32)
    theta = 1.0 / (XPOS_BASE ** jnp.linspace(0.0, 1.0, half, dtype=F32))
    ang = pos[:, None] * theta[None, :]
    cos = jnp.cos(ang)
    sin = jnp.sin(ang)
    return jnp.concatenate([cos, cos], -1), jnp.concatenate([-sin, sin], -1)


def _decay_tables():
    c = RET_CHUNK
    log_gamma = jnp.log(1.0 - 2.0 ** (-5.0 - jnp.arange(RET_HEADS, dtype=F32)))
    idx = jnp.arange(c, dtype=F32)
    rel = idx[:, None] - idx[None, :]
    inner = jnp.where(rel >= 0, jnp.exp(jnp.maximum(rel, 0.0)[None] * log_gamma[:, None, None]), 0.0)
    q_dec = jnp.exp((idx + 1.0)[None, :] * log_gamma[:, None])
    k_dec = jnp.exp((c - 1.0 - idx)[None, :] * log_gamma[:, None])
    c_dec = jnp.exp(c * log_gamma)
    cdec = jnp.broadcast_to(c_dec[:, None, None], (RET_HEADS, 1, RET_V))
    return inner, q_dec[:, :, None], k_dec[:, :, None], cdec


def _t5_bucket(n):
    max_exact = REL_BUCKETS // 2
    nf = jnp.maximum(n, 1).astype(F32)
    large = max_exact + (jnp.log(nf / max_exact) / math.log(REL_MAX_DIST / max_exact)
                         * (REL_BUCKETS - max_exact)).astype(I32)
    large = jnp.minimum(large, REL_BUCKETS - 1)
    return jnp.where(n < max_exact, n, large)


def _bias_tables(rel_bias, t):
    c = REL_MAX_DIST
    nb = t // c
    far = rel_bias[REL_BUCKETS - 1].astype(F32)
    d = jnp.arange(c)[None, :] - jnp.arange(c)[:, None]
    def lookup(dist):
        onehot = jax.nn.one_hot(_t5_bucket(dist), REL_BUCKETS, dtype=F32)
        return jnp.einsum('ijk,kh->ijh', onehot, rel_bias.astype(F32),
                          precision=lax.Precision.HIGHEST)

    near0 = (lookup(jnp.maximum(d, 0)) - far) * LOG2E
    near0 = jnp.where((d >= 0)[:, :, None], near0, MASK_VALUE)
    near1 = (lookup(d + c) - far) * LOG2E
    tile0 = jnp.tile(near0, (nb, nb, 1))
    tile1 = jnp.tile(near1, (nb, nb, 1))
    kb = (jnp.arange(t) // c)[:, None, None]
    qb = (jnp.arange(t) // c)[None, :, None]
    diag = jnp.where(qb == kb, tile0,
                     jnp.where(qb == kb + 1, tile1, jnp.where(qb > kb, 0.0, MASK_VALUE)))
    left = jnp.where((qb == 0) & (kb == nb - 1), tile1, 0.0)
    return jnp.transpose(jnp.stack([diag, left], 0), (3, 0, 1, 2))


def kernel(x, p, ln_in_g, ln_in_b, rel_bias, w_in, ret_gn_g, lambda_q1, lambda_k1, lambda_q2,
           lambda_k2, diff_subln_g, w_ret_o, w_diff_o, w_o, ln1_g, ln1_b, router_w, router_bias,
           exp_w_gate, exp_w_up, exp_w_down, sh_w_gate, sh_w_up, sh_w_down, ple_w, ple_gate_w,
           ln2_g, ln2_b):
    batch, seq, d = x.shape
    n = batch * seq
    assert d == D_MODEL and seq % (ATTN_T * ATTN_KSUB) == 0 and seq % PROJ_TM == 0
    assert n % min(DEST_T, n) == 0 and n % COMB_T == 0 and n % POST_TM == 0
    x2 = x.reshape(n, d)
    row = lambda a: a.reshape(1, -1).astype(F32)

    cosf, sinf = _xpos_tables(seq)
    inner, qdec, kdec, cdec = _decay_tables()
    h, proj = _proj_call(x2, row(ln_in_g), row(ln_in_b), cosf, sinf, w_in[0].astype(BF16), seq)

    yr = _ret_call(proj, inner, qdec, kdec, cdec,
                   ret_gn_g[0].reshape(RET_HEADS, 1, RET_V).astype(F32), batch, seq)

    od = _attn_call(proj, _bias_tables(rel_bias, ATTN_T), row(lambda_q1[0]), row(lambda_k1[0]),
                    row(lambda_q2[0]), row(lambda_k2[0]),
                    diff_subln_g[0].reshape(-1, 1).astype(F32), batch, seq)

    h1, base, idx, w, rank, counts = _post_call(
        yr, od, proj, h, p[0].reshape(n, PLE_DIM),
        w_ret_o[0].astype(BF16), w_diff_o[0].astype(BF16), w_o[0].astype(BF16),
        row(ln1_g[0]), row(ln1_b[0]),
        sh_w_gate[0].astype(BF16), sh_w_up[0].astype(BF16), sh_w_down[0].astype(BF16),
        ple_w[0].astype(BF16), ple_gate_w[0].astype(BF16),
        router_w[0].T.astype(F32), router_bias[0].reshape(N_EXPERTS, 1).astype(F32))

    g = EXPERT_BLOCK
    counts = counts.reshape(N_EXPERTS)
    padded = (counts + g - 1) // g * g
    pend = jnp.cumsum(padded)
    pstart = pend - padded
    n_rows = -(-(n * TOP_K + N_EXPERTS * (g - 1)) // g) * g
    n_blocks = n_rows // g
    block_start = jnp.arange(n_blocks, dtype=I32) * g
    block_e = jnp.minimum(
        jnp.sum((pend[None, :] <= block_start[:, None]).astype(I32), axis=1),
        N_EXPERTS - 1).astype(I32)

    dest = _dest_call(idx, rank, pstart.astype(F32).reshape(N_EXPERTS, 1))
    tok = jnp.broadcast_to(jnp.arange(n, dtype=I32)[None, :], (TOP_K, n))
    row_tok = jnp.zeros((n_rows,), I32).at[dest.reshape(-1)].set(
        tok.reshape(-1), unique_indices=True, mode='drop')

    n_used = (pend[N_EXPERTS - 1:] // g).astype(I32)
    assert (n * TOP_K) // g >= GATHER_AHEAD
    present = padded > 0
    ids = jnp.arange(N_EXPERTS, dtype=I32)
    at_or_after = lax.cummin(jnp.where(present, ids, N_EXPERTS)[::-1])[::-1]
    after = jnp.concatenate([at_or_after[1:], jnp.full((1,), N_EXPERTS, I32)])
    after = jnp.where(after >= N_EXPERTS, -1, after)
    order = jnp.cumsum(present.astype(I32)) - 1
    w_slot = (jnp.take(order, block_e) % 2).astype(I32)
    next_e = jnp.take(after, block_e).astype(I32)
    y = _expert_call(block_e, n_used, w_slot, next_e, row_tok.reshape(n_blocks, 1, g), h1,
                     exp_w_gate[0], exp_w_up[0], exp_w_down[0])

    dest3 = dest.reshape(TOP_K, n // COMB_T, COMB_T).transpose(1, 0, 2)
    out = _combine_call(dest3, y, w.T, base, row(ln2_g[0]), row(ln2_b[0]))
    return out.reshape(batch, seq, d)
```
